```python
import math
import jax, jax.numpy as jnp
from jax import lax
import numpy as np

D_MODEL = 2048
BATCH = 4
SEQ = 2048
DEPTH = 4

N_MIXERS = 3
N_A = (DEPTH + 2) // 3
N_B = (DEPTH + 1) // 3
N_C = DEPTH // 3
N_A_VRES = max(N_A - 1, 0)
N_META = 16
RMS_EPS = 1e-6

RWKV_HEAD = 64
RWKV_HEADS = D_MODEL // RWKV_HEAD
DECAY_LORA = max(32, int(round(1.8 * D_MODEL ** 0.5 / 32)) * 32)
AAA_LORA = max(32, int(round(1.8 * D_MODEL ** 0.5 / 32)) * 32)
MV_LORA = max(32, int(round(1.3 * D_MODEL ** 0.5 / 32)) * 32)
GATE_LORA = max(32, int(round(0.6 * D_MODEL ** 0.8 / 32)) * 32)
LNX_EPS = 1e-5 * RWKV_HEAD

MLA_HEADS = D_MODEL // 128
Q_LORA = D_MODEL // 4
KV_LORA = D_MODEL // 4
NOPE_D = 128
ROPE_D = 64
V_D = 128
ROPE_THETA = 10000.0
Q_BLOCK = 128

POOL_WINDOWS = (2, 4, 8, 16)
POOL_GROUP = D_MODEL // len(POOL_WINDOWS)

FFN_HIDDEN = -(-8 * D_MODEL // (3 * 256)) * 256

kernel_name = 'hybrid_rwkv7_mla_pool_trunk'


def _rmsnorm(x, g):
    x32 = x.astype(jnp.float32)
    y = x32 * lax.rsqrt(jnp.mean(x32 * x32, axis=-1, keepdims=True) + RMS_EPS)
    return (y * g.astype(jnp.float32)).astype(x.dtype)


def _swiglu(x, w_gate, w_up, w_down):
    return (jax.nn.silu(x @ w_gate) * (x @ w_up)) @ w_down


def _token_shift(x):
    return jnp.pad(x, ((0, 0), (1, 0), (0, 0)))[:, :-1]


def _wkv7_scan(r, w, k, v, a, b):
    B, L, H, N = r.shape
    tm = lambda t: jnp.moveaxis(t, 1, 0)

    def step(S, inp):
        r_t, w_t, k_t, v_t, a_t, b_t = inp
        sa = jnp.einsum('bhij,bhj->bhi', S, a_t)
        S = (S * w_t[:, :, None, :] + sa[..., None] * b_t[:, :, None, :]
             + v_t[..., None] * k_t[:, :, None, :])
        return S, jnp.einsum('bhij,bhj->bhi', S, r_t)

    S0 = jnp.zeros((B, H, N, N), jnp.float32)
    _, ys = lax.scan(step, S0, (tm(r), tm(w), tm(k), tm(v), tm(a), tm(b)))
    return jnp.moveaxis(ys, 0, 1)


def _rwkv7_time_mix(x, v_first, mix, w0, w1, w2, a0, a1, a2, vres, g1, g2,
                    k_k, k_a, r_k, w_rkv, w_o, lnx_g, lnx_b):
    B, L, D = x.shape
    H, N = RWKV_HEADS, RWKV_HEAD
    xx = _token_shift(x) - x
    xs = x[None] + xx[None] * mix[:, None, None, :]
    xr, xw, xk, xv, xa, xg = xs[0], xs[1], xs[2], xs[3], xs[4], xs[5]
    rkv = jnp.einsum('sbld,sde->sble', jnp.stack([xr, xk, xv]), w_rkv)
    r, k, v = rkv[0], rkv[1], rkv[2]
    w = -jax.nn.softplus(-(w0 + jnp.tanh(xw @ w1) @ w2)) - 0.5
    a = jax.nn.sigmoid(a0 + (xa @ a1) @ a2)
    g = jax.nn.sigmoid(xg @ g1) @ g2
    if vres is None:
        v_first = v
    else:
        v0, v1, v2 = vres
        v = v + (v_first - v) * jax.nn.sigmoid(v0 + (xv @ v1) @ v2)
    hd = lambda t: t.reshape(B, L, H, N).astype(jnp.float32)
    r_h, w_h, k_h, v_h, a_h = hd(r), hd(w), hd(k), hd(v), hd(a)
    kk = k_h * k_k.astype(jnp.float32).reshape(H, N)
    kk = kk / jnp.maximum(jnp.sqrt(jnp.sum(kk * kk, axis=-1, keepdims=True)), 1e-12)
    k_h = k_h * (1.0 + (a_h - 1.0) * k_a.astype(jnp.float32).reshape(H, N))
    decay = jnp.exp(-jnp.exp(w_h))
    y = _wkv7_scan(r_h, decay, k_h, v_h, -kk, kk * a_h)
    mu = jnp.mean(y, axis=-1, keepdims=True)
    var = jnp.mean((y - mu) ** 2, axis=-1, keepdims=True)
    y = ((y - mu) * lax.rsqrt(var + LNX_EPS)).reshape(B, L, D)
    y = y * lnx_g.astype(jnp.float32) + lnx_b.astype(jnp.float32)
    bonus = jnp.sum(r_h * k_h * r_k.astype(jnp.float32), axis=-1, keepdims=True) * v_h
    y = y + bonus.reshape(B, L, D)
    out = (y.astype(x.dtype) * g) @ w_o
    return out, v_first


def _rope_tables(pos, d):
    inv_freq = 1.0 / (ROPE_THETA ** (jnp.arange(0, d, 2, dtype=jnp.float32) / d))
    ang = pos.astype(jnp.float32)[:, None] * inv_freq[None, :]
    return jnp.cos(ang), jnp.sin(ang)


def _rope(x, cos, sin):
    half = x.shape[-1] // 2
    x1, x2 = x[..., :half], x[..., half:]
    cos = cos.astype(x.dtype)
    sin = sin.astype(x.dtype)
    return jnp.concatenate([x1 * cos - x2 * sin, x2 * cos + x1 * sin], axis=-1)


def _mla(x, pos, w_in, q_norm, w_uq, kv_norm, w_ukv, w_o):
    B, L, D = x.shape
    H = MLA_HEADS
    lat = x @ w_in
    c_q = _rmsnorm(lat[..., :Q_LORA], q_norm)
    c_kv = _rmsnorm(lat[..., Q_LORA:Q_LORA + KV_LORA], kv_norm)
    k_pe = lat[..., Q_LORA + KV_LORA:]
    q = (c_q @ w_uq).reshape(B, L, H, NOPE_D + ROPE_D)
    q_nope, q_pe = q[..., :NOPE_D], q[..., NOPE_D:]
    kv = (c_kv @ w_ukv).reshape(B, L, H, NOPE_D + V_D)
    k_nope, v = kv[..., :NOPE_D], kv[..., NOPE_D:]
    cos, sin = _rope_tables(pos, ROPE_D)
    q_pe = _rope(q_pe, cos[:, None, :], sin[:, None, :])
    k_pe = _rope(k_pe, cos, sin)
    n_blk = -(-L // Q_BLOCK)
    Lp = n_blk * Q_BLOCK
    pad = lambda t: jnp.pad(t, [(0, 0), (0, Lp - L)] + [(0, 0)] * (t.ndim - 2))
    q_nope, q_pe, k_nope, k_pe, v = pad(q_nope), pad(q_pe), pad(k_nope), pad(k_pe), pad(v)
    kpos = jnp.arange(Lp)
    scale = (NOPE_D + ROPE_D) ** -0.5
    qb = lambda t: jnp.moveaxis(t.reshape(B, n_blk, Q_BLOCK, *t.shape[2:]), 1, 0)

    def block(args):
        qn, qp, i = args
        s = (jnp.einsum('bqhd,bkhd->bhqk', qn, k_nope)
             + jnp.einsum('bqhd,bkd->bhqk', qp, k_pe)).astype(jnp.float32) * scale
        qpos = i * Q_BLOCK + jnp.arange(Q_BLOCK)
        s = jnp.where(kpos[None, :] <= qpos[:, None], s, -jnp.inf)
        p = jax.nn.softmax(s, axis=-1).astype(v.dtype)
        return jnp.einsum('bhqk,bkhd->bqhd', p, v)

    o = lax.map(block, (qb(q_nope), qb(q_pe), jnp.arange(n_blk)))
    o = jnp.moveaxis(o, 0, 1).reshape(B, Lp, H * V_D)[:, :L]
    return o @ w_o


def _pool_mix(x, w_grp, scale):
    B, L, D = x.shape
    xf = x.astype(jnp.float32)
    c0 = jnp.pad(jnp.cumsum(xf, axis=1), ((0, 0), (1, 0), (0, 0)))
    t = np.arange(L)
    outs = []
    for gi, win in enumerate(POOL_WINDOWS):
        lo = np.maximum(t + 1 - win, 0)
        cnt = (t + 1 - lo).astype(np.float32)
        cg = c0[..., gi * POOL_GROUP:(gi + 1) * POOL_GROUP]
        outs.append((cg[:, 1:] - cg[:, lo]) / cnt[None, :, None])
    pooled = jnp.stack(outs, axis=2) - xf.reshape(B, L, len(POOL_WINDOWS), POOL_GROUP)
    y = jnp.einsum('blgc,gce->blge', pooled.astype(x.dtype), w_grp).reshape(B, L, D)
    return y * scale


def setup_inputs(seed: int = 0) -> dict:
    key = jax.random.key(seed)
    ks = iter(jax.random.split(key, 64))
    nrm = lambda shape, s: s * jax.random.normal(next(ks), shape, jnp.float32)
    gain = lambda shape: 1.0 + nrm(shape, 0.05)
    D, F, H = D_MODEL, FFN_HIDDEN, MLA_HEADS
    inp = {}
    inp['x'] = nrm((BATCH, SEQ, D), 1.0)
    inp['meta_tokens'] = nrm((N_META, D), 1.0)
    inp['norm_mix_pre'] = gain((DEPTH, D))
    inp['norm_mix_post'] = gain((DEPTH, D))
    inp['norm_ffn_pre'] = gain((DEPTH, D))
    inp['norm_ffn_post'] = gain((DEPTH, D))
    inp['ffn_w_gate'] = nrm((DEPTH, D, F), D ** -0.5)
    inp['ffn_w_up'] = nrm((DEPTH, D, F), D ** -0.5)
    inp['ffn_w_down'] = nrm((DEPTH, F, D), F ** -0.5)
    inp['rwkv_mix'] = jax.random.uniform(next(ks), (N_A, 6, D), jnp.float32)
    inp['rwkv_w0'] = jax.random.uniform(next(ks), (N_A, D), jnp.float32, -6.0, -1.0)
    inp['rwkv_w1'] = nrm((N_A, D, DECAY_LORA), D ** -0.5)
    inp['rwkv_w2'] = nrm((N_A, DECAY_LORA, D), 0.3 * DECAY_LORA ** -0.5)
    inp['rwkv_a0'] = nrm((N_A, D), 0.1)
    inp['rwkv_a1'] = nrm((N_A, D, AAA_LORA), D ** -0.5)
    inp['rwkv_a2'] = nrm((N_A, AAA_LORA, D), AAA_LORA ** -0.5)
    inp['rwkv_v0'] = nrm((N_A_VRES, D), 0.1)
    inp['rwkv_v1'] = nrm((N_A_VRES, D, MV_LORA), D ** -0.5)
    inp['rwkv_v2'] = nrm((N_A_VRES, MV_LORA, D), MV_LORA ** -0.5)
    inp['rwkv_g1'] = nrm((N_A, D, GATE_LORA), D ** -0.5)
    inp['rwkv_g2'] = nrm((N_A, GATE_LORA, D), GATE_LORA ** -0.5)
    inp['rwkv_k_k'] = 0.85 + nrm((N_A, D), 0.05)
    inp['rwkv_k_a'] = 1.0 + nrm((N_A, D), 0.05)
    inp['rwkv_r_k'] = nrm((N_A, RWKV_HEADS, RWKV_HEAD), 0.1)
    inp['rwkv_w_rkv'] = nrm((N_A, 3, D, D), D ** -0.5)
    inp['rwkv_w_o'] = nrm((N_A, D, D), D ** -0.5)
    inp['rwkv_lnx_g'] = gain((N_A, D))
    inp['rwkv_lnx_b'] = nrm((N_A, D), 0.02)
    inp['mla_w_in'] = nrm((N_B, D, Q_LORA + KV_LORA + ROPE_D), D ** -0.5)
    inp['mla_q_norm'] = gain((N_B, Q_LORA))
    inp['mla_w_uq'] = nrm((N_B, Q_LORA, H * (NOPE_D + ROPE_D)), Q_LORA ** -0.5)
    inp['mla_kv_norm'] = gain((N_B, KV_LORA))
    inp['mla_w_ukv'] = nrm((N_B, KV_LORA, H * (NOPE_D + V_D)), KV_LORA ** -0.5)
    inp['mla_w_o'] = nrm((N_B, H * V_D, D), (H * V_D) ** -0.5)
    inp['pool_w'] = nrm((N_C, len(POOL_WINDOWS), POOL_GROUP, POOL_GROUP), POOL_GROUP ** -0.5)
    inp['pool_scale'] = 1.0 + nrm((N_C, D), 0.1)
    return inp


def reference(x, meta_tokens, norm_mix_pre, norm_mix_post, norm_ffn_pre, norm_ffn_post,
              ffn_w_gate, ffn_w_up, ffn_w_down,
              rwkv_mix, rwkv_w0, rwkv_w1, rwkv_w2, rwkv_a0, rwkv_a1, rwkv_a2,
              rwkv_v0, rwkv_v1, rwkv_v2, rwkv_g1, rwkv_g2, rwkv_k_k, rwkv_k_a, rwkv_r_k,
              rwkv_w_rkv, rwkv_w_o, rwkv_lnx_g, rwkv_lnx_b,
              mla_w_in, mla_q_norm, mla_w_uq, mla_kv_norm, mla_w_ukv, mla_w_o,
              pool_w, pool_scale):
    B = x.shape[0]
    meta = jnp.broadcast_to(meta_tokens.astype(x.dtype)[None], (B, N_META, D_MODEL))
    h = jnp.concatenate([meta, x], axis=1)
    L = h.shape[1]
    pos = jnp.arange(L, dtype=jnp.int32)
    v_first = None
    for i in range(DEPTH):
        hn = _rmsnorm(h, norm_mix_pre[i])
        kind, j = i % N_MIXERS, i // N_MIXERS
        if kind == 0:
            vres = None if j == 0 else (rwkv_v0[j - 1], rwkv_v1[j - 1], rwkv_v2[j - 1])
            mix_out, v_first = _rwkv7_time_mix(
                hn, v_first, rwkv_mix[j], rwkv_w0[j], rwkv_w1[j], rwkv_w2[j],
                rwkv_a0[j], rwkv_a1[j], rwkv_a2[j], vres, rwkv_g1[j], rwkv_g2[j],
                rwkv_k_k[j], rwkv_k_a[j], rwkv_r_k[j], rwkv_w_rkv[j], rwkv_w_o[j],
                rwkv_lnx_g[j], rwkv_lnx_b[j])
        elif kind == 1:
            mix_out = _mla(hn, pos, mla_w_in[j], mla_q_norm[j], mla_w_uq[j],
                           mla_kv_norm[j], mla_w_ukv[j], mla_w_o[j])
        else:
            mix_out = _pool_mix(hn, pool_w[j], pool_scale[j])
        h = h + _rmsnorm(mix_out, norm_mix_post[i])
        f = _swiglu(_rmsnorm(h, norm_ffn_pre[i]), ffn_w_gate[i], ffn_w_up[i], ffn_w_down[i])
        h = h + _rmsnorm(f, norm_ffn_post[i])
    return h[:, N_META:]
```

```python
import functools

import jax
import jax.numpy as jnp
from jax import lax
from jax.experimental import pallas as pl
from jax.experimental.pallas import tpu as pltpu

F32 = jnp.float32
BF16 = jnp.bfloat16

N_META = 16
RMS_EPS = 1e-6
RWKV_HEAD = 64
LNX_EPS = 1e-5 * RWKV_HEAD
MLA_HEADS = 16
Q_LORA = 512
KV_LORA = 512
NOPE_D = 128
ROPE_D = 64
V_D = 128
ROPE_THETA = 10000.0
POOL_WINDOWS = (2, 4, 8, 16)

LANES = 128
SEQ_ALIGN = 128
WKV_CHUNK = 64
WKV_HEADS_PER_GROUP = 4
WKV_GROUP = WKV_HEADS_PER_GROUP * RWKV_HEAD
ATTN_Q_TILE = 256
VMEM_LIMIT = 56 * 1024 * 1024


def _cparams(sem):
    return pltpu.CompilerParams(dimension_semantics=sem, vmem_limit_bytes=VMEM_LIMIT)


def _dot(a, b):
    return jnp.dot(a, b, preferred_element_type=F32)


def _dot_nt(a, b):
    return lax.dot_general(a, b, (((1,), (1,)), ((), ())), preferred_element_type=F32)


def _dot_tn(a, b):
    return lax.dot_general(a, b, (((0,), (0,)), ((), ())), preferred_element_type=F32)


def _rms(x, g, eps=RMS_EPS):
    ms = jnp.mean(x * x, axis=-1, keepdims=True)
    return x * lax.rsqrt(ms + eps) * g


def _sigmoid(x):
    return 1.0 / (1.0 + jnp.exp(-x))


def _row_tile(m):
    for t in (512, 256, 128, 64, 32, 16):
        if m % t == 0:
            return t
    raise ValueError(f"unsupported row count {m}")


def _prenorm_kernel(h_ref, g_ref, o_ref):
    o_ref[...] = _rms(h_ref[...], g_ref[...]).astype(o_ref.dtype)


def _prenorm(h2, g, dtype):
    m, d = h2.shape
    bm = _row_tile(m)
    return pl.pallas_call(
        _prenorm_kernel,
        out_shape=jax.ShapeDtypeStruct((m, d), dtype),
        grid=(m // bm,),
        in_specs=[pl.BlockSpec((bm, d), lambda i: (i, 0)),
                  pl.BlockSpec((1, d), lambda i: (0, 0))],
        out_specs=pl.BlockSpec((bm, d), lambda i: (i, 0)),
        compiler_params=_cparams(("parallel",)),
        name="prenorm",
    )(h2, g.reshape(1, d))


def _postnorm_kernel(h_ref, y_ref, g_ref, o_ref):
    o_ref[...] = h_ref[...] + _rms(y_ref[...].astype(F32), g_ref[...])


def _postnorm_residual(h2, y2, g):
    m, d = h2.shape
    bm = _row_tile(m)
    return pl.pallas_call(
        _postnorm_kernel,
        out_shape=jax.ShapeDtypeStruct((m, d), F32),
        grid=(m // bm,),
        in_specs=[pl.BlockSpec((bm, d), lambda i: (i, 0)),
                  pl.BlockSpec((bm, d), lambda i: (i, 0)),
                  pl.BlockSpec((1, d), lambda i: (0, 0))],
        out_specs=pl.BlockSpec((bm, d), lambda i: (i, 0)),
        compiler_params=_cparams(("parallel",)),
        name="postnorm_residual",
    )(h2, y2, g.reshape(1, d))


def _proj_residual_kernel(x_ref, w_ref, h_ref, g_ref, o_ref):
    y = _dot(x_ref[...], w_ref[...])
    o_ref[...] = h_ref[...] + _rms(y, g_ref[...])


def _proj_residual(x2, w, h2, g):
    m, k = x2.shape
    d = w.shape[1]
    bm = _row_tile(m)
    return pl.pallas_call(
        _proj_residual_kernel,
        out_shape=jax.ShapeDtypeStruct((m, d), F32),
        grid=(m // bm,),
        in_specs=[pl.BlockSpec((bm, k), lambda i: (i, 0)),
                  pl.BlockSpec((k, d), lambda i: (0, 0)),
                  pl.BlockSpec((bm, d), lambda i: (i, 0)),
                  pl.BlockSpec((1, d), lambda i: (0, 0))],
        out_specs=pl.BlockSpec((bm, d), lambda i: (i, 0)),
        compiler_params=_cparams(("parallel",)),
        name="proj_residual",
    )(x2, w, h2, g.reshape(1, d))


def _ffn_kernel(h_ref, gpre_ref, gpost_ref, wg_ref, wu_ref, wd_ref, o_ref, xn_ref, acc_ref):
    f = pl.program_id(1)

    @pl.when(f == 0)
    def _():
        xn_ref[...] = _rms(h_ref[...], gpre_ref[...]).astype(BF16)
        acc_ref[...] = jnp.zeros_like(acc_ref)

    x = xn_ref[...]
    gate = _dot(x, wg_ref[...])
    up = _dot(x, wu_ref[...])
    mid = (gate * _sigmoid(gate) * up).astype(BF16)
    acc_ref[...] += _dot(mid, wd_ref[...])

    @pl.when(f == pl.num_programs(1) - 1)
    def _():
        o_ref[...] = h_ref[...] + _rms(acc_ref[...], gpost_ref[...])


def _ffn(h2, gpre, gpost, wg, wu, wd):
    m, d = h2.shape
    f = wg.shape[1]
    bm = _row_tile(m)
    bf = 512 if f % 512 == 0 else f
    return pl.pallas_call(
        _ffn_kernel,
        out_shape=jax.ShapeDtypeStruct((m, d), F32),
        grid=(m // bm, f // bf),
        in_specs=[pl.BlockSpec((bm, d), lambda i, j: (i, 0)),
                  pl.BlockSpec((1, d), lambda i, j: (0, 0)),
                  pl.BlockSpec((1, d), lambda i, j: (0, 0)),
                  pl.BlockSpec((d, bf), lambda i, j: (0, j)),
                  pl.BlockSpec((d, bf), lambda i, j: (0, j)),
                  pl.BlockSpec((bf, d), lambda i, j: (j, 0))],
        out_specs=pl.BlockSpec((bm, d), lambda i, j: (i, 0)),
        scratch_shapes=[pltpu.VMEM((bm, d), BF16), pltpu.VMEM((bm, d), F32)],
        compiler_params=_cparams(("parallel", "arbitrary")),
        name="ffn",
    )(h2, gpre.reshape(1, d), gpost.reshape(1, d), wg, wu, wd)


def _shift_kernel(h_ref, halo_ref, g_ref, hn_ref, xx_ref):
    i = pl.program_id(1)
    g = g_ref[...]
    hn = _rms(h_ref[0], g)
    prev = _rms(halo_ref[0], g)[7:8]
    prev = jnp.where(i == 0, 0.0, prev)
    rolled = pltpu.roll(hn, 1, axis=0)
    row = lax.broadcasted_iota(jnp.int32, hn.shape, 0)
    shifted = jnp.where(row == 0, prev, rolled)
    hn_ref[0] = hn.astype(BF16)
    xx_ref[0] = (shifted - hn).astype(BF16)


def _norm_shift(h3, g):
    b, lp, d = h3.shape
    tl = lp // 4 if lp % 64 == 0 else lp
    halo_blocks = tl // 8
    out = jax.ShapeDtypeStruct((b, lp, d), BF16)
    return pl.pallas_call(
        _shift_kernel,
        out_shape=(out, out),
        grid=(b, lp // tl),
        in_specs=[pl.BlockSpec((1, tl, d), lambda bi, i: (bi, i, 0)),
                  pl.BlockSpec((1, 8, d), lambda bi, i: (bi, jnp.maximum(i * halo_blocks - 1, 0), 0)),
                  pl.BlockSpec((1, d), lambda bi, i: (0, 0))],
        out_specs=(pl.BlockSpec((1, tl, d), lambda bi, i: (bi, i, 0)),
                   pl.BlockSpec((1, tl, d), lambda bi, i: (bi, i, 0))),
        compiler_params=_cparams(("parallel", "parallel")),
        name="norm_shift",
    )(h3, h3, g.reshape(1, d))


def _rkv_kernel(hn_ref, xx_ref, mix_ref, w_ref, o_ref):
    xs = (hn_ref[...].astype(F32) + xx_ref[...].astype(F32) * mix_ref[0]).astype(BF16)
    o_ref[0] = _dot(xs, w_ref[0]).astype(BF16)


def _rkv_proj(hn2, xx2, mix3, w_rkv):
    m, d = hn2.shape
    bm = _row_tile(m)
    return pl.pallas_call(
        _rkv_kernel,
        out_shape=jax.ShapeDtypeStruct((3, m, d), BF16),
        grid=(3, m // bm),
        in_specs=[pl.BlockSpec((bm, d), lambda s, i: (i, 0)),
                  pl.BlockSpec((bm, d), lambda s, i: (i, 0)),
                  pl.BlockSpec((1, 1, d), lambda s, i: (s, 0, 0)),
                  pl.BlockSpec((1, d, d), lambda s, i: (s, 0, 0))],
        out_specs=pl.BlockSpec((1, bm, d), lambda s, i: (s, i, 0)),
        compiler_params=_cparams(("parallel", "parallel")),
        name="rkv_proj",
    )(hn2, xx2, mix3, w_rkv)


def _lora_kernel(*refs, has_vres):
    if has_vres:
        (hn_ref, xx_ref, mix_ref, w0_ref, w1_ref, w2_ref, a0_ref, a1_ref, a2_ref, g1_ref, g2_ref,
         v0_ref, v1_ref, v2_ref, ld_ref, a_ref, g_ref, vg_ref) = refs
    else:
        (hn_ref, xx_ref, mix_ref, w0_ref, w1_ref, w2_ref, a0_ref, a1_ref, a2_ref, g1_ref, g2_ref,
         ld_ref, a_ref, g_ref) = refs
    hn = hn_ref[...].astype(F32)
    xx = xx_ref[...].astype(F32)

    def mixed(s):
        return (hn + xx * mix_ref[s:s + 1, :]).astype(BF16)

    z = w0_ref[...] + _dot(jnp.tanh(_dot(mixed(0), w1_ref[...])).astype(BF16), w2_ref[...])
    ld_ref[...] = -jnp.exp(F32(-0.5)) * _sigmoid(z)
    za = a0_ref[...] + _dot(_dot(mixed(1), a1_ref[...]).astype(BF16), a2_ref[...])
    a_ref[...] = _sigmoid(za).astype(BF16)
    gl = _sigmoid(_dot(mixed(2), g1_ref[...])).astype(BF16)
    g_ref[...] = _dot(gl, g2_ref[...]).astype(BF16)
    if has_vres:
        zv = v0_ref[...] + _dot(_dot(mixed(3), v1_ref[...]).astype(BF16), v2_ref[...])
        vg_ref[...] = _sigmoid(zv).astype(BF16)


def _pad_lora(w_down, w_up):
    r = w_down.shape[1]
    rp = -(-r // LANES) * LANES
    return (jnp.pad(w_down, ((0, 0), (0, rp - r))).astype(BF16),
            jnp.pad(w_up, ((0, rp - r), (0, 0))).astype(BF16))


def _lora(hn2, xx2, mix, w0, w1, w2, a0, a1, a2, g1, g2, vres):
    m, d = hn2.shape
    bm = _row_tile(m)
    has_vres = vres is not None
    w1p, w2p = _pad_lora(w1, w2)
    a1p, a2p = _pad_lora(a1, a2)
    g1p, g2p = _pad_lora(g1, g2)
    args = [hn2, xx2, mix, w0.reshape(1, d), w1p, w2p, a0.reshape(1, d), a1p, a2p, g1p, g2p]
    if has_vres:
        v0, v1, v2 = vres
        v1p, v2p = _pad_lora(v1, v2)
        args += [v0.reshape(1, d), v1p, v2p]
    row = pl.BlockSpec((bm, d), lambda i: (i, 0))

    def full(a):
        return pl.BlockSpec(a.shape, lambda i: (0, 0))

    in_specs = [row, row] + [full(a) for a in args[2:]]
    outs = [jax.ShapeDtypeStruct((m, d), F32), jax.ShapeDtypeStruct((m, d), BF16),
            jax.ShapeDtypeStruct((m, d), BF16)]
    if has_vres:
        outs.append(jax.ShapeDtypeStruct((m, d), BF16))
    return pl.pallas_call(
        functools.partial(_lora_kernel, has_vres=has_vres),
        out_shape=tuple(outs),
        grid=(m // bm,),
        in_specs=in_specs,
        out_specs=tuple(row for _ in outs),
        compiler_params=_cparams(("parallel",)),
        name="rwkv_lora",
    )(*args)


def _split_dot(a_bf16, x):
    hi = x.astype(BF16)
    lo = (x - hi.astype(F32)).astype(BF16)
    return _dot(a_bf16, hi) + _dot(a_bf16, lo)


def _seg_sum(x, ones_bd):
    hi = x.astype(BF16)
    lo = (x - hi.astype(F32)).astype(BF16)
    return _dot(hi, ones_bd) + _dot(lo, ones_bd)


def _wkv_kernel(*refs, n_sub, has_vres):
    if has_vres:
        (r_ref, k_ref, v_ref, ld_ref, a_ref, g_ref, vf_ref, vg_ref,
         kk_ref, ka_ref, rk_ref, lg_ref, lb_ref, o_ref, s_ref) = refs
    else:
        (r_ref, k_ref, v_ref, ld_ref, a_ref, g_ref,
         kk_ref, ka_ref, rk_ref, lg_ref, lb_ref, o_ref, s_ref) = refs
    C, G, HG, N = WKV_CHUNK, WKV_GROUP, WKV_HEADS_PER_GROUP, RWKV_HEAD

    @pl.when(pl.program_id(2) == 0)
    def _():
        s_ref[...] = jnp.zeros_like(s_ref)

    lane_head = lax.broadcasted_iota(jnp.int32, (1, G), 1) // N
    head_mask = [(lane_head == h).astype(F32) for h in range(HG)]
    row = lax.broadcasted_iota(jnp.int32, (G, G), 0)
    col = lax.broadcasted_iota(jnp.int32, (G, G), 1)
    strict = col < row
    incl = col <= row
    eye = col == row
    ones_bd = ((row // N) == (col // N)).astype(BF16)
    trow = lax.broadcasted_iota(jnp.int32, (C, C), 0)
    tcol = lax.broadcasted_iota(jnp.int32, (C, C), 1)
    tri = (tcol <= trow).astype(BF16)
    k_k, k_a, r_k = kk_ref[...], ka_ref[...], rk_ref[...]
    lnx_g, lnx_b = lg_ref[...], lb_ref[...]

    def stack(x):
        return jnp.concatenate([(x * head_mask[h]).astype(BF16) for h in range(HG)], axis=0)

    def unstack(z):
        return z[0:C] + z[C:2 * C] + z[2 * C:3 * C] + z[3 * C:4 * C]

    for j in range(n_sub):
        sl = slice(j * C, (j + 1) * C)
        r = r_ref[0, sl, :].astype(F32)
        k = k_ref[0, sl, :].astype(F32)
        v = v_ref[0, sl, :].astype(F32)
        ld = ld_ref[0, sl, :]
        a_s = a_ref[0, sl, :].astype(F32)
        if has_vres:
            v = v + (vf_ref[0, sl, :].astype(F32) - v) * vg_ref[0, sl, :].astype(F32)

        kk = k * k_k
        nrm = jnp.sqrt(_seg_sum(kk * kk, ones_bd))
        kk = kk / jnp.maximum(nrm, 1e-12)
        kp = k * (1.0 + (a_s - 1.0) * k_a)
        aa = -kk
        bb = kk * a_s

        lw = _split_dot(tri, ld)
        lw_last = lw[C - 1:C, :]
        w_in = jnp.exp(lw)
        w_inv = jnp.exp(-lw)
        w_tail = jnp.exp(lw_last - lw)
        a_t = aa * jnp.exp(lw - ld)
        r_t = r * w_in
        a_stk, r_stk = stack(a_t), stack(r_t)
        b_stk, k_stk = stack(bb * w_inv), stack(kp * w_inv)
        v_stk = stack(v)
        bh_stk, kh_stk = stack(bb * w_tail), stack(kp * w_tail)

        sc = _dot_nt(jnp.concatenate([a_stk, r_stk], axis=0), jnp.concatenate([b_stk, k_stk], axis=0))
        l_ab = jnp.where(strict, sc[:G, :G], 0.0)
        l_ak = jnp.where(strict, sc[:G, G:], 0.0).astype(BF16)
        l_rb = jnp.where(incl, sc[G:, :G], 0.0).astype(BF16)
        l_rk = jnp.where(incl, sc[G:, G:], 0.0).astype(BF16)

        p = l_ab.astype(BF16)
        t = jnp.where(eye, 1.0, l_ab)
        n_sq = 1
        while n_sq * 2 < C:
            n_sq *= 2
            p32 = _dot(p, p)
            p = p32.astype(BF16)
            t = t + _dot(t.astype(BF16), p)
        t = t.astype(BF16)

        lv = _dot(l_ak, v_stk).astype(BF16)
        ta = _dot(t, jnp.concatenate([a_stk, lv], axis=1))
        abar = ta[:, :G].astype(BF16)
        uv = ta[:, G:].astype(BF16)
        rb = _dot(l_rb, jnp.concatenate([abar, uv], axis=1))
        rbar = r_t + unstack(rb[:, :G])
        yv = unstack(rb[:, G:] + _dot(l_rk, v_stk))
        gmat = jnp.where(eye, jnp.exp(lw_last), 0.0) + _dot_tn(bh_stk, abar)
        hmat = _dot_tn(jnp.concatenate([bh_stk, kh_stk], axis=0), jnp.concatenate([uv, v_stk], axis=0))

        s0 = s_ref[...].astype(BF16)
        y = _dot(rbar.astype(BF16), s0) + yv
        s_ref[...] = _dot(gmat.astype(BF16), s0) + hmat

        mu = _seg_sum(y, ones_bd) * (1.0 / N)
        dy = y - mu
        var = _seg_sum(dy * dy, ones_bd) * (1.0 / N)
        yn = dy * lax.rsqrt(var + LNX_EPS) * lnx_g + lnx_b
        bonus = _seg_sum(r * kp * r_k, ones_bd) * v
        o_ref[0, sl, :] = ((yn + bonus) * g_ref[0, sl, :].astype(F32)).astype(BF16)


def _wkv(r, k, v, ld, a, g, v_first, vg, k_k, k_a, r_k, lnx_g, lnx_b):
    b, lp, d = r.shape
    n_sub = 2 if lp % (2 * WKV_CHUNK) == 0 else 1
    tl = n_sub * WKV_CHUNK
    has_vres = v_first is not None
    seq = pl.BlockSpec((1, tl, WKV_GROUP), lambda bi, hg, c: (bi, c, hg))
    par = pl.BlockSpec((1, WKV_GROUP), lambda bi, hg, c: (0, hg))
    args = [r, k, v, ld, a, g] + ([v_first, vg] if has_vres else [])
    params = [p.reshape(1, d) for p in (k_k, k_a, r_k, lnx_g, lnx_b)]
    return pl.pallas_call(
        functools.partial(_wkv_kernel, n_sub=n_sub, has_vres=has_vres),
        out_shape=jax.ShapeDtypeStruct((b, lp, d), BF16),
        grid=(b, d // WKV_GROUP, lp // tl),
        in_specs=[seq] * len(args) + [par] * len(params),
        out_specs=seq,
        scratch_shapes=[pltpu.VMEM((WKV_GROUP, WKV_GROUP), F32)],
        compiler_params=_cparams(("parallel", "parallel", "arbitrary")),
        name="wkv7",
    )(*args, *params)


def _rwkv_layer(h3, j, v_first, p):
    b, lp, d = h3.shape
    m = b * lp
    hn, xx = _norm_shift(h3, p["norm_mix_pre"])
    hn2, xx2 = hn.reshape(m, d), xx.reshape(m, d)
    mix = p["rwkv_mix"][j]
    rkv = _rkv_proj(hn2, xx2, mix[jnp.array([0, 2, 3])].reshape(3, 1, d), p["rwkv_w_rkv"][j].astype(BF16))
    r, k, v = (rkv[s].reshape(b, lp, d) for s in range(3))
    vres = None if j == 0 else (p["rwkv_v0"][j - 1], p["rwkv_v1"][j - 1], p["rwkv_v2"][j - 1])
    lora = _lora(hn2, xx2, mix[jnp.array([1, 4, 5, 3])], p["rwkv_w0"][j], p["rwkv_w1"][j], p["rwkv_w2"][j],
                 p["rwkv_a0"][j], p["rwkv_a1"][j], p["rwkv_a2"][j], p["rwkv_g1"][j], p["rwkv_g2"][j], vres)
    ld, a, g = (t.reshape(b, lp, d) for t in lora[:3])
    vg = lora[3].reshape(b, lp, d) if vres is not None else None
    if vres is None:
        v_first = v
    yg = _wkv(r, k, v, ld, a, g, v_first if vres is not None else None, vg,
              p["rwkv_k_k"][j], p["rwkv_k_a"][j], p["rwkv_r_k"][j].reshape(d),
              p["rwkv_lnx_g"][j], p["rwkv_lnx_b"][j])
    h2 = _proj_residual(yg.reshape(m, d), p["rwkv_w_o"][j].astype(BF16), h3.reshape(m, d), p["norm_mix_post"])
    return h2.reshape(b, lp, d), v_first


def _mla_proj_kernel(h_ref, g_ref, win_ref, qn_ref, wuq_ref, kvn_ref, wukv_ref, cos_ref, sin_ref,
                     q_ref, kv_ref, kpe_ref, *, scale):
    hn = _rms(h_ref[...], g_ref[...]).astype(BF16)
    lat = _dot(hn, win_ref[...])
    cq = _rms(lat[:, :Q_LORA], qn_ref[...]).astype(BF16)
    ckv = _rms(lat[:, Q_LORA:Q_LORA + KV_LORA], kvn_ref[...]).astype(BF16)
    cos, sin = cos_ref[...], sin_ref[...]

    def rope(x):
        return x * cos + pltpu.roll(x, ROPE_D // 2, axis=1) * sin

    keep = (lax.broadcasted_iota(jnp.int32, (1, LANES), 1) < ROPE_D).astype(F32)
    kpe_ref[...] = (rope(lat[:, Q_LORA + KV_LORA:]) * keep).astype(BF16)
    kv_ref[...] = _dot(ckv, wukv_ref[...]).astype(BF16)
    q = _dot(cq, wuq_ref[...])
    hw = NOPE_D + LANES
    for h in range(MLA_HEADS):
        q_ref[:, h * hw:h * hw + NOPE_D] = (q[:, h * hw:h * hw + NOPE_D] * scale).astype(BF16)
        q_ref[:, h * hw + NOPE_D:(h + 1) * hw] = (rope(q[:, h * hw + NOPE_D:(h + 1) * hw]) * scale).astype(BF16)


def _mla_attn_kernel(q_ref, kv_ref, kpe_ref, o_ref, kcat_ref):
    lp = q_ref.shape[1]
    kcat_ref[:, :NOPE_D] = kv_ref[0, :, :NOPE_D]
    kcat_ref[:, NOPE_D:] = kpe_ref[0]
    for q0 in range(0, lp, ATTN_Q_TILE):
        q1 = min(q0 + ATTN_Q_TILE, lp)
        s = _dot_nt(q_ref[0, q0:q1, :], kcat_ref[:q1, :])
        qpos = q0 + lax.broadcasted_iota(jnp.int32, s.shape, 0)
        kpos = lax.broadcasted_iota(jnp.int32, s.shape, 1)
        s = jnp.where(kpos <= qpos, s, -1e30)
        s = s - jnp.max(s, axis=-1, keepdims=True)
        e = jnp.exp(s)
        den = jnp.sum(e, axis=-1, keepdims=True)
        o = _dot(e.astype(BF16), kv_ref[0, :q1, NOPE_D:])
        o_ref[0, q0:q1, :] = (o / den).astype(BF16)


def _rope_tables(b, lp):
    half = ROPE_D // 2
    inv_freq = 1.0 / (ROPE_THETA ** (jnp.arange(0, ROPE_D, 2, dtype=F32) / ROPE_D))
    ang = jnp.arange(lp, dtype=jnp.int32).astype(F32)[:, None] * inv_freq[None, :]
    cos, sin = jnp.cos(ang), jnp.sin(ang)
    cos4 = jnp.tile(cos, (b, LANES // half))
    sin4 = jnp.tile(jnp.concatenate([-sin, sin], axis=1), (b, LANES // ROPE_D))
    return cos4, sin4


def _mla_layer(h3, j, p):
    b, lp, d = h3.shape
    m = b * lp
    hq = MLA_HEADS
    hw = NOPE_D + LANES
    bm = _row_tile(m)
    w_in = p["mla_w_in"][j]
    w_in = jnp.concatenate([w_in, w_in[:, Q_LORA + KV_LORA:]], axis=1).astype(BF16)
    w_uq = p["mla_w_uq"][j].reshape(Q_LORA, hq, NOPE_D + ROPE_D)
    w_uq = jnp.concatenate([w_uq, w_uq[:, :, NOPE_D:]], axis=2).reshape(Q_LORA, hq * hw).astype(BF16)
    w_ukv = p["mla_w_ukv"][j].astype(BF16)
    cos4, sin4 = _rope_tables(b, lp)
    scale = float((NOPE_D + ROPE_D) ** -0.5)

    def full(a):
        return pl.BlockSpec(a.shape, lambda i: (0, 0))

    consts = [p["norm_mix_pre"].reshape(1, d), w_in, p["mla_q_norm"][j].reshape(1, Q_LORA), w_uq,
              p["mla_kv_norm"][j].reshape(1, KV_LORA), w_ukv]
    q, kv, kpe = pl.pallas_call(
        functools.partial(_mla_proj_kernel, scale=scale),
        out_shape=(jax.ShapeDtypeStruct((m, hq * hw), BF16),
                   jax.ShapeDtypeStruct((m, hq * (NOPE_D + V_D)), BF16),
                   jax.ShapeDtypeStruct((m, LANES), BF16)),
        grid=(m // bm,),
        in_specs=[pl.BlockSpec((bm, d), lambda i: (i, 0))] + [full(a) for a in consts]
                 + [pl.BlockSpec((bm, LANES), lambda i: (i, 0))] * 2,
        out_specs=(pl.BlockSpec((bm, hq * hw), lambda i: (i, 0)),
                   pl.BlockSpec((bm, hq * (NOPE_D + V_D)), lambda i: (i, 0)),
                   pl.BlockSpec((bm, LANES), lambda i: (i, 0))),
        compiler_params=_cparams(("parallel",)),
        name="mla_proj",
    )(h3.reshape(m, d), *consts, cos4, sin4)

    o = pl.pallas_call(
        _mla_attn_kernel,
        out_shape=jax.ShapeDtypeStruct((b, lp, hq * V_D), BF16),
        grid=(b, hq),
        in_specs=[pl.BlockSpec((1, lp, hw), lambda bi, h: (bi, 0, h)),
                  pl.BlockSpec((1, lp, NOPE_D + V_D), lambda bi, h: (bi, 0, h)),
                  pl.BlockSpec((1, lp, LANES), lambda bi, h: (bi, 0, 0))],
        out_specs=pl.BlockSpec((1, lp, V_D), lambda bi, h: (bi, 0, h)),
        scratch_shapes=[pltpu.VMEM((lp, hw), BF16)],
        compiler_params=_cparams(("parallel", "parallel")),
        name="mla_attention",
    )(q.reshape(b, lp, hq * hw), kv.reshape(b, lp, hq * (NOPE_D + V_D)), kpe.reshape(b, lp, LANES))

    h2 = _proj_residual(o.reshape(m, hq * V_D), p["mla_w_o"][j].astype(BF16), h3.reshape(m, d),
                        p["norm_mix_post"])
    return h2.reshape(b, lp, d)


def _pool_kernel(x_ref, w_ref, sc_ref, o_ref):
    gi = pl.program_id(1)
    x = x_ref[0]
    t = lax.broadcasted_iota(jnp.int32, x.shape, 0)
    for idx, win in enumerate(POOL_WINDOWS):
        @pl.when(gi == idx)
        def _(win=win):
            s = x
            step = 1
            while step < win:
                s = s + jnp.where(t >= step, pltpu.roll(s, step, axis=0), 0.0)
                step *= 2
            cnt = jnp.minimum(t + 1, win).astype(F32)
            pooled = (s / cnt - x).astype(BF16)
            o_ref[0] = _dot(pooled, w_ref[0]) * sc_ref[...]


def _pool_layer(h3, j, p):
    b, lp, d = h3.shape
    m = b * lp
    ng = len(POOL_WINDOWS)
    gw = d // ng
    hn = _prenorm(h3.reshape(m, d), p["norm_mix_pre"], F32).reshape(b, lp, d)
    y = pl.pallas_call(
        _pool_kernel,
        out_shape=jax.ShapeDtypeStruct((b, lp, d), F32),
        grid=(b, ng),
        in_specs=[pl.BlockSpec((1, lp, gw), lambda bi, g: (bi, 0, g)),
                  pl.BlockSpec((1, gw, gw), lambda bi, g: (g, 0, 0)),
                  pl.BlockSpec((1, gw), lambda bi, g: (0, g))],
        out_specs=pl.BlockSpec((1, lp, gw), lambda bi, g: (bi, 0, g)),
        compiler_params=_cparams(("parallel", "parallel")),
        name="pool_mix",
    )(hn, p["pool_w"][j].astype(BF16), p["pool_scale"][j].reshape(1, d))
    h2 = _postnorm_residual(h3.reshape(m, d), y.reshape(m, d), p["norm_mix_post"])
    return h2.reshape(b, lp, d)


def kernel(x, meta_tokens, norm_mix_pre, norm_mix_post, norm_ffn_pre, norm_ffn_post, ffn_w_gate, ffn_w_up, ffn_w_down, rwkv_mix, rwkv_w0, rwkv_w1, rwkv_w2, rwkv_a0, rwkv_a1, rwkv_a2, rwkv_v0, rwkv_v1, rwkv_v2, rwkv_g1, rwkv_g2, rwkv_k_k, rwkv_k_a, rwkv_r_k, rwkv_w_rkv, rwkv_w_o, rwkv_lnx_g, rwkv_lnx_b, mla_w_in, mla_q_norm, mla_w_uq, mla_kv_norm, mla_w_ukv, mla_w_o, pool_w, pool_scale):
    b, seq, d = x.shape
    depth = norm_mix_pre.shape[0]
    ltot = N_META + seq
    lp = -(-ltot // SEQ_ALIGN) * SEQ_ALIGN
    meta = jnp.broadcast_to(meta_tokens.astype(x.dtype)[None], (b, N_META, d))
    h = jnp.concatenate([meta, x, jnp.zeros((b, lp - ltot, d), x.dtype)], axis=1)
    shared = dict(rwkv_mix=rwkv_mix, rwkv_w0=rwkv_w0, rwkv_w1=rwkv_w1, rwkv_w2=rwkv_w2, rwkv_a0=rwkv_a0,
                  rwkv_a1=rwkv_a1, rwkv_a2=rwkv_a2, rwkv_v0=rwkv_v0, rwkv_v1=rwkv_v1, rwkv_v2=rwkv_v2,
                  rwkv_g1=rwkv_g1, rwkv_g2=rwkv_g2, rwkv_k_k=rwkv_k_k, rwkv_k_a=rwkv_k_a, rwkv_r_k=rwkv_r_k,
                  rwkv_w_rkv=rwkv_w_rkv, rwkv_w_o=rwkv_w_o, rwkv_lnx_g=rwkv_lnx_g, rwkv_lnx_b=rwkv_lnx_b,
                  mla_w_in=mla_w_in, mla_q_norm=mla_q_norm, mla_w_uq=mla_w_uq, mla_kv_norm=mla_kv_norm,
                  mla_w_ukv=mla_w_ukv, mla_w_o=mla_w_o, pool_w=pool_w, pool_scale=pool_scale)
    v_first = None
    for i in range(depth):
        p = dict(shared, norm_mix_pre=norm_mix_pre[i], norm_mix_post=norm_mix_post[i])
        kind, j = i % 3, i // 3
        if kind == 0:
            h, v_first = _rwkv_layer(h, j, v_first, p)
        elif kind == 1:
            h = _mla_layer(h, j, p)
        else:
            h = _pool_layer(h, j, p)
        h = _ffn(h.reshape(b * lp, d), norm_ffn_pre[i], norm_ffn_post[i], ffn_w_gate[i].astype(BF16),
                 ffn_w_up[i].astype(BF16), ffn_w_down[i].astype(BF16)).reshape(b, lp, d)
    return h[:, N_META:ltot]
```

```python
import functools

import jax
import jax.numpy as jnp
from jax import lax
from jax.experimental import pallas as pl
from jax.experimental.pallas import tpu as pltpu

F32 = jnp.float32
BF16 = jnp.bfloat16

N_META = 16
RMS_EPS = 1e-6
RWKV_HEAD = 64
LNX_EPS = 1e-5 * RWKV_HEAD
MLA_HEADS = 16
Q_LORA = 512
KV_LORA = 512
NOPE_D = 128
ROPE_D = 64
V_D = 128
ROPE_THETA = 10000.0
POOL_WINDOWS = (2, 4, 8, 16)

LANES = 128
SEQ_ALIGN = 128
WKV_CHUNK = 64
WKV_HEADS_PER_GROUP = 4
WKV_GROUP = WKV_HEADS_PER_GROUP * RWKV_HEAD
WKV_GROUPS_PER_STEP = 4
ATTN_Q_TILE = 256
VMEM_LIMIT = 56 * 1024 * 1024


def _cparams(sem):
    return pltpu.CompilerParams(dimension_semantics=sem, vmem_limit_bytes=VMEM_LIMIT)


def _dot(a, b):
    return jnp.dot(a, b, preferred_element_type=F32)


def _dot_nt(a, b):
    return lax.dot_general(a, b, (((1,), (1,)), ((), ())), preferred_element_type=F32)


def _dot_tn(a, b):
    return lax.dot_general(a, b, (((0,), (0,)), ((), ())), preferred_element_type=F32)


def _rms(x, g, eps=RMS_EPS):
    ms = jnp.mean(x * x, axis=-1, keepdims=True)
    return x * lax.rsqrt(ms + eps) * g


def _sigmoid(x):
    return 1.0 / (1.0 + jnp.exp(-x))


def _row_tile(m):
    for t in (512, 256, 128, 64, 32, 16):
        if m % t == 0:
            return t
    raise ValueError(f"unsupported row count {m}")


def _prenorm_kernel(h_ref, g_ref, o_ref):
    o_ref[...] = _rms(h_ref[...], g_ref[...]).astype(o_ref.dtype)


def _prenorm(h2, g, dtype):
    m, d = h2.shape
    bm = _row_tile(m)
    return pl.pallas_call(
        _prenorm_kernel,
        out_shape=jax.ShapeDtypeStruct((m, d), dtype),
        grid=(m // bm,),
        in_specs=[pl.BlockSpec((bm, d), lambda i: (i, 0)),
                  pl.BlockSpec((1, d), lambda i: (0, 0))],
        out_specs=pl.BlockSpec((bm, d), lambda i: (i, 0)),
        compiler_params=_cparams(("parallel",)),
        name="prenorm",
    )(h2, g.reshape(1, d))


def _postnorm_kernel(h_ref, y_ref, g_ref, o_ref):
    o_ref[...] = h_ref[...] + _rms(y_ref[...].astype(F32), g_ref[...])


def _postnorm_residual(h2, y2, g):
    m, d = h2.shape
    bm = _row_tile(m)
    return pl.pallas_call(
        _postnorm_kernel,
        out_shape=jax.ShapeDtypeStruct((m, d), F32),
        grid=(m // bm,),
        in_specs=[pl.BlockSpec((bm, d), lambda i: (i, 0)),
                  pl.BlockSpec((bm, d), lambda i: (i, 0)),
                  pl.BlockSpec((1, d), lambda i: (0, 0))],
        out_specs=pl.BlockSpec((bm, d), lambda i: (i, 0)),
        compiler_params=_cparams(("parallel",)),
        name="postnorm_residual",
    )(h2, y2, g.reshape(1, d))


def _proj_residual_kernel(x_ref, w_ref, h_ref, g_ref, o_ref):
    y = _dot(x_ref[...], w_ref[...])
    o_ref[...] = h_ref[...] + _rms(y, g_ref[...])


def _proj_residual(x2, w, h2, g):
    m, k = x2.shape
    d = w.shape[1]
    bm = _row_tile(m)
    return pl.pallas_call(
        _proj_residual_kernel,
        out_shape=jax.ShapeDtypeStruct((m, d), F32),
        grid=(m // bm,),
        in_specs=[pl.BlockSpec((bm, k), lambda i: (i, 0)),
                  pl.BlockSpec((k, d), lambda i: (0, 0)),
                  pl.BlockSpec((bm, d), lambda i: (i, 0)),
                  pl.BlockSpec((1, d), lambda i: (0, 0))],
        out_specs=pl.BlockSpec((bm, d), lambda i: (i, 0)),
        compiler_params=_cparams(("parallel",)),
        name="proj_residual",
    )(x2, w, h2, g.reshape(1, d))


def _ffn_kernel(h_ref, gpre_ref, gpost_ref, wg_ref, wu_ref, wd_ref, o_ref, xn_ref, acc_ref):
    f = pl.program_id(1)

    @pl.when(f == 0)
    def _():
        xn_ref[...] = _rms(h_ref[...], gpre_ref[...]).astype(BF16)
        acc_ref[...] = jnp.zeros_like(acc_ref)

    x = xn_ref[...]
    gate = _dot(x, wg_ref[...])
    up = _dot(x, wu_ref[...])
    mid = (gate * _sigmoid(gate) * up).astype(BF16)
    acc_ref[...] += _dot(mid, wd_ref[...])

    @pl.when(f == pl.num_programs(1) - 1)
    def _():
        o_ref[...] = h_ref[...] + _rms(acc_ref[...], gpost_ref[...])


def _ffn(h2, gpre, gpost, wg, wu, wd):
    m, d = h2.shape
    f = wg.shape[1]
    bm = _row_tile(m)
    bf = 512 if f % 512 == 0 else f
    return pl.pallas_call(
        _ffn_kernel,
        out_shape=jax.ShapeDtypeStruct((m, d), F32),
        grid=(m // bm, f // bf),
        in_specs=[pl.BlockSpec((bm, d), lambda i, j: (i, 0)),
                  pl.BlockSpec((1, d), lambda i, j: (0, 0)),
                  pl.BlockSpec((1, d), lambda i, j: (0, 0)),
                  pl.BlockSpec((d, bf), lambda i, j: (0, j)),
                  pl.BlockSpec((d, bf), lambda i, j: (0, j)),
                  pl.BlockSpec((bf, d), lambda i, j: (j, 0))],
        out_specs=pl.BlockSpec((bm, d), lambda i, j: (i, 0)),
        scratch_shapes=[pltpu.VMEM((bm, d), BF16), pltpu.VMEM((bm, d), F32)],
        compiler_params=_cparams(("parallel", "arbitrary")),
        name="ffn",
    )(h2, gpre.reshape(1, d), gpost.reshape(1, d), wg, wu, wd)


def _shift_kernel(h_ref, halo_ref, g_ref, hn_ref, xx_ref):
    i = pl.program_id(1)
    g = g_ref[...]
    hn = _rms(h_ref[0], g)
    prev = _rms(halo_ref[0], g)[7:8]
    prev = jnp.where(i == 0, 0.0, prev)
    rolled = pltpu.roll(hn, 1, axis=0)
    row = lax.broadcasted_iota(jnp.int32, hn.shape, 0)
    shifted = jnp.where(row == 0, prev, rolled)
    hn_ref[0] = hn.astype(BF16)
    xx_ref[0] = (shifted - hn).astype(BF16)


def _norm_shift(h3, g):
    b, lp, d = h3.shape
    tl = lp // 4 if lp % 64 == 0 else lp
    halo_blocks = tl // 8
    out = jax.ShapeDtypeStruct((b, lp, d), BF16)
    return pl.pallas_call(
        _shift_kernel,
        out_shape=(out, out),
        grid=(b, lp // tl),
        in_specs=[pl.BlockSpec((1, tl, d), lambda bi, i: (bi, i, 0)),
                  pl.BlockSpec((1, 8, d), lambda bi, i: (bi, jnp.maximum(i * halo_blocks - 1, 0), 0)),
                  pl.BlockSpec((1, d), lambda bi, i: (0, 0))],
        out_specs=(pl.BlockSpec((1, tl, d), lambda bi, i: (bi, i, 0)),
                   pl.BlockSpec((1, tl, d), lambda bi, i: (bi, i, 0))),
        compiler_params=_cparams(("parallel", "parallel")),
        name="norm_shift",
    )(h3, h3, g.reshape(1, d))


def _rkv_kernel(hn_ref, xx_ref, mix_ref, w_ref, o_ref):
    xs = (hn_ref[...].astype(F32) + xx_ref[...].astype(F32) * mix_ref[0]).astype(BF16)
    o_ref[0] = _dot(xs, w_ref[0]).astype(BF16)


def _rkv_proj(hn2, xx2, mix3, w_rkv):
    m, d = hn2.shape
    bm = _row_tile(m)
    return pl.pallas_call(
        _rkv_kernel,
        out_shape=jax.ShapeDtypeStruct((3, m, d), BF16),
        grid=(3, m // bm),
        in_specs=[pl.BlockSpec((bm, d), lambda s, i: (i, 0)),
                  pl.BlockSpec((bm, d), lambda s, i: (i, 0)),
                  pl.BlockSpec((1, 1, d), lambda s, i: (s, 0, 0)),
                  pl.BlockSpec((1, d, d), lambda s, i: (s, 0, 0))],
        out_specs=pl.BlockSpec((1, bm, d), lambda s, i: (s, i, 0)),
        compiler_params=_cparams(("parallel", "parallel")),
        name="rkv_proj",
    )(hn2, xx2, mix3, w_rkv)


def _lora_kernel(*refs, has_vres):
    if has_vres:
        (hn_ref, xx_ref, mix_ref, w0_ref, w1_ref, w2_ref, a0_ref, a1_ref, a2_ref, g1_ref, g2_ref,
         v0_ref, v1_ref, v2_ref, ld_ref, a_ref, g_ref, vg_ref) = refs
    else:
        (hn_ref, xx_ref, mix_ref, w0_ref, w1_ref, w2_ref, a0_ref, a1_ref, a2_ref, g1_ref, g2_ref,
         ld_ref, a_ref, g_ref) = refs
    hn = hn_ref[...].astype(F32)
    xx = xx_ref[...].astype(F32)

    def mixed(s):
        return (hn + xx * mix_ref[s:s + 1, :]).astype(BF16)

    z = w0_ref[...] + _dot(jnp.tanh(_dot(mixed(0), w1_ref[...])).astype(BF16), w2_ref[...])
    ld_ref[...] = -jnp.exp(F32(-0.5)) * _sigmoid(z)
    za = a0_ref[...] + _dot(_dot(mixed(1), a1_ref[...]).astype(BF16), a2_ref[...])
    a_ref[...] = _sigmoid(za).astype(BF16)
    gl = _sigmoid(_dot(mixed(2), g1_ref[...])).astype(BF16)
    g_ref[...] = _dot(gl, g2_ref[...]).astype(BF16)
    if has_vres:
        zv = v0_ref[...] + _dot(_dot(mixed(3), v1_ref[...]).astype(BF16), v2_ref[...])
        vg_ref[...] = _sigmoid(zv).astype(BF16)


def _pad_lora(w_down, w_up):
    r = w_down.shape[1]
    rp = -(-r // LANES) * LANES
    return (jnp.pad(w_down, ((0, 0), (0, rp - r))).astype(BF16),
            jnp.pad(w_up, ((0, rp - r), (0, 0))).astype(BF16))


def _lora(hn2, xx2, mix, w0, w1, w2, a0, a1, a2, g1, g2, vres):
    m, d = hn2.shape
    bm = _row_tile(m)
    has_vres = vres is not None
    w1p, w2p = _pad_lora(w1, w2)
    a1p, a2p = _pad_lora(a1, a2)
    g1p, g2p = _pad_lora(g1, g2)
    args = [hn2, xx2, mix, w0.reshape(1, d), w1p, w2p, a0.reshape(1, d), a1p, a2p, g1p, g2p]
    if has_vres:
        v0, v1, v2 = vres
        v1p, v2p = _pad_lora(v1, v2)
        args += [v0.reshape(1, d), v1p, v2p]
    row = pl.BlockSpec((bm, d), lambda i: (i, 0))

    def full(a):
        return pl.BlockSpec(a.shape, lambda i: (0, 0))

    in_specs = [row, row] + [full(a) for a in args[2:]]
    outs = [jax.ShapeDtypeStruct((m, d), F32), jax.ShapeDtypeStruct((m, d), BF16),
            jax.ShapeDtypeStruct((m, d), BF16)]
    if has_vres:
        outs.append(jax.ShapeDtypeStruct((m, d), BF16))
    return pl.pallas_call(
        functools.partial(_lora_kernel, has_vres=has_vres),
        out_shape=tuple(outs),
        grid=(m // bm,),
        in_specs=in_specs,
        out_specs=tuple(row for _ in outs),
        compiler_params=_cparams(("parallel",)),
        name="rwkv_lora",
    )(*args)


def _split_dot(a_bf16, x):
    hi = x.astype(BF16)
    lo = (x - hi.astype(F32)).astype(BF16)
    return _dot(a_bf16, hi) + _dot(a_bf16, lo)


def _split_dot_rhs(x, w_bf16):
    hi = x.astype(BF16)
    lo = (x - hi.astype(F32)).astype(BF16)
    return _dot(hi, w_bf16) + _dot(lo, w_bf16)


def _wkv_kernel(*refs, n_chunks, n_groups, has_vres):
    if has_vres:
        (r_ref, k_ref, v_ref, ld_ref, a_ref, g_ref, vf_ref, vg_ref,
         kk_ref, ka_ref, rk_ref, lg_ref, lb_ref, mbd_ref, msk_ref, tri_ref, o_ref, s_ref) = refs
    else:
        (r_ref, k_ref, v_ref, ld_ref, a_ref, g_ref,
         kk_ref, ka_ref, rk_ref, lg_ref, lb_ref, mbd_ref, msk_ref, tri_ref, o_ref, s_ref) = refs
    C, G, HG, N = WKV_CHUNK, WKV_GROUP, WKV_HEADS_PER_GROUP, RWKV_HEAD

    @pl.when(pl.program_id(2) == 0)
    def _():
        s_ref[...] = jnp.zeros_like(s_ref)

    def bd(x):
        xb = x.astype(BF16)
        return jnp.concatenate([xb] * HG, axis=0) * mbd_ref[...]

    gs = range(n_groups)
    lanes = [slice(gi * G, (gi + 1) * G) for gi in gs]

    def chunk(j, carry):
        rows = pl.ds(pl.multiple_of(j * C, C), C)
        strict, incl, eye = msk_ref[0], msk_ref[1], msk_ref[2]
        r = [r_ref[0, rows, lanes[i]].astype(F32) for i in gs]
        k = [k_ref[0, rows, lanes[i]].astype(F32) for i in gs]
        v = [v_ref[0, rows, lanes[i]].astype(F32) for i in gs]
        a_s = [a_ref[0, rows, lanes[i]].astype(F32) for i in gs]
        if has_vres:
            v = [v[i] + (vf_ref[0, rows, lanes[i]].astype(F32) - v[i]) * vg_ref[0, rows, lanes[i]].astype(F32)
                 for i in gs]
        kk = [k[i] * kk_ref[:, lanes[i]] for i in gs]
        kp = [k[i] * (1.0 + (a_s[i] - 1.0) * ka_ref[:, lanes[i]]) for i in gs]
        lw = [_split_dot(tri_ref[...], ld_ref[0, rows, lanes[i]]) for i in gs]
        sums = [_dot(jnp.concatenate([kk[i] * kk[i], r[i] * kp[i] * rk_ref[:, lanes[i]]], axis=0).astype(BF16),
                     mbd_ref[...]) for i in gs]
        kk = [kk[i] / jnp.maximum(jnp.sqrt(sums[i][:C]), 1e-12) for i in gs]
        bb = [kk[i] * a_s[i] for i in gs]
        lw_last = [lw[i][C - 1:C, :] for i in gs]
        w_inv = [jnp.exp(-lw[i]) for i in gs]
        w_tail = [jnp.exp(lw_last[i] - lw[i]) for i in gs]
        a_t = [(-kk[i] * jnp.exp(lw[i] - ld_ref[0, rows, lanes[i]])).astype(BF16) for i in gs]
        r_t = [(r[i] * jnp.exp(lw[i])).astype(BF16) for i in gs]

        sc = [_dot_nt(jnp.concatenate([a_t[i], r_t[i]], axis=0),
                      jnp.concatenate([bd(bb[i] * w_inv[i]), bd(kp[i] * w_inv[i])], axis=0)) for i in gs]
        l_ab = [sc[i][:C, :G] * strict for i in gs]
        l_ak = [(sc[i][:C, G:] * strict).astype(BF16) for i in gs]
        l_rbk = [jnp.concatenate([sc[i][C:, :G] * incl, sc[i][C:, G:] * incl], axis=1).astype(BF16) for i in gs]
        v_bd = [bd(v[i]) for i in gs]
        lv = [_dot(l_ak[i], v_bd[i]) for i in gs]

        p = [l_ab[i].astype(BF16) for i in gs]
        t = [eye + l_ab[i] for i in gs]
        power = 1
        while power < C:
            p_bd = [bd(p[i]) for i in gs]
            if power == 1:
                p = [_dot(p[i], p_bd[i]).astype(BF16) for i in gs]
            elif power * 2 < C:
                res = [_dot(jnp.concatenate([p[i], t[i].astype(BF16)], axis=0), p_bd[i]) for i in gs]
                p = [res[i][:C].astype(BF16) for i in gs]
                t = [t[i] + res[i][C:] for i in gs]
            else:
                t = [t[i] + _dot(t[i].astype(BF16), p_bd[i]) for i in gs]
            power *= 2

        ta = [_dot(t[i].astype(BF16), jnp.concatenate([bd(a_t[i]), bd(lv[i])], axis=1)) for i in gs]
        s0 = [s_ref[i] for i in gs]
        res = [_dot_nt(jnp.concatenate([ta[i][:, :G].astype(BF16), r_t[i]], axis=0), s0[i].astype(BF16)) for i in gs]
        uv_stack = [jnp.concatenate([bd(res[i][:C] + ta[i][:, G:]), v_bd[i]], axis=0) for i in gs]
        y = [res[i][C:] + _dot(l_rbk[i], uv_stack[i]) for i in gs]
        for i in gs:
            s_ref[i] = s0[i] * jnp.exp(lw_last[i]) + _dot_tn(
                uv_stack[i], jnp.concatenate([bd(bb[i] * w_tail[i]), bd(kp[i] * w_tail[i])], axis=0))

        mu = [_split_dot_rhs(y[i], mbd_ref[...]) * (1.0 / N) for i in gs]
        dy = [y[i] - mu[i] for i in gs]
        var = [_dot((dy[i] * dy[i]).astype(BF16), mbd_ref[...]) * (1.0 / N) for i in gs]
        for i in gs:
            yn = dy[i] * lax.rsqrt(var[i] + LNX_EPS) * lg_ref[:, lanes[i]] + lb_ref[:, lanes[i]]
            o_ref[0, rows, lanes[i]] = ((yn + sums[i][C:] * v[i]) * g_ref[0, rows, lanes[i]].astype(F32)).astype(BF16)
        return carry

    lax.fori_loop(0, n_chunks, chunk, 0)


def _wkv_tables():
    C, G, N = WKV_CHUNK, WKV_GROUP, RWKV_HEAD
    idx = jnp.arange(G)
    mbd = (idx[:, None] // N == idx[None, :] // N).astype(BF16)
    t = jnp.arange(C)[:, None]
    s = (idx % N)[None, :]
    masks = jnp.stack([s < t, s <= t, s == t]).astype(F32)
    tri = (jnp.arange(C)[None, :] <= t).astype(BF16)
    return mbd, masks, tri


def _wkv(r, k, v, ld, a, g, v_first, vg, k_k, k_a, r_k, lnx_g, lnx_b):
    b, lp, d = r.shape
    n_groups = next(n for n in (WKV_GROUPS_PER_STEP, 2, 1) if d % (n * WKV_GROUP) == 0)
    width = n_groups * WKV_GROUP
    n_blocks = 2 if lp % (2 * WKV_CHUNK) == 0 else 1
    tl = lp // n_blocks
    has_vres = v_first is not None
    seq = pl.BlockSpec((1, tl, width), lambda bi, hg, c: (bi, c, hg))
    par = pl.BlockSpec((1, width), lambda bi, hg, c: (0, hg))
    args = [r, k, v, ld, a, g] + ([v_first, vg] if has_vres else [])
    params = [p.reshape(1, d) for p in (k_k, k_a, r_k, lnx_g, lnx_b)]
    tables = _wkv_tables()
    table_specs = [pl.BlockSpec(t.shape, lambda bi, hg, c, nd=t.ndim: (0,) * nd) for t in tables]
    return pl.pallas_call(
        functools.partial(_wkv_kernel, n_chunks=tl // WKV_CHUNK, n_groups=n_groups, has_vres=has_vres),
        out_shape=jax.ShapeDtypeStruct((b, lp, d), BF16),
        grid=(b, d // width, n_blocks),
        in_specs=[seq] * len(args) + [par] * len(params) + table_specs,
        out_specs=seq,
        scratch_shapes=[pltpu.VMEM((n_groups, WKV_GROUP, WKV_GROUP), F32)],
        compiler_params=_cparams(("parallel", "parallel", "arbitrary")),
        name="wkv7",
    )(*args, *params, *tables)


def _rwkv_layer(h3, j, v_first, p):
    b, lp, d = h3.shape
    m = b * lp
    hn, xx = _norm_shift(h3, p["norm_mix_pre"])
    hn2, xx2 = hn.reshape(m, d), xx.reshape(m, d)
    mix = p["rwkv_mix"][j]
    rkv = _rkv_proj(hn2, xx2, mix[jnp.array([0, 2, 3])].reshape(3, 1, d), p["rwkv_w_rkv"][j].astype(BF16))
    r, k, v = (rkv[s].reshape(b, lp, d) for s in range(3))
    vres = None if j == 0 else (p["rwkv_v0"][j - 1], p["rwkv_v1"][j - 1], p["rwkv_v2"][j - 1])
    lora = _lora(hn2, xx2, mix[jnp.array([1, 4, 5, 3])], p["rwkv_w0"][j], p["rwkv_w1"][j], p["rwkv_w2"][j],
                 p["rwkv_a0"][j], p["rwkv_a1"][j], p["rwkv_a2"][j], p["rwkv_g1"][j], p["rwkv_g2"][j], vres)
    ld, a, g = (t.reshape(b, lp, d) for t in lora[:3])
    vg = lora[3].reshape(b, lp, d) if vres is not None else None
    if vres is None:
        v_first = v
    yg = _wkv(r, k, v, ld, a, g, v_first if vres is not None else None, vg,
              p["rwkv_k_k"][j], p["rwkv_k_a"][j], p["rwkv_r_k"][j].reshape(d),
              p["rwkv_lnx_g"][j], p["rwkv_lnx_b"][j])
    h2 = _proj_residual(yg.reshape(m, d), p["rwkv_w_o"][j].astype(BF16), h3.reshape(m, d), p["norm_mix_post"])
    return h2.reshape(b, lp, d), v_first


def _mla_proj_kernel(h_ref, g_ref, win_ref, qn_ref, wuq_ref, kvn_ref, wukv_ref, cos_ref, sin_ref,
                     q_ref, kv_ref, kpe_ref, *, scale):
    hn = _rms(h_ref[...], g_ref[...]).astype(BF16)
    lat = _dot(hn, win_ref[...])
    cq = _rms(lat[:, :Q_LORA], qn_ref[...]).astype(BF16)
    ckv = _rms(lat[:, Q_LORA:Q_LORA + KV_LORA], kvn_ref[...]).astype(BF16)
    cos, sin = cos_ref[...], sin_ref[...]

    def rope(x):
        return x * cos + pltpu.roll(x, ROPE_D // 2, axis=1) * sin

    keep = (lax.broadcasted_iota(jnp.int32, (1, LANES), 1) < ROPE_D).astype(F32)
    kpe_ref[...] = (rope(lat[:, Q_LORA + KV_LORA:]) * keep).astype(BF16)
    kv_ref[...] = _dot(ckv, wukv_ref[...]).astype(BF16)
    q = _dot(cq, wuq_ref[...])
    hw = NOPE_D + LANES
    for h in range(MLA_HEADS):
        q_ref[:, h * hw:h * hw + NOPE_D] = (q[:, h * hw:h * hw + NOPE_D] * scale).astype(BF16)
        q_ref[:, h * hw + NOPE_D:(h + 1) * hw] = (rope(q[:, h * hw + NOPE_D:(h + 1) * hw]) * scale).astype(BF16)


def _mla_attn_kernel(q_ref, kv_ref, kpe_ref, o_ref, kcat_ref):
    lp = q_ref.shape[1]
    kcat_ref[:, :NOPE_D] = kv_ref[0, :, :NOPE_D]
    kcat_ref[:, NOPE_D:] = kpe_ref[0]
    for q0 in range(0, lp, ATTN_Q_TILE):
        q1 = min(q0 + ATTN_Q_TILE, lp)
        s = _dot_nt(q_ref[0, q0:q1, :], kcat_ref[:q1, :])
        qpos = q0 + lax.broadcasted_iota(jnp.int32, s.shape, 0)
        kpos = lax.broadcasted_iota(jnp.int32, s.shape, 1)
        s = jnp.where(kpos <= qpos, s, -1e30)
        s = s - jnp.max(s, axis=-1, keepdims=True)
        e = jnp.exp(s)
        den = jnp.sum(e, axis=-1, keepdims=True)
        o = _dot(e.astype(BF16), kv_ref[0, :q1, NOPE_D:])
        o_ref[0, q0:q1, :] = (o / den).astype(BF16)


def _rope_tables(b, lp):
    half = ROPE_D // 2
    inv_freq = 1.0 / (ROPE_THETA ** (jnp.arange(0, ROPE_D, 2, dtype=F32) / ROPE_D))
    ang = jnp.arange(lp, dtype=jnp.int32).astype(F32)[:, None] * inv_freq[None, :]
    cos, sin = jnp.cos(ang), jnp.sin(ang)
    cos4 = jnp.tile(cos, (b, LANES // half))
    sin4 = jnp.tile(jnp.concatenate([-sin, sin], axis=1), (b, LANES // ROPE_D))
    return cos4, sin4


def _mla_layer(h3, j, p):
    b, lp, d = h3.shape
    m = b * lp
    hq = MLA_HEADS
    hw = NOPE_D + LANES
    bm = _row_tile(m)
    w_in = p["mla_w_in"][j]
    w_in = jnp.concatenate([w_in, w_in[:, Q_LORA + KV_LORA:]], axis=1).astype(BF16)
    w_uq = p["mla_w_uq"][j].reshape(Q_LORA, hq, NOPE_D + ROPE_D)
    w_uq = jnp.concatenate([w_uq, w_uq[:, :, NOPE_D:]], axis=2).reshape(Q_LORA, hq * hw).astype(BF16)
    w_ukv = p["mla_w_ukv"][j].astype(BF16)
    cos4, sin4 = _rope_tables(b, lp)
    scale = float((NOPE_D + ROPE_D) ** -0.5)

    def full(a):
        return pl.BlockSpec(a.shape, lambda i: (0, 0))

    consts = [p["norm_mix_pre"].reshape(1, d), w_in, p["mla_q_norm"][j].reshape(1, Q_LORA), w_uq,
              p["mla_kv_norm"][j].reshape(1, KV_LORA), w_ukv]
    q, kv, kpe = pl.pallas_call(
        functools.partial(_mla_proj_kernel, scale=scale),
        out_shape=(jax.ShapeDtypeStruct((m, hq * hw), BF16),
                   jax.ShapeDtypeStruct((m, hq * (NOPE_D + V_D)), BF16),
                   jax.ShapeDtypeStruct((m, LANES), BF16)),
        grid=(m // bm,),
        in_specs=[pl.BlockSpec((bm, d), lambda i: (i, 0))] + [full(a) for a in consts]
                 + [pl.BlockSpec((bm, LANES), lambda i: (i, 0))] * 2,
        out_specs=(pl.BlockSpec((bm, hq * hw), lambda i: (i, 0)),
                   pl.BlockSpec((bm, hq * (NOPE_D + V_D)), lambda i: (i, 0)),
                   pl.BlockSpec((bm, LANES), lambda i: (i, 0))),
        compiler_params=_cparams(("parallel",)),
        name="mla_proj",
    )(h3.reshape(m, d), *consts, cos4, sin4)

    o = pl.pallas_call(
        _mla_attn_kernel,
        out_shape=jax.ShapeDtypeStruct((b, lp, hq * V_D), BF16),
        grid=(b, hq),
        in_specs=[pl.BlockSpec((1, lp, hw), lambda bi, h: (bi, 0, h)),
                  pl.BlockSpec((1, lp, NOPE_D + V_D), lambda bi, h: (bi, 0, h)),
                  pl.BlockSpec((1, lp, LANES), lambda bi, h: (bi, 0, 0))],
        out_specs=pl.BlockSpec((1, lp, V_D), lambda bi, h: (bi, 0, h)),
        scratch_shapes=[pltpu.VMEM((lp, hw), BF16)],
        compiler_params=_cparams(("parallel", "parallel")),
        name="mla_attention",
    )(q.reshape(b, lp, hq * hw), kv.reshape(b, lp, hq * (NOPE_D + V_D)), kpe.reshape(b, lp, LANES))

    h2 = _proj_residual(o.reshape(m, hq * V_D), p["mla_w_o"][j].astype(BF16), h3.reshape(m, d),
                        p["norm_mix_post"])
    return h2.reshape(b, lp, d)


def _pool_kernel(x_ref, w_ref, sc_ref, o_ref):
    gi = pl.program_id(1)
    x = x_ref[0]
    t = lax.broadcasted_iota(jnp.int32, x.shape, 0)
    for idx, win in enumerate(POOL_WINDOWS):
        @pl.when(gi == idx)
        def _(win=win):
            s = x
            step = 1
            while step < win:
                s = s + jnp.where(t >= step, pltpu.roll(s, step, axis=0), 0.0)
                step *= 2
            cnt = jnp.minimum(t + 1, win).astype(F32)
            pooled = (s / cnt - x).astype(BF16)
            o_ref[0] = _dot(pooled, w_ref[0]) * sc_ref[...]


def _pool_layer(h3, j, p):
    b, lp, d = h3.shape
    m = b * lp
    ng = len(POOL_WINDOWS)
    gw = d // ng
    hn = _prenorm(h3.reshape(m, d), p["norm_mix_pre"], F32).reshape(b, lp, d)
    y = pl.pallas_call(
        _pool_kernel,
        out_shape=jax.ShapeDtypeStruct((b, lp, d), F32),
        grid=(b, ng),
        in_specs=[pl.BlockSpec((1, lp, gw), lambda bi, g: (bi, 0, g)),
                  pl.BlockSpec((1, gw, gw), lambda bi, g: (g, 0, 0)),
                  pl.BlockSpec((1, gw), lambda bi, g: (0, g))],
        out_specs=pl.BlockSpec((1, lp, gw), lambda bi, g: (bi, 0, g)),
        compiler_params=_cparams(("parallel", "parallel")),
        name="pool_mix",
    )(hn, p["pool_w"][j].astype(BF16), p["pool_scale"][j].reshape(1, d))
    h2 = _postnorm_residual(h3.reshape(m, d), y.reshape(m, d), p["norm_mix_post"])
    return h2.reshape(b, lp, d)


def kernel(x, meta_tokens, norm_mix_pre, norm_mix_post, norm_ffn_pre, norm_ffn_post, ffn_w_gate, ffn_w_up, ffn_w_down, rwkv_mix, rwkv_w0, rwkv_w1, rwkv_w2, rwkv_a0, rwkv_a1, rwkv_a2, rwkv_v0, rwkv_v1, rwkv_v2, rwkv_g1, rwkv_g2, rwkv_k_k, rwkv_k_a, rwkv_r_k, rwkv_w_rkv, rwkv_w_o, rwkv_lnx_g, rwkv_lnx_b, mla_w_in, mla_q_norm, mla_w_uq, mla_kv_norm, mla_w_ukv, mla_w_o, pool_w, pool_scale):
    b, seq, d = x.shape
    depth = norm_mix_pre.shape[0]
    ltot = N_META + seq
    lp = -(-ltot // SEQ_ALIGN) * SEQ_ALIGN
    meta = jnp.broadcast_to(meta_tokens.astype(x.dtype)[None], (b, N_META, d))
    h = jnp.concatenate([meta, x, jnp.zeros((b, lp - ltot, d), x.dtype)], axis=1)
    shared = dict(rwkv_mix=rwkv_mix, rwkv_w0=rwkv_w0, rwkv_w1=rwkv_w1, rwkv_w2=rwkv_w2, rwkv_a0=rwkv_a0,
                  rwkv_a1=rwkv_a1, rwkv_a2=rwkv_a2, rwkv_v0=rwkv_v0, rwkv_v1=rwkv_v1, rwkv_v2=rwkv_v2,
                  rwkv_g1=rwkv_g1, rwkv_g2=rwkv_g2, rwkv_k_k=rwkv_k_k, rwkv_k_a=rwkv_k_a, rwkv_r_k=rwkv_r_k,
                  rwkv_w_rkv=rwkv_w_rkv, rwkv_w_o=rwkv_w_o, rwkv_lnx_g=rwkv_lnx_g, rwkv_lnx_b=rwkv_lnx_b,
                  mla_w_in=mla_w_in, mla_q_norm=mla_q_norm, mla_w_uq=mla_w_uq, mla_kv_norm=mla_kv_norm,
                  mla_w_ukv=mla_w_ukv, mla_w_o=mla_w_o, pool_w=pool_w, pool_scale=pool_scale)
    v_first = None
    for i in range(depth):
        p = dict(shared, norm_mix_pre=norm_mix_pre[i], norm_mix_post=norm_mix_post[i])
        kind, j = i % 3, i // 3
        if kind == 0:
            h, v_first = _rwkv_layer(h, j, v_first, p)
        elif kind == 1:
            h = _mla_layer(h, j, p)
        else:
            h = _pool_layer(h, j, p)
        h = _ffn(h.reshape(b * lp, d), norm_ffn_pre[i], norm_ffn_post[i], ffn_w_gate[i].astype(BF16),
                 ffn_w_up[i].astype(BF16), ffn_w_down[i].astype(BF16)).reshape(b, lp, d)
    return h[:, N_META:ltot]
```

```python
import functools

import jax
import jax.numpy as jnp
from jax import lax
from jax.experimental import pallas as pl
from jax.experimental.pallas import tpu as pltpu

F32 = jnp.float32
BF16 = jnp.bfloat16

N_META = 16
RMS_EPS = 1e-6
RWKV_HEAD = 64
LNX_EPS = 1e-5 * RWKV_HEAD
MLA_HEADS = 16
Q_LORA = 512
KV_LORA = 512
NOPE_D = 128
ROPE_D = 64
V_D = 128
ROPE_THETA = 10000.0
POOL_WINDOWS = (2, 4, 8, 16)

LANES = 128
WKV_CHUNK = 64
SEQ_ALIGN = WKV_CHUNK
WKV_HEADS_PER_GROUP = 4
WKV_GROUP = WKV_HEADS_PER_GROUP * RWKV_HEAD
WKV_GROUPS_PER_STEP = 8
WKV_CHUNKS_PER_STEP = 3
ATTN_Q_TILE = 256
VMEM_LIMIT = 56 * 1024 * 1024


def _cparams(sem):
    return pltpu.CompilerParams(dimension_semantics=sem, vmem_limit_bytes=VMEM_LIMIT)


def _dot(a, b):
    return jnp.dot(a, b, preferred_element_type=F32)


def _dot_nt(a, b):
    return lax.dot_general(a, b, (((1,), (1,)), ((), ())), preferred_element_type=F32)


def _dot_tn(a, b):
    return lax.dot_general(a, b, (((0,), (0,)), ((), ())), preferred_element_type=F32)


def _rms(x, g, eps=RMS_EPS):
    ms = jnp.mean(x * x, axis=-1, keepdims=True)
    return x * lax.rsqrt(ms + eps) * g


def _sigmoid(x):
    return 1.0 / (1.0 + jnp.exp(-x))


def _row_tile(m, cap=768):
    for t in (768, 512, 384, 256, 128, 64, 32, 16):
        if t <= cap and m % t == 0:
            return t
    raise ValueError(f"unsupported row count {m}")


def _prenorm_kernel(h_ref, g_ref, o_ref):
    o_ref[...] = _rms(h_ref[...], g_ref[...]).astype(o_ref.dtype)


def _prenorm(h2, g, dtype):
    m, d = h2.shape
    bm = _row_tile(m)
    return pl.pallas_call(
        _prenorm_kernel,
        out_shape=jax.ShapeDtypeStruct((m, d), dtype),
        grid=(m // bm,),
        in_specs=[pl.BlockSpec((bm, d), lambda i: (i, 0)),
                  pl.BlockSpec((1, d), lambda i: (0, 0))],
        out_specs=pl.BlockSpec((bm, d), lambda i: (i, 0)),
        compiler_params=_cparams(("parallel",)),
        name="prenorm",
    )(h2, g.reshape(1, d))


def _postnorm_kernel(h_ref, y_ref, g_ref, o_ref):
    o_ref[...] = h_ref[...] + _rms(y_ref[...].astype(F32), g_ref[...])


def _postnorm_residual(h2, y2, g):
    m, d = h2.shape
    bm = _row_tile(m)
    return pl.pallas_call(
        _postnorm_kernel,
        out_shape=jax.ShapeDtypeStruct((m, d), F32),
        grid=(m // bm,),
        in_specs=[pl.BlockSpec((bm, d), lambda i: (i, 0)),
                  pl.BlockSpec((bm, d), lambda i: (i, 0)),
                  pl.BlockSpec((1, d), lambda i: (0, 0))],
        out_specs=pl.BlockSpec((bm, d), lambda i: (i, 0)),
        compiler_params=_cparams(("parallel",)),
        name="postnorm_residual",
    )(h2, y2, g.reshape(1, d))


def _proj_residual_kernel(x_ref, w_ref, h_ref, g_ref, o_ref):
    y = _dot(x_ref[...], w_ref[...])
    o_ref[...] = h_ref[...] + _rms(y, g_ref[...])


def _proj_residual(x2, w, h2, g):
    m, k = x2.shape
    d = w.shape[1]
    bm = _row_tile(m)
    return pl.pallas_call(
        _proj_residual_kernel,
        out_shape=jax.ShapeDtypeStruct((m, d), F32),
        grid=(m // bm,),
        in_specs=[pl.BlockSpec((bm, k), lambda i: (i, 0)),
                  pl.BlockSpec((k, d), lambda i: (0, 0)),
                  pl.BlockSpec((bm, d), lambda i: (i, 0)),
                  pl.BlockSpec((1, d), lambda i: (0, 0))],
        out_specs=pl.BlockSpec((bm, d), lambda i: (i, 0)),
        compiler_params=_cparams(("parallel",)),
        name="proj_residual",
    )(x2, w, h2, g.reshape(1, d))


def _ffn_kernel(h_ref, gpre_ref, gpost_ref, wg_ref, wu_ref, wd_ref, o_ref, xn_ref):
    f = pl.program_id(1)

    @pl.when(f == 0)
    def _():
        xn_ref[...] = _rms(h_ref[...], gpre_ref[...]).astype(BF16)

    x = xn_ref[...]
    gate = _dot(x, wg_ref[0])
    up = _dot(x, wu_ref[0])
    mid = (gate * _sigmoid(gate) * up).astype(BF16)
    part = _dot(mid, wd_ref[...])

    @pl.when(f == 0)
    def _():
        o_ref[...] = part

    @pl.when(jnp.logical_and(f > 0, f < pl.num_programs(1) - 1))
    def _():
        o_ref[...] += part

    @pl.when(f == pl.num_programs(1) - 1)
    def _():
        o_ref[...] = h_ref[...] + _rms(o_ref[...] + part, gpost_ref[...])


FFN_HIDDEN_TILE = 512


def _ffn_weights(wg, wu, wd):
    d, f = wg.shape
    bf = FFN_HIDDEN_TILE if f % FFN_HIDDEN_TILE == 0 else f

    def tiles(w):
        return w.astype(BF16).reshape(d, f // bf, bf).transpose(1, 0, 2)

    return tiles(wg), tiles(wu), wd.astype(BF16)


def _ffn(h2, gpre, gpost, wg_t, wu_t, wd):
    m, d = h2.shape
    nf, _, bf = wg_t.shape
    assert nf >= 2
    bm = _row_tile(m)
    return pl.pallas_call(
        _ffn_kernel,
        out_shape=jax.ShapeDtypeStruct((m, d), F32),
        grid=(m // bm, nf),
        in_specs=[pl.BlockSpec((bm, d), lambda i, j: (i, 0)),
                  pl.BlockSpec((1, d), lambda i, j: (0, 0)),
                  pl.BlockSpec((1, d), lambda i, j: (0, 0)),
                  pl.BlockSpec((1, d, bf), lambda i, j: (j, 0, 0)),
                  pl.BlockSpec((1, d, bf), lambda i, j: (j, 0, 0)),
                  pl.BlockSpec((bf, d), lambda i, j: (j, 0))],
        out_specs=pl.BlockSpec((bm, d), lambda i, j: (i, 0)),
        scratch_shapes=[pltpu.VMEM((bm, d), BF16)],
        compiler_params=_cparams(("parallel", "arbitrary")),
        name="ffn",
    )(h2, gpre.reshape(1, d), gpost.reshape(1, d), wg_t, wu_t, wd)


def _shift_kernel(h_ref, halo_ref, g_ref, hn_ref, xx_ref):
    i = pl.program_id(1)
    g = g_ref[...]
    hn = _rms(h_ref[0], g)
    prev = _rms(halo_ref[0], g)[7:8]
    prev = jnp.where(i == 0, 0.0, prev)
    rolled = pltpu.roll(hn, 1, axis=0)
    row = lax.broadcasted_iota(jnp.int32, hn.shape, 0)
    shifted = jnp.where(row == 0, prev, rolled)
    hn_ref[0] = hn.astype(BF16)
    xx_ref[0] = (shifted - hn).astype(BF16)


def _norm_shift(h3, g):
    b, lp, d = h3.shape
    tl = lp // 4 if lp % 64 == 0 else lp
    halo_blocks = tl // 8
    out = jax.ShapeDtypeStruct((b, lp, d), BF16)
    return pl.pallas_call(
        _shift_kernel,
        out_shape=(out, out),
        grid=(b, lp // tl),
        in_specs=[pl.BlockSpec((1, tl, d), lambda bi, i: (bi, i, 0)),
                  pl.BlockSpec((1, 8, d), lambda bi, i: (bi, jnp.maximum(i * halo_blocks - 1, 0), 0)),
                  pl.BlockSpec((1, d), lambda bi, i: (0, 0))],
        out_specs=(pl.BlockSpec((1, tl, d), lambda bi, i: (bi, i, 0)),
                   pl.BlockSpec((1, tl, d), lambda bi, i: (bi, i, 0))),
        compiler_params=_cparams(("parallel", "parallel")),
        name="norm_shift",
    )(h3, h3, g.reshape(1, d))


def _rkv_kernel(hn_ref, xx_ref, mix_ref, w_ref, o_ref):
    xs = (hn_ref[...].astype(F32) + xx_ref[...].astype(F32) * mix_ref[0]).astype(BF16)
    o_ref[...] = _dot(xs, w_ref[0]).astype(BF16)


def _rkv_proj(hn2, xx2, mix3, w_rkv):
    m, d = hn2.shape
    bm = _row_tile(m)
    return pl.pallas_call(
        _rkv_kernel,
        out_shape=jax.ShapeDtypeStruct((m, 3 * d), BF16),
        grid=(m // bm, 3),
        in_specs=[pl.BlockSpec((bm, d), lambda i, s: (i, 0)),
                  pl.BlockSpec((bm, d), lambda i, s: (i, 0)),
                  pl.BlockSpec((1, 1, d), lambda i, s: (s, 0, 0)),
                  pl.BlockSpec((1, d, d), lambda i, s: (s, 0, 0))],
        out_specs=pl.BlockSpec((bm, d), lambda i, s: (i, s)),
        compiler_params=_cparams(("parallel", "parallel")),
        name="rkv_proj",
    )(hn2, xx2, mix3, w_rkv)


def _lora_kernel(*refs, has_vres):
    if has_vres:
        (hn_ref, xx_ref, mix_ref, w0_ref, w1_ref, w2_ref, a0_ref, a1_ref, a2_ref, g1_ref, g2_ref,
         v0_ref, v1_ref, v2_ref, ld_ref, a_ref, g_ref, vg_ref) = refs
    else:
        (hn_ref, xx_ref, mix_ref, w0_ref, w1_ref, w2_ref, a0_ref, a1_ref, a2_ref, g1_ref, g2_ref,
         ld_ref, a_ref, g_ref) = refs
    hn = hn_ref[...].astype(F32)
    xx = xx_ref[...].astype(F32)

    def mixed(s):
        return (hn + xx * mix_ref[s:s + 1, :]).astype(BF16)

    z = w0_ref[...] + _dot(jnp.tanh(_dot(mixed(0), w1_ref[...])).astype(BF16), w2_ref[...])
    ld_ref[...] = -jnp.exp(F32(-0.5)) * _sigmoid(z)
    za = a0_ref[...] + _dot(_dot(mixed(1), a1_ref[...]).astype(BF16), a2_ref[...])
    a_ref[...] = _sigmoid(za).astype(BF16)
    gl = _sigmoid(_dot(mixed(2), g1_ref[...])).astype(BF16)
    g_ref[...] = _dot(gl, g2_ref[...]).astype(BF16)
    if has_vres:
        zv = v0_ref[...] + _dot(_dot(mixed(3), v1_ref[...]).astype(BF16), v2_ref[...])
        vg_ref[...] = _sigmoid(zv).astype(BF16)


def _pad_lora(w_down, w_up):
    r = w_down.shape[1]
    rp = -(-r // LANES) * LANES
    return (jnp.pad(w_down, ((0, 0), (0, rp - r))).astype(BF16),
            jnp.pad(w_up, ((0, rp - r), (0, 0))).astype(BF16))


def _lora(hn2, xx2, mix, w0, w1, w2, a0, a1, a2, g1, g2, vres):
    m, d = hn2.shape
    bm = _row_tile(m, 384)
    has_vres = vres is not None
    w1p, w2p = _pad_lora(w1, w2)
    a1p, a2p = _pad_lora(a1, a2)
    g1p, g2p = _pad_lora(g1, g2)
    args = [hn2, xx2, mix, w0.reshape(1, d), w1p, w2p, a0.reshape(1, d), a1p, a2p, g1p, g2p]
    if has_vres:
        v0, v1, v2 = vres
        v1p, v2p = _pad_lora(v1, v2)
        args += [v0.reshape(1, d), v1p, v2p]
    row = pl.BlockSpec((bm, d), lambda i: (i, 0))

    def full(a):
        return pl.BlockSpec(a.shape, lambda i: (0, 0))

    in_specs = [row, row] + [full(a) for a in args[2:]]
    outs = [jax.ShapeDtypeStruct((m, d), F32), jax.ShapeDtypeStruct((m, d), BF16),
            jax.ShapeDtypeStruct((m, d), BF16)]
    if has_vres:
        outs.append(jax.ShapeDtypeStruct((m, d), BF16))
    return pl.pallas_call(
        functools.partial(_lora_kernel, has_vres=has_vres),
        out_shape=tuple(outs),
        grid=(m // bm,),
        in_specs=in_specs,
        out_specs=tuple(row for _ in outs),
        compiler_params=_cparams(("parallel",)),
        name="rwkv_lora",
    )(*args)


def _split_dot(a_bf16, x):
    hi = x.astype(BF16)
    lo = (x - hi.astype(F32)).astype(BF16)
    return _dot(a_bf16, hi) + _dot(a_bf16, lo)


def _split_dot_rhs(x, w_bf16):
    hi = x.astype(BF16)
    lo = (x - hi.astype(F32)).astype(BF16)
    return _dot(hi, w_bf16) + _dot(lo, w_bf16)


def _wkv_kernel(*refs, n_chunks, n_groups, has_vres):
    if has_vres:
        (r_ref, k_ref, v_ref, ld_ref, a_ref, g_ref, vf_ref, vg_ref,
         kk_ref, ka_ref, rk_ref, lg_ref, lb_ref, mbd_ref, msk_ref, tri_ref, o_ref, s_ref) = refs
    else:
        (r_ref, k_ref, v_ref, ld_ref, a_ref, g_ref,
         kk_ref, ka_ref, rk_ref, lg_ref, lb_ref, mbd_ref, msk_ref, tri_ref, o_ref, s_ref) = refs
    C, G, HG, N = WKV_CHUNK, WKV_GROUP, WKV_HEADS_PER_GROUP, RWKV_HEAD

    @pl.when(pl.program_id(2) == 0)
    def _():
        s_ref[...] = jnp.zeros_like(s_ref)

    def bd(x):
        xb = x.astype(BF16)
        return jnp.concatenate([xb] * HG, axis=0) * mbd_ref[...]

    gs = range(n_groups)
    lanes = [slice(gi * G, (gi + 1) * G) for gi in gs]

    def group_rows(x, n):
        return [x[i * n:(i + 1) * n] for i in gs]

    def chunk(j, carry):
        rows = pl.ds(pl.multiple_of(j * C, C), C)
        strict, incl, eye = msk_ref[0], msk_ref[1], msk_ref[2]
        r = [r_ref[0, rows, lanes[i]].astype(F32) for i in gs]
        k = [k_ref[0, rows, lanes[i]].astype(F32) for i in gs]
        v = [v_ref[0, rows, lanes[i]].astype(F32) for i in gs]
        a_s = [a_ref[0, rows, lanes[i]].astype(F32) for i in gs]
        if has_vres:
            v = [v[i] + (vf_ref[0, rows, lanes[i]].astype(F32) - v[i]) * vg_ref[0, rows, lanes[i]].astype(F32)
                 for i in gs]
        kk = [k[i] * kk_ref[:, lanes[i]] for i in gs]
        kp = [k[i] * (1.0 + (a_s[i] - 1.0) * ka_ref[:, lanes[i]]) for i in gs]
        lw = [_split_dot(tri_ref[...], ld_ref[0, rows, lanes[i]]) for i in gs]
        sums = group_rows(_dot(jnp.concatenate(
            [jnp.concatenate([kk[i] * kk[i], r[i] * kp[i] * rk_ref[:, lanes[i]]], axis=0).astype(BF16) for i in gs],
            axis=0), mbd_ref[...]), 2 * C)
        kk = [kk[i] / jnp.maximum(jnp.sqrt(sums[i][:C]), 1e-12) for i in gs]
        bb = [kk[i] * a_s[i] for i in gs]
        lw_last = [lw[i][C - 1:C, :] for i in gs]
        w_inv = [jnp.exp(-lw[i]) for i in gs]
        w_tail = [jnp.exp(lw_last[i] - lw[i]) for i in gs]
        a_t = [(-kk[i] * jnp.exp(lw[i] - ld_ref[0, rows, lanes[i]])).astype(BF16) for i in gs]
        r_t = [(r[i] * jnp.exp(lw[i])).astype(BF16) for i in gs]

        sc = [_dot_nt(jnp.concatenate([a_t[i], r_t[i]], axis=0),
                      jnp.concatenate([bd(bb[i] * w_inv[i]), bd(kp[i] * w_inv[i])], axis=0)) for i in gs]
        l_ab = [sc[i][:C, :G] * strict for i in gs]
        l_ak = [(sc[i][:C, G:] * strict).astype(BF16) for i in gs]
        l_rbk = [jnp.concatenate([sc[i][C:, :G] * incl, sc[i][C:, G:] * incl], axis=1).astype(BF16) for i in gs]
        v_bd = [bd(v[i]) for i in gs]
        lv = [_dot(l_ak[i], v_bd[i]) for i in gs]

        p = [l_ab[i].astype(BF16) for i in gs]
        t = [eye + l_ab[i] for i in gs]
        power = 1
        while power < C:
            p_bd = [bd(p[i]) for i in gs]
            if power == 1:
                p = [_dot(p[i], p_bd[i]).astype(BF16) for i in gs]
            elif power * 2 < C:
                res = [_dot(jnp.concatenate([p[i], t[i].astype(BF16)], axis=0), p_bd[i]) for i in gs]
                p = [res[i][:C].astype(BF16) for i in gs]
                t = [t[i] + res[i][C:] for i in gs]
            else:
                t = [t[i] + _dot(t[i].astype(BF16), p_bd[i]) for i in gs]
            power *= 2

        ta = [_dot(t[i].astype(BF16), jnp.concatenate([bd(a_t[i]), bd(lv[i])], axis=1)) for i in gs]
        s0 = [s_ref[i] for i in gs]
        res = [_dot_nt(jnp.concatenate([ta[i][:, :G].astype(BF16), r_t[i]], axis=0), s0[i].astype(BF16)) for i in gs]
        uv_stack = [jnp.concatenate([bd(res[i][:C] + ta[i][:, G:]), v_bd[i]], axis=0) for i in gs]
        y = [res[i][C:] + _dot(l_rbk[i], uv_stack[i]) for i in gs]
        for i in gs:
            s_ref[i] = s0[i] * jnp.exp(lw_last[i]) + _dot_tn(
                uv_stack[i], jnp.concatenate([bd(bb[i] * w_tail[i]), bd(kp[i] * w_tail[i])], axis=0))

        y_hi = [y[i].astype(BF16) for i in gs]
        y_lo = [(y[i] - y_hi[i].astype(F32)).astype(BF16) for i in gs]
        mu2 = group_rows(_dot(jnp.concatenate([jnp.concatenate([y_hi[i], y_lo[i]], axis=0) for i in gs], axis=0),
                              mbd_ref[...]), 2 * C)
        dy = [y[i] - (mu2[i][:C] + mu2[i][C:]) * (1.0 / N) for i in gs]
        var = group_rows(_dot(jnp.concatenate([(dy[i] * dy[i]).astype(BF16) for i in gs], axis=0), mbd_ref[...]), C)
        var = [var[i] * (1.0 / N) for i in gs]
        for i in gs:
            yn = dy[i] * lax.rsqrt(var[i] + LNX_EPS) * lg_ref[:, lanes[i]] + lb_ref[:, lanes[i]]
            o_ref[0, rows, lanes[i]] = ((yn + sums[i][C:] * v[i]) * g_ref[0, rows, lanes[i]].astype(F32)).astype(BF16)
        return carry

    lax.fori_loop(0, n_chunks, chunk, 0)


def _wkv_tables():
    C, G, N = WKV_CHUNK, WKV_GROUP, RWKV_HEAD
    idx = jnp.arange(G)
    mbd = (idx[:, None] // N == idx[None, :] // N).astype(BF16)
    t = jnp.arange(C)[:, None]
    s = (idx % N)[None, :]
    masks = jnp.stack([s < t, s <= t, s == t]).astype(F32)
    tri = (jnp.arange(C)[None, :] <= t).astype(BF16)
    return mbd, masks, tri


def _wkv(rkv, ld, a, g, rkv_first, vg, k_k, k_a, r_k, lnx_g, lnx_b):
    b, lp, d = ld.shape
    n_groups = next(n for n in (WKV_GROUPS_PER_STEP, 4, 2, 1) if d % (n * WKV_GROUP) == 0)
    width = n_groups * WKV_GROUP
    n_lane_blocks = d // width
    n_chunks = next(n for n in (WKV_CHUNKS_PER_STEP, 2, 1) if (lp // WKV_CHUNK) % n == 0)
    tl = n_chunks * WKV_CHUNK
    has_vres = rkv_first is not None

    def seq(col):
        return pl.BlockSpec((1, tl, width), lambda bi, hg, c: (bi, c, hg + col * n_lane_blocks))

    par = pl.BlockSpec((1, width), lambda bi, hg, c: (0, hg))
    args = [rkv, rkv, rkv, ld, a, g] + ([rkv_first, vg] if has_vres else [])
    specs = [seq(0), seq(1), seq(2), seq(0), seq(0), seq(0)] + ([seq(2), seq(0)] if has_vres else [])
    params = [p.reshape(1, d) for p in (k_k, k_a, r_k, lnx_g, lnx_b)]
    tables = _wkv_tables()
    table_specs = [pl.BlockSpec(t.shape, lambda bi, hg, c, nd=t.ndim: (0,) * nd) for t in tables]
    return pl.pallas_call(
        functools.partial(_wkv_kernel, n_chunks=n_chunks, n_groups=n_groups, has_vres=has_vres),
        out_shape=jax.ShapeDtypeStruct((b, lp, d), BF16),
        grid=(b, n_lane_blocks, lp // tl),
        in_specs=specs + [par] * len(params) + table_specs,
        out_specs=seq(0),
        scratch_shapes=[pltpu.VMEM((n_groups, WKV_GROUP, WKV_GROUP), F32)],
        compiler_params=_cparams(("parallel", "parallel", "arbitrary")),
        name="wkv7",
    )(*args, *params, *tables)


def _rwkv_layer(h3, j, rkv_first, p):
    b, lp, d = h3.shape
    m = b * lp
    hn, xx = _norm_shift(h3, p["norm_mix_pre"])
    hn2, xx2 = hn.reshape(m, d), xx.reshape(m, d)
    mix = p["rwkv_mix"][j]
    rkv = _rkv_proj(hn2, xx2, jnp.stack([mix[0], mix[2], mix[3]]).reshape(3, 1, d),
                    p["rwkv_w_rkv"][j].astype(BF16)).reshape(b, lp, 3 * d)
    vres = None if j == 0 else (p["rwkv_v0"][j - 1], p["rwkv_v1"][j - 1], p["rwkv_v2"][j - 1])
    lora = _lora(hn2, xx2, jnp.stack([mix[1], mix[4], mix[5], mix[3]]), p["rwkv_w0"][j], p["rwkv_w1"][j],
                 p["rwkv_w2"][j], p["rwkv_a0"][j], p["rwkv_a1"][j], p["rwkv_a2"][j], p["rwkv_g1"][j],
                 p["rwkv_g2"][j], vres)
    ld, a, g = (t.reshape(b, lp, d) for t in lora[:3])
    vg = lora[3].reshape(b, lp, d) if vres is not None else None
    if vres is None:
        rkv_first = rkv
    yg = _wkv(rkv, ld, a, g, rkv_first if vres is not None else None, vg,
              p["rwkv_k_k"][j], p["rwkv_k_a"][j], p["rwkv_r_k"][j].reshape(d),
              p["rwkv_lnx_g"][j], p["rwkv_lnx_b"][j])
    h2 = _proj_residual(yg.reshape(m, d), p["rwkv_w_o"][j].astype(BF16), h3.reshape(m, d), p["norm_mix_post"])
    return h2.reshape(b, lp, d), rkv_first


def _mla_proj_kernel(h_ref, g_ref, win_ref, qn_ref, wuq_ref, kvn_ref, wukv_ref, cos_ref, sin_ref,
                     q_ref, kv_ref, kpe_ref, *, scale):
    hn = _rms(h_ref[...], g_ref[...]).astype(BF16)
    lat = _dot(hn, win_ref[...])
    cq = _rms(lat[:, :Q_LORA], qn_ref[...]).astype(BF16)
    ckv = _rms(lat[:, Q_LORA:Q_LORA + KV_LORA], kvn_ref[...]).astype(BF16)
    cos, sin = cos_ref[...], sin_ref[...]

    def rope(x):
        return x * cos + pltpu.roll(x, ROPE_D // 2, axis=1) * sin

    keep = (lax.broadcasted_iota(jnp.int32, (1, LANES), 1) < ROPE_D).astype(F32)
    kpe_ref[...] = (rope(lat[:, Q_LORA + KV_LORA:]) * keep).astype(BF16)
    kv_ref[...] = _dot(ckv, wukv_ref[...]).astype(BF16)
    q = _dot(cq, wuq_ref[...])
    hw = NOPE_D + LANES
    for h in range(MLA_HEADS):
        q_ref[:, h * hw:h * hw + NOPE_D] = (q[:, h * hw:h * hw + NOPE_D] * scale).astype(BF16)
        q_ref[:, h * hw + NOPE_D:(h + 1) * hw] = (rope(q[:, h * hw + NOPE_D:(h + 1) * hw]) * scale).astype(BF16)


def _mla_attn_kernel(q_ref, kv_ref, kpe_ref, o_ref, kcat_ref):
    lp = q_ref.shape[1]
    kcat_ref[:, :NOPE_D] = kv_ref[0, :, :NOPE_D]
    kcat_ref[:, NOPE_D:] = kpe_ref[0]
    for q0 in range(0, lp, ATTN_Q_TILE):
        q1 = min(q0 + ATTN_Q_TILE, lp)
        s = _dot_nt(q_ref[0, q0:q1, :], kcat_ref[:q1, :])
        qpos = q0 + lax.broadcasted_iota(jnp.int32, s.shape, 0)
        kpos = lax.broadcasted_iota(jnp.int32, s.shape, 1)
        s = jnp.where(kpos <= qpos, s, -1e30)
        s = s - jnp.max(s, axis=-1, keepdims=True)
        e = jnp.exp(s)
        den = jnp.sum(e, axis=-1, keepdims=True)
        o = _dot(e.astype(BF16), kv_ref[0, :q1, NOPE_D:])
        o_ref[0, q0:q1, :] = (o / den).astype(BF16)


def _rope_tables(b, lp):
    half = ROPE_D // 2
    inv_freq = 1.0 / (ROPE_THETA ** (jnp.arange(0, ROPE_D, 2, dtype=F32) / ROPE_D))
    ang = jnp.arange(lp, dtype=jnp.int32).astype(F32)[:, None] * inv_freq[None, :]
    cos, sin = jnp.cos(ang), jnp.sin(ang)
    cos4 = jnp.tile(cos, (b, LANES // half))
    sin4 = jnp.tile(jnp.concatenate([-sin, sin], axis=1), (b, LANES // ROPE_D))
    return cos4, sin4


def _mla_layer(h3, j, p):
    b, lp, d = h3.shape
    m = b * lp
    hq = MLA_HEADS
    hw = NOPE_D + LANES
    bm = _row_tile(m, 384)
    w_in = p["mla_w_in"][j]
    w_in = jnp.concatenate([w_in, w_in[:, Q_LORA + KV_LORA:]], axis=1).astype(BF16)
    w_uq = p["mla_w_uq"][j].reshape(Q_LORA, hq, NOPE_D + ROPE_D)
    w_uq = jnp.concatenate([w_uq, w_uq[:, :, NOPE_D:]], axis=2).reshape(Q_LORA, hq * hw).astype(BF16)
    w_ukv = p["mla_w_ukv"][j].astype(BF16)
    cos4, sin4 = _rope_tables(b, lp)
    scale = float((NOPE_D + ROPE_D) ** -0.5)

    def full(a):
        return pl.BlockSpec(a.shape, lambda i: (0, 0))

    consts = [p["norm_mix_pre"].reshape(1, d), w_in, p["mla_q_norm"][j].reshape(1, Q_LORA), w_uq,
              p["mla_kv_norm"][j].reshape(1, KV_LORA), w_ukv]
    q, kv, kpe = pl.pallas_call(
        functools.partial(_mla_proj_kernel, scale=scale),
        out_shape=(jax.ShapeDtypeStruct((m, hq * hw), BF16),
                   jax.ShapeDtypeStruct((m, hq * (NOPE_D + V_D)), BF16),
                   jax.ShapeDtypeStruct((m, LANES), BF16)),
        grid=(m // bm,),
        in_specs=[pl.BlockSpec((bm, d), lambda i: (i, 0))] + [full(a) for a in consts]
                 + [pl.BlockSpec((bm, LANES), lambda i: (i, 0))] * 2,
        out_specs=(pl.BlockSpec((bm, hq * hw), lambda i: (i, 0)),
                   pl.BlockSpec((bm, hq * (NOPE_D + V_D)), lambda i: (i, 0)),
                   pl.BlockSpec((bm, LANES), lambda i: (i, 0))),
        compiler_params=_cparams(("parallel",)),
        name="mla_proj",
    )(h3.reshape(m, d), *consts, cos4, sin4)

    o = pl.pallas_call(
        _mla_attn_kernel,
        out_shape=jax.ShapeDtypeStruct((b, lp, hq * V_D), BF16),
        grid=(b, hq),
        in_specs=[pl.BlockSpec((1, lp, hw), lambda bi, h: (bi, 0, h)),
                  pl.BlockSpec((1, lp, NOPE_D + V_D), lambda bi, h: (bi, 0, h)),
                  pl.BlockSpec((1, lp, LANES), lambda bi, h: (bi, 0, 0))],
        out_specs=pl.BlockSpec((1, lp, V_D), lambda bi, h: (bi, 0, h)),
        scratch_shapes=[pltpu.VMEM((lp, hw), BF16)],
        compiler_params=_cparams(("parallel", "parallel")),
        name="mla_attention",
    )(q.reshape(b, lp, hq * hw), kv.reshape(b, lp, hq * (NOPE_D + V_D)), kpe.reshape(b, lp, LANES))

    h2 = _proj_residual(o.reshape(m, hq * V_D), p["mla_w_o"][j].astype(BF16), h3.reshape(m, d),
                        p["norm_mix_post"])
    return h2.reshape(b, lp, d)


def _pool_kernel(x_ref, w_ref, sc_ref, o_ref):
    gi = pl.program_id(1)
    x = x_ref[0]
    t = lax.broadcasted_iota(jnp.int32, x.shape, 0)
    for idx, win in enumerate(POOL_WINDOWS):
        @pl.when(gi == idx)
        def _(win=win):
            s = x
            step = 1
            while step < win:
                s = s + jnp.where(t >= step, pltpu.roll(s, step, axis=0), 0.0)
                step *= 2
            cnt = jnp.minimum(t + 1, win).astype(F32)
            pooled = (s / cnt - x).astype(BF16)
            o_ref[0] = _dot(pooled, w_ref[0]) * sc_ref[...]


def _pool_layer(h3, j, p):
    b, lp, d = h3.shape
    m = b * lp
    ng = len(POOL_WINDOWS)
    gw = d // ng
    hn = _prenorm(h3.reshape(m, d), p["norm_mix_pre"], F32).reshape(b, lp, d)
    y = pl.pallas_call(
        _pool_kernel,
        out_shape=jax.ShapeDtypeStruct((b, lp, d), F32),
        grid=(b, ng),
        in_specs=[pl.BlockSpec((1, lp, gw), lambda bi, g: (bi, 0, g)),
                  pl.BlockSpec((1, gw, gw), lambda bi, g: (g, 0, 0)),
                  pl.BlockSpec((1, gw), lambda bi, g: (0, g))],
        out_specs=pl.BlockSpec((1, lp, gw), lambda bi, g: (bi, 0, g)),
        compiler_params=_cparams(("parallel", "parallel")),
        name="pool_mix",
    )(hn, p["pool_w"][j].astype(BF16), p["pool_scale"][j].reshape(1, d))
    h2 = _postnorm_residual(h3.reshape(m, d), y.reshape(m, d), p["norm_mix_post"])
    return h2.reshape(b, lp, d)


def kernel(x, meta_tokens, norm_mix_pre, norm_mix_post, norm_ffn_pre, norm_ffn_post, ffn_w_gate, ffn_w_up, ffn_w_down, rwkv_mix, rwkv_w0, rwkv_w1, rwkv_w2, rwkv_a0, rwkv_a1, rwkv_a2, rwkv_v0, rwkv_v1, rwkv_v2, rwkv_g1, rwkv_g2, rwkv_k_k, rwkv_k_a, rwkv_r_k, rwkv_w_rkv, rwkv_w_o, rwkv_lnx_g, rwkv_lnx_b, mla_w_in, mla_q_norm, mla_w_uq, mla_kv_norm, mla_w_ukv, mla_w_o, pool_w, pool_scale):
    b, seq, d = x.shape
    depth = norm_mix_pre.shape[0]
    ltot = N_META + seq
    lp = -(-ltot // SEQ_ALIGN) * SEQ_ALIGN
    meta = jnp.broadcast_to(meta_tokens.astype(x.dtype)[None], (b, N_META, d))
    h = jnp.concatenate([meta, x, jnp.zeros((b, lp - ltot, d), x.dtype)], axis=1)
    shared = dict(rwkv_mix=rwkv_mix, rwkv_w0=rwkv_w0, rwkv_w1=rwkv_w1, rwkv_w2=rwkv_w2, rwkv_a0=rwkv_a0,
                  rwkv_a1=rwkv_a1, rwkv_a2=rwkv_a2, rwkv_v0=rwkv_v0, rwkv_v1=rwkv_v1, rwkv_v2=rwkv_v2,
                  rwkv_g1=rwkv_g1, rwkv_g2=rwkv_g2, rwkv_k_k=rwkv_k_k, rwkv_k_a=rwkv_k_a, rwkv_r_k=rwkv_r_k,
                  rwkv_w_rkv=rwkv_w_rkv, rwkv_w_o=rwkv_w_o, rwkv_lnx_g=rwkv_lnx_g, rwkv_lnx_b=rwkv_lnx_b,
                  mla_w_in=mla_w_in, mla_q_norm=mla_q_norm, mla_w_uq=mla_w_uq, mla_kv_norm=mla_kv_norm,
                  mla_w_ukv=mla_w_ukv, mla_w_o=mla_w_o, pool_w=pool_w, pool_scale=pool_scale)
    v_first = None
    for i in range(depth):
        p = dict(shared, norm_mix_pre=norm_mix_pre[i], norm_mix_post=norm_mix_post[i])
        kind, j = i % 3, i // 3
        if kind == 0:
            h, v_first = _rwkv_layer(h, j, v_first, p)
        elif kind == 1:
            h = _mla_layer(h, j, p)
        else:
            h = _pool_layer(h, j, p)
        h = _ffn(h.reshape(b * lp, d), norm_ffn_pre[i], norm_ffn_post[i],
                 *_ffn_weights(ffn_w_gate[i], ffn_w_up[i], ffn_w_down[i])).reshape(b, lp, d)
    return h[:, N_META:ltot]
```

```python
import functools

import jax
import jax.numpy as jnp
from jax import lax
from jax.experimental import pallas as pl
from jax.experimental.pallas import tpu as pltpu

F32 = jnp.float32
BF16 = jnp.bfloat16

N_META = 16
RMS_EPS = 1e-6
RWKV_HEAD = 64
LNX_EPS = 1e-5 * RWKV_HEAD
MLA_HEADS = 16
Q_LORA = 512
KV_LORA = 512
NOPE_D = 128
ROPE_D = 64
V_D = 128
ROPE_THETA = 10000.0
POOL_WINDOWS = (2, 4, 8, 16)

LANES = 128
WKV_CHUNK = 64
SEQ_ALIGN = WKV_CHUNK
WKV_HEADS_PER_GROUP = 4
WKV_GROUP = WKV_HEADS_PER_GROUP * RWKV_HEAD
WKV_GROUPS_PER_STEP = 8
WKV_CHUNKS_PER_STEP = 3
ATTN_Q_TILE = 256
VMEM_LIMIT = 56 * 1024 * 1024


def _cparams(sem):
    return pltpu.CompilerParams(dimension_semantics=sem, vmem_limit_bytes=VMEM_LIMIT)


def _dot(a, b):
    return jnp.dot(a, b, preferred_element_type=F32)


def _dot_nt(a, b):
    return lax.dot_general(a, b, (((1,), (1,)), ((), ())), preferred_element_type=F32)


def _dot_tn(a, b):
    return lax.dot_general(a, b, (((0,), (0,)), ((), ())), preferred_element_type=F32)


def _rms(x, g, eps=RMS_EPS):
    ms = jnp.mean(x * x, axis=-1, keepdims=True)
    return x * lax.rsqrt(ms + eps) * g


def _sigmoid(x):
    return 1.0 / (1.0 + jnp.exp(-x))


def _row_tile(m, cap=768):
    for t in (768, 512, 384, 256, 128, 64, 32, 16):
        if t <= cap and m % t == 0:
            return t
    raise ValueError(f"unsupported row count {m}")


def _proj_residual_kernel(x_ref, w_ref, h_ref, g_ref, o_ref):
    y = _dot(x_ref[...], w_ref[...])
    o_ref[...] = h_ref[...] + _rms(y, g_ref[...])


def _proj_residual(x2, w, h2, g):
    m, k = x2.shape
    d = w.shape[1]
    bm = _row_tile(m)
    return pl.pallas_call(
        _proj_residual_kernel,
        out_shape=jax.ShapeDtypeStruct((m, d), F32),
        grid=(m // bm,),
        in_specs=[pl.BlockSpec((bm, k), lambda i: (i, 0)),
                  pl.BlockSpec((k, d), lambda i: (0, 0)),
                  pl.BlockSpec((bm, d), lambda i: (i, 0)),
                  pl.BlockSpec((1, d), lambda i: (0, 0))],
        out_specs=pl.BlockSpec((bm, d), lambda i: (i, 0)),
        compiler_params=_cparams(("parallel",)),
        name="proj_residual",
    )(x2, w, h2, g.reshape(1, d))


FFN_HIDDEN_TILE = 512
FFN_OUT_TILE = 512
FFN_ROW_SPLIT = 2


def _ffn_kernel(h_ref, gpre_ref, gpost_ref, wg_ref, wu_ref, wd_ref, o_ref, xn_ref):
    f = pl.program_id(1)
    bm, d = o_ref.shape

    @pl.when(f == 0)
    def _():
        xn_ref[...] = _rms(h_ref[...], gpre_ref[...]).astype(BF16)
        o_ref[...] = jnp.zeros_like(o_ref)

    half = bm // FFN_ROW_SPLIT
    for r0 in range(0, bm, half):
        x = xn_ref[r0:r0 + half, :]
        gate = _dot(x, wg_ref[...])
        up = _dot(x, wu_ref[...])
        mid = (gate * _sigmoid(gate) * up).astype(BF16)
        for n0 in range(0, d, FFN_OUT_TILE):
            o_ref[r0:r0 + half, n0:n0 + FFN_OUT_TILE] += _dot(mid, wd_ref[:, n0:n0 + FFN_OUT_TILE])

    @pl.when(f == pl.num_programs(1) - 1)
    def _():
        o_ref[...] = h_ref[...] + _rms(o_ref[...], gpost_ref[...])


def _ffn(h2, gpre, gpost, wg, wu, wd):
    m, d = h2.shape
    f = wg.shape[1]
    bm = _row_tile(m)
    bf = FFN_HIDDEN_TILE if f % FFN_HIDDEN_TILE == 0 else f
    assert d % FFN_OUT_TILE == 0 and bm % (16 * FFN_ROW_SPLIT) == 0
    return pl.pallas_call(
        _ffn_kernel,
        out_shape=jax.ShapeDtypeStruct((m, d), F32),
        grid=(m // bm, f // bf),
        in_specs=[pl.BlockSpec((bm, d), lambda i, j: (i, 0)),
                  pl.BlockSpec((1, d), lambda i, j: (0, 0)),
                  pl.BlockSpec((1, d), lambda i, j: (0, 0)),
                  pl.BlockSpec((d, bf), lambda i, j: (0, j)),
                  pl.BlockSpec((d, bf), lambda i, j: (0, j)),
                  pl.BlockSpec((bf, d), lambda i, j: (j, 0))],
        out_specs=pl.BlockSpec((bm, d), lambda i, j: (i, 0)),
        scratch_shapes=[pltpu.VMEM((bm, d), BF16)],
        compiler_params=_cparams(("parallel", "arbitrary")),
        name="ffn",
    )(h2, gpre.reshape(1, d), gpost.reshape(1, d), wg, wu, wd)


def _shift_kernel(h_ref, halo_ref, g_ref, hn_ref, xx_ref):
    i = pl.program_id(1)
    g = g_ref[...]
    hn = _rms(h_ref[0], g)
    prev = _rms(halo_ref[0], g)[7:8]
    prev = jnp.where(i == 0, 0.0, prev)
    rolled = pltpu.roll(hn, 1, axis=0)
    row = lax.broadcasted_iota(jnp.int32, hn.shape, 0)
    shifted = jnp.where(row == 0, prev, rolled)
    hn_ref[0] = hn.astype(BF16)
    xx_ref[0] = (shifted - hn).astype(BF16)


def _norm_shift(h3, g):
    b, lp, d = h3.shape
    tl = lp // 4 if lp % 64 == 0 else lp
    halo_blocks = tl // 8
    out = jax.ShapeDtypeStruct((b, lp, d), BF16)
    return pl.pallas_call(
        _shift_kernel,
        out_shape=(out, out),
        grid=(b, lp // tl),
        in_specs=[pl.BlockSpec((1, tl, d), lambda bi, i: (bi, i, 0)),
                  pl.BlockSpec((1, 8, d), lambda bi, i: (bi, jnp.maximum(i * halo_blocks - 1, 0), 0)),
                  pl.BlockSpec((1, d), lambda bi, i: (0, 0))],
        out_specs=(pl.BlockSpec((1, tl, d), lambda bi, i: (bi, i, 0)),
                   pl.BlockSpec((1, tl, d), lambda bi, i: (bi, i, 0))),
        compiler_params=_cparams(("parallel", "parallel")),
        name="norm_shift",
    )(h3, h3, g.reshape(1, d))


def _rkv_kernel(hn_ref, xx_ref, mix_ref, w_ref, o_ref):
    xs = (hn_ref[...].astype(F32) + xx_ref[...].astype(F32) * mix_ref[0]).astype(BF16)
    o_ref[...] = _dot(xs, w_ref[0]).astype(BF16)


def _rkv_proj(hn2, xx2, mix3, w_rkv):
    m, d = hn2.shape
    bm = _row_tile(m)
    return pl.pallas_call(
        _rkv_kernel,
        out_shape=jax.ShapeDtypeStruct((m, 3 * d), BF16),
        grid=(m // bm, 3),
        in_specs=[pl.BlockSpec((bm, d), lambda i, s: (i, 0)),
                  pl.BlockSpec((bm, d), lambda i, s: (i, 0)),
                  pl.BlockSpec((1, 1, d), lambda i, s: (s, 0, 0)),
                  pl.BlockSpec((1, d, d), lambda i, s: (s, 0, 0))],
        out_specs=pl.BlockSpec((bm, d), lambda i, s: (i, s)),
        compiler_params=_cparams(("parallel", "parallel")),
        name="rkv_proj",
    )(hn2, xx2, mix3, w_rkv)


def _lora_kernel(*refs, has_vres):
    if has_vres:
        (hn_ref, xx_ref, mix_ref, w0_ref, w1_ref, w2_ref, a0_ref, a1_ref, a2_ref, g1_ref, g2_ref,
         v0_ref, v1_ref, v2_ref, ld_ref, a_ref, g_ref, vg_ref) = refs
    else:
        (hn_ref, xx_ref, mix_ref, w0_ref, w1_ref, w2_ref, a0_ref, a1_ref, a2_ref, g1_ref, g2_ref,
         ld_ref, a_ref, g_ref) = refs
    hn = hn_ref[...].astype(F32)
    xx = xx_ref[...].astype(F32)

    def mixed(s):
        return (hn + xx * mix_ref[s:s + 1, :]).astype(BF16)

    z = w0_ref[...] + _dot(jnp.tanh(_dot(mixed(0), w1_ref[...])).astype(BF16), w2_ref[...])
    ld_ref[...] = -jnp.exp(F32(-0.5)) * _sigmoid(z)
    za = a0_ref[...] + _dot(_dot(mixed(1), a1_ref[...]).astype(BF16), a2_ref[...])
    a_ref[...] = _sigmoid(za).astype(BF16)
    gl = _sigmoid(_dot(mixed(2), g1_ref[...])).astype(BF16)
    g_ref[...] = _dot(gl, g2_ref[...]).astype(BF16)
    if has_vres:
        zv = v0_ref[...] + _dot(_dot(mixed(3), v1_ref[...]).astype(BF16), v2_ref[...])
        vg_ref[...] = _sigmoid(zv).astype(BF16)


def _pad_lora(w_down, w_up):
    r = w_down.shape[1]
    rp = -(-r // LANES) * LANES
    return (jnp.pad(w_down, ((0, 0), (0, rp - r))).astype(BF16),
            jnp.pad(w_up, ((0, rp - r), (0, 0))).astype(BF16))


def _lora(hn2, xx2, mix, w0, w1, w2, a0, a1, a2, g1, g2, vres):
    m, d = hn2.shape
    bm = _row_tile(m, 384)
    has_vres = vres is not None
    w1p, w2p = _pad_lora(w1, w2)
    a1p, a2p = _pad_lora(a1, a2)
    g1p, g2p = _pad_lora(g1, g2)
    args = [hn2, xx2, mix, w0.reshape(1, d), w1p, w2p, a0.reshape(1, d), a1p, a2p, g1p, g2p]
    if has_vres:
        v0, v1, v2 = vres
        v1p, v2p = _pad_lora(v1, v2)
        args += [v0.reshape(1, d), v1p, v2p]
    row = pl.BlockSpec((bm, d), lambda i: (i, 0))

    def full(a):
        return pl.BlockSpec(a.shape, lambda i: (0, 0))

    in_specs = [row, row] + [full(a) for a in args[2:]]
    outs = [jax.ShapeDtypeStruct((m, d), F32), jax.ShapeDtypeStruct((m, d), BF16),
            jax.ShapeDtypeStruct((m, d), BF16)]
    if has_vres:
        outs.append(jax.ShapeDtypeStruct((m, d), BF16))
    return pl.pallas_call(
        functools.partial(_lora_kernel, has_vres=has_vres),
        out_shape=tuple(outs),
        grid=(m // bm,),
        in_specs=in_specs,
        out_specs=tuple(row for _ in outs),
        compiler_params=_cparams(("parallel",)),
        name="rwkv_lora",
    )(*args)


def _split_dot(a_bf16, x):
    hi = x.astype(BF16)
    lo = (x - hi.astype(F32)).astype(BF16)
    return _dot(a_bf16, hi) + _dot(a_bf16, lo)


def _split_dot_rhs(x, w_bf16):
    hi = x.astype(BF16)
    lo = (x - hi.astype(F32)).astype(BF16)
    return _dot(hi, w_bf16) + _dot(lo, w_bf16)


def _wkv_kernel(*refs, n_chunks, n_groups, has_vres):
    if has_vres:
        (r_ref, k_ref, v_ref, ld_ref, a_ref, g_ref, vf_ref, vg_ref,
         kk_ref, ka_ref, rk_ref, lg_ref, lb_ref, mbd_ref, msk_ref, tri_ref, o_ref, s_ref) = refs
    else:
        (r_ref, k_ref, v_ref, ld_ref, a_ref, g_ref,
         kk_ref, ka_ref, rk_ref, lg_ref, lb_ref, mbd_ref, msk_ref, tri_ref, o_ref, s_ref) = refs
    C, G, HG, N = WKV_CHUNK, WKV_GROUP, WKV_HEADS_PER_GROUP, RWKV_HEAD

    @pl.when(pl.program_id(2) == 0)
    def _():
        s_ref[...] = jnp.zeros_like(s_ref)

    def bd(x):
        xb = x.astype(BF16)
        return jnp.concatenate([xb] * HG, axis=0) * mbd_ref[...]

    gs = range(n_groups)
    lanes = [slice(gi * G, (gi + 1) * G) for gi in gs]

    def group_rows(x, n):
        return [x[i * n:(i + 1) * n] for i in gs]

    def chunk(j, carry):
        rows = pl.ds(pl.multiple_of(j * C, C), C)
        strict, incl, eye = msk_ref[0], msk_ref[1], msk_ref[2]
        r = [r_ref[0, rows, lanes[i]].astype(F32) for i in gs]
        k = [k_ref[0, rows, lanes[i]].astype(F32) for i in gs]
        v = [v_ref[0, rows, lanes[i]].astype(F32) for i in gs]
        a_s = [a_ref[0, rows, lanes[i]].astype(F32) for i in gs]
        if has_vres:
            v = [v[i] + (vf_ref[0, rows, lanes[i]].astype(F32) - v[i]) * vg_ref[0, rows, lanes[i]].astype(F32)
                 for i in gs]
        kk = [k[i] * kk_ref[:, lanes[i]] for i in gs]
        kp = [k[i] * (1.0 + (a_s[i] - 1.0) * ka_ref[:, lanes[i]]) for i in gs]
        lw = [_split_dot(tri_ref[...], ld_ref[0, rows, lanes[i]]) for i in gs]
        sums = group_rows(_dot(jnp.concatenate(
            [jnp.concatenate([kk[i] * kk[i], r[i] * kp[i] * rk_ref[:, lanes[i]]], axis=0).astype(BF16) for i in gs],
            axis=0), mbd_ref[...]), 2 * C)
        kk = [kk[i] / jnp.maximum(jnp.sqrt(sums[i][:C]), 1e-12) for i in gs]
        bb = [kk[i] * a_s[i] for i in gs]
        lw_last = [lw[i][C - 1:C, :] for i in gs]
        w_inv = [jnp.exp(-lw[i]) for i in gs]
        w_tail = [jnp.exp(lw_last[i] - lw[i]) for i in gs]
        a_t = [(-kk[i] * jnp.exp(lw[i] - ld_ref[0, rows, lanes[i]])).astype(BF16) for i in gs]
        r_t = [(r[i] * jnp.exp(lw[i])).astype(BF16) for i in gs]

        sc = [_dot_nt(jnp.concatenate([a_t[i], r_t[i]], axis=0),
                      jnp.concatenate([bd(bb[i] * w_inv[i]), bd(kp[i] * w_inv[i])], axis=0)) for i in gs]
        l_ab = [sc[i][:C, :G] * strict for i in gs]
        l_ak = [(sc[i][:C, G:] * strict).astype(BF16) for i in gs]
        l_rbk = [jnp.concatenate([sc[i][C:, :G] * incl, sc[i][C:, G:] * incl], axis=1).astype(BF16) for i in gs]
        v_bd = [bd(v[i]) for i in gs]
        lv = [_dot(l_ak[i], v_bd[i]) for i in gs]

        p = [l_ab[i].astype(BF16) for i in gs]
        t = [eye + l_ab[i] for i in gs]
        power = 1
        while power < C:
            p_bd = [bd(p[i]) for i in gs]
            if power == 1:
                p = [_dot(p[i], p_bd[i]).astype(BF16) for i in gs]
            elif power * 2 < C:
                res = [_dot(jnp.concatenate([p[i], t[i].astype(BF16)], axis=0), p_bd[i]) for i in gs]
                p = [res[i][:C].astype(BF16) for i in gs]
                t = [t[i] + res[i][C:] for i in gs]
            else:
                t = [t[i] + _dot(t[i].astype(BF16), p_bd[i]) for i in gs]
            power *= 2

        ta = [_dot(t[i].astype(BF16), jnp.concatenate([bd(a_t[i]), bd(lv[i])], axis=1)) for i in gs]
        s0 = [s_ref[i] for i in gs]
        res = [_dot_nt(jnp.concatenate([ta[i][:, :G].astype(BF16), r_t[i]], axis=0), s0[i].astype(BF16)) for i in gs]
        uv_stack = [jnp.concatenate([bd(res[i][:C] + ta[i][:, G:]), v_bd[i]], axis=0) for i in gs]
        y = [res[i][C:] + _dot(l_rbk[i], uv_stack[i]) for i in gs]
        for i in gs:
            s_ref[i] = s0[i] * jnp.exp(lw_last[i]) + _dot_tn(
                uv_stack[i], jnp.concatenate([bd(bb[i] * w_tail[i]), bd(kp[i] * w_tail[i])], axis=0))

        y_hi = [y[i].astype(BF16) for i in gs]
        y_lo = [(y[i] - y_hi[i].astype(F32)).astype(BF16) for i in gs]
        mu2 = group_rows(_dot(jnp.concatenate([jnp.concatenate([y_hi[i], y_lo[i]], axis=0) for i in gs], axis=0),
                              mbd_ref[...]), 2 * C)
        dy = [y[i] - (mu2[i][:C] + mu2[i][C:]) * (1.0 / N) for i in gs]
        var = group_rows(_dot(jnp.concatenate([(dy[i] * dy[i]).astype(BF16) for i in gs], axis=0), mbd_ref[...]), C)
        var = [var[i] * (1.0 / N) for i in gs]
        for i in gs:
            yn = dy[i] * lax.rsqrt(var[i] + LNX_EPS) * lg_ref[:, lanes[i]] + lb_ref[:, lanes[i]]
            o_ref[0, rows, lanes[i]] = ((yn + sums[i][C:] * v[i]) * g_ref[0, rows, lanes[i]].astype(F32)).astype(BF16)
        return carry

    lax.fori_loop(0, n_chunks, chunk, 0)


def _wkv_tables():
    C, G, N = WKV_CHUNK, WKV_GROUP, RWKV_HEAD
    idx = jnp.arange(G)
    mbd = (idx[:, None] // N == idx[None, :] // N).astype(BF16)
    t = jnp.arange(C)[:, None]
    s = (idx % N)[None, :]
    masks = jnp.stack([s < t, s <= t, s == t]).astype(F32)
    tri = (jnp.arange(C)[None, :] <= t).astype(BF16)
    return mbd, masks, tri


def _wkv(rkv, ld, a, g, rkv_first, vg, k_k, k_a, r_k, lnx_g, lnx_b):
    b, lp, d = ld.shape
    n_groups = next(n for n in (WKV_GROUPS_PER_STEP, 4, 2, 1) if d % (n * WKV_GROUP) == 0)
    width = n_groups * WKV_GROUP
    n_lane_blocks = d // width
    n_chunks = next(n for n in (WKV_CHUNKS_PER_STEP, 2, 1) if (lp // WKV_CHUNK) % n == 0)
    tl = n_chunks * WKV_CHUNK
    has_vres = rkv_first is not None

    def seq(col):
        return pl.BlockSpec((1, tl, width), lambda bi, hg, c: (bi, c, hg + col * n_lane_blocks))

    par = pl.BlockSpec((1, width), lambda bi, hg, c: (0, hg))
    args = [rkv, rkv, rkv, ld, a, g] + ([rkv_first, vg] if has_vres else [])
    specs = [seq(0), seq(1), seq(2), seq(0), seq(0), seq(0)] + ([seq(2), seq(0)] if has_vres else [])
    params = [p.reshape(1, d) for p in (k_k, k_a, r_k, lnx_g, lnx_b)]
    tables = _wkv_tables()
    table_specs = [pl.BlockSpec(t.shape, lambda bi, hg, c, nd=t.ndim: (0,) * nd) for t in tables]
    return pl.pallas_call(
        functools.partial(_wkv_kernel, n_chunks=n_chunks, n_groups=n_groups, has_vres=has_vres),
        out_shape=jax.ShapeDtypeStruct((b, lp, d), BF16),
        grid=(b, n_lane_blocks, lp // tl),
        in_specs=specs + [par] * len(params) + table_specs,
        out_specs=seq(0),
        scratch_shapes=[pltpu.VMEM((n_groups, WKV_GROUP, WKV_GROUP), F32)],
        compiler_params=_cparams(("parallel", "parallel", "arbitrary")),
        name="wkv7",
    )(*args, *params, *tables)


def _rwkv_layer(h3, j, rkv_first, p):
    b, lp, d = h3.shape
    m = b * lp
    hn, xx = _norm_shift(h3, p["norm_mix_pre"])
    hn2, xx2 = hn.reshape(m, d), xx.reshape(m, d)
    mix = p["rwkv_mix"][j]
    rkv = _rkv_proj(hn2, xx2, jnp.stack([mix[0], mix[2], mix[3]]).reshape(3, 1, d),
                    p["rwkv_w_rkv"][j].astype(BF16)).reshape(b, lp, 3 * d)
    vres = None if j == 0 else (p["rwkv_v0"][j - 1], p["rwkv_v1"][j - 1], p["rwkv_v2"][j - 1])
    lora = _lora(hn2, xx2, jnp.stack([mix[1], mix[4], mix[5], mix[3]]), p["rwkv_w0"][j], p["rwkv_w1"][j],
                 p["rwkv_w2"][j], p["rwkv_a0"][j], p["rwkv_a1"][j], p["rwkv_a2"][j], p["rwkv_g1"][j],
                 p["rwkv_g2"][j], vres)
    ld, a, g = (t.reshape(b, lp, d) for t in lora[:3])
    vg = lora[3].reshape(b, lp, d) if vres is not None else None
    if vres is None:
        rkv_first = rkv
    yg = _wkv(rkv, ld, a, g, rkv_first if vres is not None else None, vg,
              p["rwkv_k_k"][j], p["rwkv_k_a"][j], p["rwkv_r_k"][j].reshape(d),
              p["rwkv_lnx_g"][j], p["rwkv_lnx_b"][j])
    h2 = _proj_residual(yg.reshape(m, d), p["rwkv_w_o"][j].astype(BF16), h3.reshape(m, d), p["norm_mix_post"])
    return h2.reshape(b, lp, d), rkv_first


def _mla_proj_kernel(h_ref, g_ref, win_ref, qn_ref, wuq_ref, kvn_ref, wukv_ref, cos_ref, sin_ref,
                     q_ref, kv_ref, kpe_ref, *, scale):
    hn = _rms(h_ref[...], g_ref[...]).astype(BF16)
    lat = _dot(hn, win_ref[...])
    cq = _rms(lat[:, :Q_LORA], qn_ref[...]).astype(BF16)
    ckv = _rms(lat[:, Q_LORA:Q_LORA + KV_LORA], kvn_ref[...]).astype(BF16)
    cos, sin = cos_ref[...], sin_ref[...]

    def rope(x):
        return x * cos + pltpu.roll(x, ROPE_D // 2, axis=1) * sin

    keep = (lax.broadcasted_iota(jnp.int32, (1, LANES), 1) < ROPE_D).astype(F32)
    kpe_ref[...] = (rope(lat[:, Q_LORA + KV_LORA:]) * keep).astype(BF16)
    kv_ref[...] = _dot(ckv, wukv_ref[...]).astype(BF16)
    q = _dot(cq, wuq_ref[...])
    hw = NOPE_D + LANES
    for h in range(MLA_HEADS):
        q_ref[:, h * hw:h * hw + NOPE_D] = (q[:, h * hw:h * hw + NOPE_D] * scale).astype(BF16)
        q_ref[:, h * hw + NOPE_D:(h + 1) * hw] = (rope(q[:, h * hw + NOPE_D:(h + 1) * hw]) * scale).astype(BF16)


def _mla_attn_kernel(q_ref, kv_ref, kpe_ref, o_ref, kcat_ref, vone_ref):
    lp = q_ref.shape[1]
    kcat_ref[:, :NOPE_D] = kv_ref[0, :, :NOPE_D]
    kcat_ref[:, NOPE_D:] = kpe_ref[0]
    vone_ref[:, :V_D] = kv_ref[0, :, NOPE_D:]
    vone_ref[:, V_D:] = jnp.ones((lp, LANES), BF16)
    for q0 in range(0, lp, ATTN_Q_TILE):
        q1 = min(q0 + ATTN_Q_TILE, lp)
        q = q_ref[0, q0:q1, :]
        sd = _dot_nt(q, kcat_ref[q0:q1, :])
        row = lax.broadcasted_iota(jnp.int32, sd.shape, 0)
        col = lax.broadcasted_iota(jnp.int32, sd.shape, 1)
        sd = jnp.where(col <= row, sd, -1e30)
        mx = jnp.max(sd, axis=-1, keepdims=True)
        if q0 > 0:
            sp = _dot_nt(q, kcat_ref[:q0, :])
            mx = jnp.maximum(mx, jnp.max(sp, axis=-1, keepdims=True))
        acc = _dot(jnp.exp(sd - mx).astype(BF16), vone_ref[q0:q1, :])
        if q0 > 0:
            acc = acc + _dot(jnp.exp(sp - mx).astype(BF16), vone_ref[:q0, :])
        o_ref[0, q0:q1, :] = (acc[:, :V_D] / acc[:, V_D:V_D + 1]).astype(BF16)


def _rope_tables(b, lp):
    half = ROPE_D // 2
    inv_freq = 1.0 / (ROPE_THETA ** (jnp.arange(0, ROPE_D, 2, dtype=F32) / ROPE_D))
    ang = jnp.arange(lp, dtype=jnp.int32).astype(F32)[:, None] * inv_freq[None, :]
    cos, sin = jnp.cos(ang), jnp.sin(ang)
    cos4 = jnp.tile(cos, (b, LANES // half))
    sin4 = jnp.tile(jnp.concatenate([-sin, sin], axis=1), (b, LANES // ROPE_D))
    return cos4, sin4


def _mla_layer(h3, j, p):
    b, lp, d = h3.shape
    m = b * lp
    hq = MLA_HEADS
    hw = NOPE_D + LANES
    bm = _row_tile(m, 384)
    w_in = p["mla_w_in"][j]
    w_in = jnp.concatenate([w_in, w_in[:, Q_LORA + KV_LORA:]], axis=1).astype(BF16)
    w_uq = p["mla_w_uq"][j].reshape(Q_LORA, hq, NOPE_D + ROPE_D)
    w_uq = jnp.concatenate([w_uq, w_uq[:, :, NOPE_D:]], axis=2).reshape(Q_LORA, hq * hw).astype(BF16)
    w_ukv = p["mla_w_ukv"][j].astype(BF16)
    cos4, sin4 = _rope_tables(b, lp)
    scale = float((NOPE_D + ROPE_D) ** -0.5)

    def full(a):
        return pl.BlockSpec(a.shape, lambda i: (0, 0))

    consts = [p["norm_mix_pre"].reshape(1, d), w_in, p["mla_q_norm"][j].reshape(1, Q_LORA), w_uq,
              p["mla_kv_norm"][j].reshape(1, KV_LORA), w_ukv]
    q, kv, kpe = pl.pallas_call(
        functools.partial(_mla_proj_kernel, scale=scale),
        out_shape=(jax.ShapeDtypeStruct((m, hq * hw), BF16),
                   jax.ShapeDtypeStruct((m, hq * (NOPE_D + V_D)), BF16),
                   jax.ShapeDtypeStruct((m, LANES), BF16)),
        grid=(m // bm,),
        in_specs=[pl.BlockSpec((bm, d), lambda i: (i, 0))] + [full(a) for a in consts]
                 + [pl.BlockSpec((bm, LANES), lambda i: (i, 0))] * 2,
        out_specs=(pl.BlockSpec((bm, hq * hw), lambda i: (i, 0)),
                   pl.BlockSpec((bm, hq * (NOPE_D + V_D)), lambda i: (i, 0)),
                   pl.BlockSpec((bm, LANES), lambda i: (i, 0))),
        compiler_params=_cparams(("parallel",)),
        name="mla_proj",
    )(h3.reshape(m, d), *consts, cos4, sin4)

    o = pl.pallas_call(
        _mla_attn_kernel,
        out_shape=jax.ShapeDtypeStruct((b, lp, hq * V_D), BF16),
        grid=(b, hq),
        in_specs=[pl.BlockSpec((1, lp, hw), lambda bi, h: (bi, 0, h)),
                  pl.BlockSpec((1, lp, NOPE_D + V_D), lambda bi, h: (bi, 0, h)),
                  pl.BlockSpec((1, lp, LANES), lambda bi, h: (bi, 0, 0))],
        out_specs=pl.BlockSpec((1, lp, V_D), lambda bi, h: (bi, 0, h)),
        scratch_shapes=[pltpu.VMEM((lp, hw), BF16), pltpu.VMEM((lp, V_D + LANES), BF16)],
        compiler_params=_cparams(("parallel", "parallel")),
        name="mla_attention",
    )(q.reshape(b, lp, hq * hw), kv.reshape(b, lp, hq * (NOPE_D + V_D)), kpe.reshape(b, lp, LANES))

    h2 = _proj_residual(o.reshape(m, hq * V_D), p["mla_w_o"][j].astype(BF16), h3.reshape(m, d),
                        p["norm_mix_post"])
    return h2.reshape(b, lp, d)


POOL_HALO = 16


def _pool_kernel(h_ref, halo_ref, gpre_ref, w_ref, sc_ref, gpost_ref, o_ref):
    i = pl.program_id(1)
    tl, d = h_ref.shape[1], h_ref.shape[2]
    gw = d // len(POOL_WINDOWS)
    h = h_ref[0]
    hn = _rms(h, gpre_ref[...])
    hist = jnp.where(i == 0, 0.0, _rms(halo_ref[0], gpre_ref[...]))
    ext = jnp.concatenate([hist, hn], axis=0)
    t = i * tl + lax.broadcasted_iota(jnp.int32, (tl, gw), 0)
    ys = []
    for gi, win in enumerate(POOL_WINDOWS):
        cols = slice(gi * gw, (gi + 1) * gw)
        s = ext[:, cols]
        step = 1
        while step < win:
            s = s + pltpu.roll(s, step, axis=0)
            step *= 2
        cnt = jnp.minimum(t + 1, win).astype(F32)
        pooled = (s[POOL_HALO:] / cnt - hn[:, cols]).astype(BF16)
        ys.append(_dot(pooled, w_ref[gi]) * sc_ref[:, cols])
    o_ref[0] = h + _rms(jnp.concatenate(ys, axis=1), gpost_ref[...])


def _pool_layer(h3, j, p):
    b, lp, d = h3.shape
    assert max(POOL_WINDOWS) <= POOL_HALO
    tl = lp // 4 if lp % (4 * POOL_HALO) == 0 else lp
    halo_blocks = tl // POOL_HALO

    def full(a):
        return pl.BlockSpec(a.shape, lambda bi, i, nd=a.ndim: (0,) * nd)

    consts = [p["norm_mix_pre"].reshape(1, d), p["pool_w"][j].astype(BF16), p["pool_scale"][j].reshape(1, d),
              p["norm_mix_post"].reshape(1, d)]
    return pl.pallas_call(
        _pool_kernel,
        out_shape=jax.ShapeDtypeStruct((b, lp, d), F32),
        grid=(b, lp // tl),
        in_specs=[pl.BlockSpec((1, tl, d), lambda bi, i: (bi, i, 0)),
                  pl.BlockSpec((1, POOL_HALO, d), lambda bi, i: (bi, jnp.maximum(i * halo_blocks - 1, 0), 0))]
                 + [full(a) for a in consts],
        out_specs=pl.BlockSpec((1, tl, d), lambda bi, i: (bi, i, 0)),
        compiler_params=_cparams(("parallel", "parallel")),
        name="pool_layer",
    )(h3, h3, *consts)


def kernel(x, meta_tokens, norm_mix_pre, norm_mix_post, norm_ffn_pre, norm_ffn_post, ffn_w_gate, ffn_w_up, ffn_w_down, rwkv_mix, rwkv_w0, rwkv_w1, rwkv_w2, rwkv_a0, rwkv_a1, rwkv_a2, rwkv_v0, rwkv_v1, rwkv_v2, rwkv_g1, rwkv_g2, rwkv_k_k, rwkv_k_a, rwkv_r_k, rwkv_w_rkv, rwkv_w_o, rwkv_lnx_g, rwkv_lnx_b, mla_w_in, mla_q_norm, mla_w_uq, mla_kv_norm, mla_w_ukv, mla_w_o, pool_w, pool_scale):
    b, seq, d = x.shape
    depth = norm_mix_pre.shape[0]
    ltot = N_META + seq
    lp = -(-ltot // SEQ_ALIGN) * SEQ_ALIGN
    meta = jnp.broadcast_to(meta_tokens.astype(x.dtype)[None], (b, N_META, d))
    h = jnp.concatenate([meta, x, jnp.zeros((b, lp - ltot, d), x.dtype)], axis=1)
    shared = dict(rwkv_mix=rwkv_mix, rwkv_w0=rwkv_w0, rwkv_w1=rwkv_w1, rwkv_w2=rwkv_w2, rwkv_a0=rwkv_a0,
                  rwkv_a1=rwkv_a1, rwkv_a2=rwkv_a2, rwkv_v0=rwkv_v0, rwkv_v1=rwkv_v1, rwkv_v2=rwkv_v2,
                  rwkv_g1=rwkv_g1, rwkv_g2=rwkv_g2, rwkv_k_k=rwkv_k_k, rwkv_k_a=rwkv_k_a, rwkv_r_k=rwkv_r_k,
                  rwkv_w_rkv=rwkv_w_rkv, rwkv_w_o=rwkv_w_o, rwkv_lnx_g=rwkv_lnx_g, rwkv_lnx_b=rwkv_lnx_b,
                  mla_w_in=mla_w_in, mla_q_norm=mla_q_norm, mla_w_uq=mla_w_uq, mla_kv_norm=mla_kv_norm,
                  mla_w_ukv=mla_w_ukv, mla_w_o=mla_w_o, pool_w=pool_w, pool_scale=pool_scale)
    v_first = None
    for i in range(depth):
        p = dict(shared, norm_mix_pre=norm_mix_pre[i], norm_mix_post=norm_mix_post[i])
        kind, j = i % 3, i // 3
        if kind == 0:
            h, v_first = _rwkv_layer(h, j, v_first, p)
        elif kind == 1:
            h = _mla_layer(h, j, p)
        else:
            h = _pool_layer(h, j, p)
        h = _ffn(h.reshape(b * lp, d), norm_ffn_pre[i], norm_ffn_post[i], ffn_w_gate[i].astype(BF16),
                 ffn_w_up[i].astype(BF16), ffn_w_down[i].astype(BF16)).reshape(b, lp, d)
    return h[:, N_META:ltot]
```

```python
import functools

import jax
import jax.numpy as jnp
from jax import lax
from jax.experimental import pallas as pl
from jax.experimental.pallas import tpu as pltpu

F32 = jnp.float32
BF16 = jnp.bfloat16

N_META = 16
RMS_EPS = 1e-6
RWKV_HEAD = 64
LNX_EPS = 1e-5 * RWKV_HEAD
MLA_HEADS = 16
Q_LORA = 512
KV_LORA = 512
NOPE_D = 128
ROPE_D = 64
V_D = 128
ROPE_THETA = 10000.0
POOL_WINDOWS = (2, 4, 8, 16)

LANES = 128
WKV_CHUNK = 64
SEQ_ALIGN = WKV_CHUNK
WKV_HEADS_PER_GROUP = 4
WKV_GROUP = WKV_HEADS_PER_GROUP * RWKV_HEAD
WKV_GROUPS_PER_STEP = 8
WKV_CHUNKS_PER_STEP = 3
ATTN_Q_TILE = 256
VMEM_LIMIT = 56 * 1024 * 1024


def _cparams(sem):
    return pltpu.CompilerParams(dimension_semantics=sem, vmem_limit_bytes=VMEM_LIMIT)


def _dot(a, b):
    return jnp.dot(a, b, preferred_element_type=F32)


def _dot_nt(a, b):
    return lax.dot_general(a, b, (((1,), (1,)), ((), ())), preferred_element_type=F32)


def _dot_tn(a, b):
    return lax.dot_general(a, b, (((0,), (0,)), ((), ())), preferred_element_type=F32)


def _rms(x, g, eps=RMS_EPS):
    ms = jnp.mean(x * x, axis=-1, keepdims=True)
    return x * lax.rsqrt(ms + eps) * g


def _sigmoid(x):
    return 1.0 / (1.0 + jnp.exp(-x))


def _row_tile(m, cap=768):
    for t in (768, 512, 384, 256, 128, 64, 32, 16):
        if t <= cap and m % t == 0:
            return t
    raise ValueError(f"unsupported row count {m}")


def _proj_residual_kernel(x_ref, w_ref, h_ref, g_ref, o_ref):
    y = _dot(x_ref[...], w_ref[...])
    o_ref[...] = h_ref[...] + _rms(y, g_ref[...])


def _proj_residual(x2, w_all, layer, h2, g):
    m, k = x2.shape
    d = w_all.shape[2]
    bm = _row_tile(m)
    return pl.pallas_call(
        _proj_residual_kernel,
        out_shape=jax.ShapeDtypeStruct((m, d), F32),
        grid=(m // bm,),
        in_specs=[pl.BlockSpec((bm, k), lambda i: (i, 0)),
                  pl.BlockSpec((None, k, d), lambda i: (layer, 0, 0)),
                  pl.BlockSpec((bm, d), lambda i: (i, 0)),
                  pl.BlockSpec((1, d), lambda i: (0, 0))],
        out_specs=pl.BlockSpec((bm, d), lambda i: (i, 0)),
        compiler_params=_cparams(("parallel",)),
        name="proj_residual",
    )(x2, w_all, h2, g.reshape(1, d))


FFN_HIDDEN_TILE = 512
FFN_OUT_TILE = 512
FFN_ROW_SPLIT = 2


def _ffn_kernel(h_ref, gpre_ref, gpost_ref, wg_ref, wu_ref, wd_ref, o_ref, xn_ref):
    f = pl.program_id(1)
    bm, d = o_ref.shape

    @pl.when(f == 0)
    def _():
        xn_ref[...] = _rms(h_ref[...], gpre_ref[...]).astype(BF16)
        o_ref[...] = jnp.zeros_like(o_ref)

    half = bm // FFN_ROW_SPLIT
    for r0 in range(0, bm, half):
        x = xn_ref[r0:r0 + half, :]
        gate = _dot(x, wg_ref[...])
        up = _dot(x, wu_ref[...])
        mid = (gate * _sigmoid(gate) * up).astype(BF16)
        for n0 in range(0, d, FFN_OUT_TILE):
            o_ref[r0:r0 + half, n0:n0 + FFN_OUT_TILE] += _dot(mid, wd_ref[:, n0:n0 + FFN_OUT_TILE])

    @pl.when(f == pl.num_programs(1) - 1)
    def _():
        o_ref[...] = h_ref[...] + _rms(o_ref[...], gpost_ref[...])


def _ffn(h2, gpre, gpost, wg, wu, wd, layer):
    m, d = h2.shape
    f = wg.shape[2]
    bm = _row_tile(m)
    bf = FFN_HIDDEN_TILE if f % FFN_HIDDEN_TILE == 0 else f
    assert d % FFN_OUT_TILE == 0 and bm % (16 * FFN_ROW_SPLIT) == 0
    return pl.pallas_call(
        _ffn_kernel,
        out_shape=jax.ShapeDtypeStruct((m, d), F32),
        grid=(m // bm, f // bf),
        in_specs=[pl.BlockSpec((bm, d), lambda i, j: (i, 0)),
                  pl.BlockSpec((1, d), lambda i, j: (0, 0)),
                  pl.BlockSpec((1, d), lambda i, j: (0, 0)),
                  pl.BlockSpec((None, d, bf), lambda i, j: (layer, 0, j)),
                  pl.BlockSpec((None, d, bf), lambda i, j: (layer, 0, j)),
                  pl.BlockSpec((None, bf, d), lambda i, j: (layer, j, 0))],
        out_specs=pl.BlockSpec((bm, d), lambda i, j: (i, 0)),
        scratch_shapes=[pltpu.VMEM((bm, d), BF16)],
        compiler_params=_cparams(("parallel", "arbitrary")),
        name="ffn",
    )(h2, gpre.reshape(1, d), gpost.reshape(1, d), wg, wu, wd)


def _shift_kernel(h_ref, halo_ref, g_ref, hn_ref, xx_ref):
    i = pl.program_id(1)
    g = g_ref[...]
    hn = _rms(h_ref[0], g)
    prev = _rms(halo_ref[0], g)[7:8]
    prev = jnp.where(i == 0, 0.0, prev)
    rolled = pltpu.roll(hn, 1, axis=0)
    row = lax.broadcasted_iota(jnp.int32, hn.shape, 0)
    shifted = jnp.where(row == 0, prev, rolled)
    hn_ref[0] = hn.astype(BF16)
    xx_ref[0] = (shifted - hn).astype(BF16)


def _norm_shift(h3, g):
    b, lp, d = h3.shape
    tl = lp // 4 if lp % 64 == 0 else lp
    halo_blocks = tl // 8
    out = jax.ShapeDtypeStruct((b, lp, d), BF16)
    return pl.pallas_call(
        _shift_kernel,
        out_shape=(out, out),
        grid=(b, lp // tl),
        in_specs=[pl.BlockSpec((1, tl, d), lambda bi, i: (bi, i, 0)),
                  pl.BlockSpec((1, 8, d), lambda bi, i: (bi, jnp.maximum(i * halo_blocks - 1, 0), 0)),
                  pl.BlockSpec((1, d), lambda bi, i: (0, 0))],
        out_specs=(pl.BlockSpec((1, tl, d), lambda bi, i: (bi, i, 0)),
                   pl.BlockSpec((1, tl, d), lambda bi, i: (bi, i, 0))),
        compiler_params=_cparams(("parallel", "parallel")),
        name="norm_shift",
    )(h3, h3, g.reshape(1, d))


def _rkv_kernel(hn_ref, xx_ref, mix_ref, w_ref, o_ref):
    xs = (hn_ref[...].astype(F32) + xx_ref[...].astype(F32) * mix_ref[0]).astype(BF16)
    o_ref[...] = _dot(xs, w_ref[...]).astype(BF16)


def _rkv_proj(hn2, xx2, mix3, w_rkv_all, layer):
    m, d = hn2.shape
    bm = _row_tile(m)
    return pl.pallas_call(
        _rkv_kernel,
        out_shape=jax.ShapeDtypeStruct((m, 3 * d), BF16),
        grid=(m // bm, 3),
        in_specs=[pl.BlockSpec((bm, d), lambda i, s: (i, 0)),
                  pl.BlockSpec((bm, d), lambda i, s: (i, 0)),
                  pl.BlockSpec((1, 1, d), lambda i, s: (s, 0, 0)),
                  pl.BlockSpec((None, None, d, d), lambda i, s: (layer, s, 0, 0))],
        out_specs=pl.BlockSpec((bm, d), lambda i, s: (i, s)),
        compiler_params=_cparams(("parallel", "parallel")),
        name="rkv_proj",
    )(hn2, xx2, mix3, w_rkv_all)


LORA_SMALL_TILE = 256


def _lora_kernel(*refs, has_vres, w_rank):
    if has_vres:
        (hn_ref, xx_ref, wf_ref, w0_ref, w2_ref, a0_ref, a2_ref, g2_ref, v0_ref, v2_ref,
         ld_ref, a_ref, g_ref, vg_ref) = refs
    else:
        hn_ref, xx_ref, wf_ref, w0_ref, w2_ref, a0_ref, a2_ref, g2_ref, ld_ref, a_ref, g_ref = refs
    t = _dot(jnp.concatenate([hn_ref[...], xx_ref[...]], axis=1), wf_ref[...])
    small = t[:, :LORA_SMALL_TILE]
    lane = lax.broadcasted_iota(jnp.int32, (1, LORA_SMALL_TILE), 1)
    act = jnp.where(lane < w_rank, jnp.tanh(small), small).astype(BF16)
    z = w0_ref[...] + _dot(act, w2_ref[...])
    ld_ref[...] = -jnp.exp(F32(-0.5)) * _sigmoid(z)
    a_ref[...] = _sigmoid(a0_ref[...] + _dot(act, a2_ref[...])).astype(BF16)
    g_ref[...] = _dot(_sigmoid(t[:, LORA_SMALL_TILE:]).astype(BF16), g2_ref[...]).astype(BF16)
    if has_vres:
        vg_ref[...] = _sigmoid(v0_ref[...] + _dot(act, v2_ref[...])).astype(BF16)


def _lora(hn2, xx2, mix, w0, w1, w2, a0, a1, a2, g1, g2, vres):
    m, d = hn2.shape
    bm = _row_tile(m, 384)
    has_vres = vres is not None
    downs = [(w1, mix[0]), (a1, mix[1])] + ([(vres[1], mix[3])] if has_vres else [])
    ups = [w2, a2] + ([vres[2]] if has_vres else [])
    ranks = [w.shape[1] for w, _ in downs]
    assert sum(ranks) <= LORA_SMALL_TILE
    pad = LORA_SMALL_TILE - sum(ranks)
    plain = jnp.concatenate([w for w, _ in downs] + [jnp.zeros((d, pad), F32), g1], axis=1)
    mixed = jnp.concatenate([w * mx[:, None] for w, mx in downs] + [jnp.zeros((d, pad), F32), g1 * mix[2][:, None]],
                            axis=1)
    w_first = jnp.concatenate([plain, mixed], axis=0).astype(BF16)

    def placed(up, offset):
        return jnp.pad(up, ((offset, LORA_SMALL_TILE - offset - up.shape[0]), (0, 0))).astype(BF16)

    offsets = [sum(ranks[:i]) for i in range(len(ranks))]
    ups = [placed(u, o) for u, o in zip(ups, offsets)]
    args = [hn2, xx2, w_first, w0.reshape(1, d), ups[0], a0.reshape(1, d), ups[1], g2.astype(BF16)]
    if has_vres:
        args += [vres[0].reshape(1, d), ups[2]]
    row = pl.BlockSpec((bm, d), lambda i: (i, 0))

    def full(a):
        return pl.BlockSpec(a.shape, lambda i: (0, 0))

    outs = [jax.ShapeDtypeStruct((m, d), F32), jax.ShapeDtypeStruct((m, d), BF16),
            jax.ShapeDtypeStruct((m, d), BF16)]
    if has_vres:
        outs.append(jax.ShapeDtypeStruct((m, d), BF16))
    return pl.pallas_call(
        functools.partial(_lora_kernel, has_vres=has_vres, w_rank=ranks[0]),
        out_shape=tuple(outs),
        grid=(m // bm,),
        in_specs=[row, row] + [full(a) for a in args[2:]],
        out_specs=tuple(row for _ in outs),
        compiler_params=_cparams(("parallel",)),
        name="rwkv_lora",
    )(*args)


def _split_dot(a_bf16, x):
    hi = x.astype(BF16)
    lo = (x - hi.astype(F32)).astype(BF16)
    return _dot(a_bf16, hi) + _dot(a_bf16, lo)


def _split_dot_rhs(x, w_bf16):
    hi = x.astype(BF16)
    lo = (x - hi.astype(F32)).astype(BF16)
    return _dot(hi, w_bf16) + _dot(lo, w_bf16)


def _wkv_kernel(*refs, n_chunks, n_groups, has_vres):
    if has_vres:
        (r_ref, k_ref, v_ref, ld_ref, a_ref, g_ref, vf_ref, vg_ref,
         kk_ref, ka_ref, rk_ref, lg_ref, lb_ref, mbd_ref, msk_ref, tri_ref, o_ref, s_ref) = refs
    else:
        (r_ref, k_ref, v_ref, ld_ref, a_ref, g_ref,
         kk_ref, ka_ref, rk_ref, lg_ref, lb_ref, mbd_ref, msk_ref, tri_ref, o_ref, s_ref) = refs
    C, G, HG, N = WKV_CHUNK, WKV_GROUP, WKV_HEADS_PER_GROUP, RWKV_HEAD

    @pl.when(pl.program_id(2) == 0)
    def _():
        s_ref[...] = jnp.zeros_like(s_ref)

    def bd(x):
        xb = x.astype(BF16)
        return jnp.concatenate([xb] * HG, axis=0) * mbd_ref[...]

    gs = range(n_groups)
    lanes = [slice(gi * G, (gi + 1) * G) for gi in gs]

    def group_rows(x, n):
        return [x[i * n:(i + 1) * n] for i in gs]

    def chunk(j, carry):
        rows = pl.ds(pl.multiple_of(j * C, C), C)
        strict, incl, eye = msk_ref[0], msk_ref[1], msk_ref[2]
        r = [r_ref[0, rows, lanes[i]].astype(F32) for i in gs]
        k = [k_ref[0, rows, lanes[i]].astype(F32) for i in gs]
        v = [v_ref[0, rows, lanes[i]].astype(F32) for i in gs]
        a_s = [a_ref[0, rows, lanes[i]].astype(F32) for i in gs]
        if has_vres:
            v = [v[i] + (vf_ref[0, rows, lanes[i]].astype(F32) - v[i]) * vg_ref[0, rows, lanes[i]].astype(F32)
                 for i in gs]
        kk = [k[i] * kk_ref[:, lanes[i]] for i in gs]
        kp = [k[i] * (1.0 + (a_s[i] - 1.0) * ka_ref[:, lanes[i]]) for i in gs]
        lw = [_split_dot(tri_ref[...], ld_ref[0, rows, lanes[i]]) for i in gs]
        sums = group_rows(_dot(jnp.concatenate(
            [jnp.concatenate([kk[i] * kk[i], r[i] * kp[i] * rk_ref[:, lanes[i]]], axis=0).astype(BF16) for i in gs],
            axis=0), mbd_ref[...]), 2 * C)
        kk = [kk[i] / jnp.maximum(jnp.sqrt(sums[i][:C]), 1e-12) for i in gs]
        bb = [kk[i] * a_s[i] for i in gs]
        lw_last = [lw[i][C - 1:C, :] for i in gs]
        w_inv = [jnp.exp(-lw[i]) for i in gs]
        w_tail = [jnp.exp(lw_last[i] - lw[i]) for i in gs]
        a_t = [(-kk[i] * jnp.exp(lw[i] - ld_ref[0, rows, lanes[i]])).astype(BF16) for i in gs]
        r_t = [(r[i] * jnp.exp(lw[i])).astype(BF16) for i in gs]

        sc = [_dot_nt(jnp.concatenate([a_t[i], r_t[i]], axis=0),
                      jnp.concatenate([bd(bb[i] * w_inv[i]), bd(kp[i] * w_inv[i])], axis=0)) for i in gs]
        l_ab = [sc[i][:C, :G] * strict for i in gs]
        l_rb = [(sc[i][C:, :G] * incl).astype(BF16) for i in gs]
        l_k = [jnp.concatenate([sc[i][:C, G:] * strict, sc[i][C:, G:] * incl], axis=0).astype(BF16) for i in gs]
        v_bd = [bd(v[i]) for i in gs]
        lkv = [_dot(l_k[i], v_bd[i]) for i in gs]

        p = [l_ab[i].astype(BF16) for i in gs]
        t = [eye + l_ab[i] for i in gs]
        power = 1
        while power < C:
            p_bd = [bd(p[i]) for i in gs]
            if power == 1:
                p = [_dot(p[i], p_bd[i]).astype(BF16) for i in gs]
            elif power * 2 < C:
                res = [_dot(jnp.concatenate([p[i], t[i].astype(BF16)], axis=0), p_bd[i]) for i in gs]
                p = [res[i][:C].astype(BF16) for i in gs]
                t = [t[i] + res[i][C:] for i in gs]
            else:
                t = [t[i] + _dot(t[i].astype(BF16), p_bd[i]) for i in gs]
            power *= 2

        ta = [_dot(t[i].astype(BF16), jnp.concatenate([bd(a_t[i]), bd(lkv[i][:C])], axis=1)) for i in gs]
        s0 = [s_ref[i] for i in gs]
        res = [_dot_nt(jnp.concatenate([ta[i][:, :G].astype(BF16), r_t[i]], axis=0), s0[i].astype(BF16)) for i in gs]
        u_bd = [bd(res[i][:C] + ta[i][:, G:]) for i in gs]
        y = [res[i][C:] + lkv[i][C:] + _dot(l_rb[i], u_bd[i]) for i in gs]
        uv_stack = [jnp.concatenate([u_bd[i], v_bd[i]], axis=0) for i in gs]
        for i in gs:
            s_ref[i] = s0[i] * jnp.exp(lw_last[i]) + _dot_tn(
                uv_stack[i], jnp.concatenate([bd(bb[i] * w_tail[i]), bd(kp[i] * w_tail[i])], axis=0))

        y_hi = [y[i].astype(BF16) for i in gs]
        y_lo = [(y[i] - y_hi[i].astype(F32)).astype(BF16) for i in gs]
        mu2 = group_rows(_dot(jnp.concatenate([jnp.concatenate([y_hi[i], y_lo[i]], axis=0) for i in gs], axis=0),
                              mbd_ref[...]), 2 * C)
        dy = [y[i] - (mu2[i][:C] + mu2[i][C:]) * (1.0 / N) for i in gs]
        var = group_rows(_dot(jnp.concatenate([(dy[i] * dy[i]).astype(BF16) for i in gs], axis=0), mbd_ref[...]), C)
        var = [var[i] * (1.0 / N) for i in gs]
        for i in gs:
            yn = dy[i] * lax.rsqrt(var[i] + LNX_EPS) * lg_ref[:, lanes[i]] + lb_ref[:, lanes[i]]
            o_ref[0, rows, lanes[i]] = ((yn + sums[i][C:] * v[i]) * g_ref[0, rows, lanes[i]].astype(F32)).astype(BF16)
        return carry

    lax.fori_loop(0, n_chunks, chunk, 0)


def _wkv_tables():
    C, G, N = WKV_CHUNK, WKV_GROUP, RWKV_HEAD
    idx = jnp.arange(G)
    mbd = (idx[:, None] // N == idx[None, :] // N).astype(BF16)
    t = jnp.arange(C)[:, None]
    s = (idx % N)[None, :]
    masks = jnp.stack([s < t, s <= t, s == t]).astype(F32)
    tri = (jnp.arange(C)[None, :] <= t).astype(BF16)
    return mbd, masks, tri


def _wkv(rkv, ld, a, g, rkv_first, vg, k_k, k_a, r_k, lnx_g, lnx_b):
    b, lp, d = ld.shape
    n_groups = next(n for n in (WKV_GROUPS_PER_STEP, 4, 2, 1) if d % (n * WKV_GROUP) == 0)
    width = n_groups * WKV_GROUP
    n_lane_blocks = d // width
    n_chunks = next(n for n in (WKV_CHUNKS_PER_STEP, 2, 1) if (lp // WKV_CHUNK) % n == 0)
    tl = n_chunks * WKV_CHUNK
    has_vres = rkv_first is not None

    def seq(col):
        return pl.BlockSpec((1, tl, width), lambda bi, hg, c: (bi, c, hg + col * n_lane_blocks))

    par = pl.BlockSpec((1, width), lambda bi, hg, c: (0, hg))
    args = [rkv, rkv, rkv, ld, a, g] + ([rkv_first, vg] if has_vres else [])
    specs = [seq(0), seq(1), seq(2), seq(0), seq(0), seq(0)] + ([seq(2), seq(0)] if has_vres else [])
    params = [p.reshape(1, d) for p in (k_k, k_a, r_k, lnx_g, lnx_b)]
    tables = _wkv_tables()
    table_specs = [pl.BlockSpec(t.shape, lambda bi, hg, c, nd=t.ndim: (0,) * nd) for t in tables]
    return pl.pallas_call(
        functools.partial(_wkv_kernel, n_chunks=n_chunks, n_groups=n_groups, has_vres=has_vres),
        out_shape=jax.ShapeDtypeStruct((b, lp, d), BF16),
        grid=(b, n_lane_blocks, lp // tl),
        in_specs=specs + [par] * len(params) + table_specs,
        out_specs=seq(0),
        scratch_shapes=[pltpu.VMEM((n_groups, WKV_GROUP, WKV_GROUP), F32)],
        compiler_params=_cparams(("parallel", "parallel", "arbitrary")),
        name="wkv7",
    )(*args, *params, *tables)


def _rwkv_layer(h3, j, rkv_first, p):
    b, lp, d = h3.shape
    m = b * lp
    hn, xx = _norm_shift(h3, p["norm_mix_pre"])
    hn2, xx2 = hn.reshape(m, d), xx.reshape(m, d)
    mix = p["rwkv_mix"][j]
    rkv = _rkv_proj(hn2, xx2, jnp.stack([mix[0], mix[2], mix[3]]).reshape(3, 1, d),
                    p["rwkv_w_rkv"], j).reshape(b, lp, 3 * d)
    vres = None if j == 0 else (p["rwkv_v0"][j - 1], p["rwkv_v1"][j - 1], p["rwkv_v2"][j - 1])
    lora = _lora(hn2, xx2, jnp.stack([mix[1], mix[4], mix[5], mix[3]]), p["rwkv_w0"][j], p["rwkv_w1"][j],
                 p["rwkv_w2"][j], p["rwkv_a0"][j], p["rwkv_a1"][j], p["rwkv_a2"][j], p["rwkv_g1"][j],
                 p["rwkv_g2"][j], vres)
    ld, a, g = (t.reshape(b, lp, d) for t in lora[:3])
    vg = lora[3].reshape(b, lp, d) if vres is not None else None
    if vres is None:
        rkv_first = rkv
    yg = _wkv(rkv, ld, a, g, rkv_first if vres is not None else None, vg,
              p["rwkv_k_k"][j], p["rwkv_k_a"][j], p["rwkv_r_k"][j].reshape(d),
              p["rwkv_lnx_g"][j], p["rwkv_lnx_b"][j])
    h2 = _proj_residual(yg.reshape(m, d), p["rwkv_w_o"], j, h3.reshape(m, d), p["norm_mix_post"])
    return h2.reshape(b, lp, d), rkv_first


def _mla_proj_kernel(h_ref, g_ref, win_ref, qn_ref, wuq_ref, kvn_ref, wukv_ref, cos_ref, sin_ref,
                     q_ref, kv_ref, kpe_ref, *, scale):
    hn = _rms(h_ref[...], g_ref[...]).astype(BF16)
    lat = _dot(hn, win_ref[...])
    cq = _rms(lat[:, :Q_LORA], qn_ref[...]).astype(BF16)
    ckv = _rms(lat[:, Q_LORA:Q_LORA + KV_LORA], kvn_ref[...]).astype(BF16)
    cos, sin = cos_ref[...], sin_ref[...]

    def rope(x):
        return x * cos + pltpu.roll(x, ROPE_D // 2, axis=1) * sin

    keep = (lax.broadcasted_iota(jnp.int32, (1, LANES), 1) < ROPE_D).astype(F32)
    kpe_ref[...] = (rope(lat[:, Q_LORA + KV_LORA:]) * keep).astype(BF16)
    kv_ref[...] = _dot(ckv, wukv_ref[...]).astype(BF16)
    q = _dot(cq, wuq_ref[...])
    hw = NOPE_D + LANES
    for h in range(MLA_HEADS):
        q_ref[:, h * hw:h * hw + NOPE_D] = (q[:, h * hw:h * hw + NOPE_D] * scale).astype(BF16)
        q_ref[:, h * hw + NOPE_D:(h + 1) * hw] = (rope(q[:, h * hw + NOPE_D:(h + 1) * hw]) * scale).astype(BF16)


def _mla_attn_kernel(q_ref, kv_ref, kpe_ref, o_ref, kcat_ref, vone_ref):
    lp = q_ref.shape[1]
    kcat_ref[:, :NOPE_D] = kv_ref[0, :, :NOPE_D]
    kcat_ref[:, NOPE_D:] = kpe_ref[0]
    vone_ref[:, :V_D] = kv_ref[0, :, NOPE_D:]
    vone_ref[:, V_D:] = jnp.ones((lp, LANES), BF16)
    tiles = [(q0, min(q0 + ATTN_Q_TILE, lp)) for q0 in range(0, lp, ATTN_Q_TILE)]

    def scores(q0, q1):
        q = q_ref[0, q0:q1, :]
        sd = _dot_nt(q, kcat_ref[q0:q1, :])
        sp = _dot_nt(q, kcat_ref[:q0, :]) if q0 > 0 else None
        return sd, sp

    def finish(q0, q1, sd, sp):
        row = lax.broadcasted_iota(jnp.int32, sd.shape, 0)
        col = lax.broadcasted_iota(jnp.int32, sd.shape, 1)
        sd = jnp.where(col <= row, sd, -1e30)
        mx = jnp.max(sd, axis=-1, keepdims=True)
        if sp is not None:
            mx = jnp.maximum(mx, jnp.max(sp, axis=-1, keepdims=True))
        acc = _dot(jnp.exp(sd - mx).astype(BF16), vone_ref[q0:q1, :])
        if sp is not None:
            acc = acc + _dot(jnp.exp(sp - mx).astype(BF16), vone_ref[:q0, :])
        o_ref[0, q0:q1, :] = (acc[:, :V_D] / acc[:, V_D:V_D + 1]).astype(BF16)

    pending = scores(*tiles[0])
    for idx, (q0, q1) in enumerate(tiles):
        upcoming = scores(*tiles[idx + 1]) if idx + 1 < len(tiles) else None
        finish(q0, q1, *pending)
        pending = upcoming


def _rope_tables(b, lp):
    half = ROPE_D // 2
    inv_freq = 1.0 / (ROPE_THETA ** (jnp.arange(0, ROPE_D, 2, dtype=F32) / ROPE_D))
    ang = jnp.arange(lp, dtype=jnp.int32).astype(F32)[:, None] * inv_freq[None, :]
    cos, sin = jnp.cos(ang), jnp.sin(ang)
    cos4 = jnp.tile(cos, (b, LANES // half))
    sin4 = jnp.tile(jnp.concatenate([-sin, sin], axis=1), (b, LANES // ROPE_D))
    return cos4, sin4


def _mla_layer(h3, j, p):
    b, lp, d = h3.shape
    m = b * lp
    hq = MLA_HEADS
    hw = NOPE_D + LANES
    bm = _row_tile(m, 384)
    w_in = p["mla_w_in"][j]
    w_in = jnp.concatenate([w_in, w_in[:, Q_LORA + KV_LORA:]], axis=1).astype(BF16)
    w_uq = p["mla_w_uq"][j].reshape(Q_LORA, hq, NOPE_D + ROPE_D)
    w_uq = jnp.concatenate([w_uq, w_uq[:, :, NOPE_D:]], axis=2).reshape(Q_LORA, hq * hw).astype(BF16)
    w_ukv = p["mla_w_ukv"][j].astype(BF16)
    cos4, sin4 = _rope_tables(b, lp)
    scale = float((NOPE_D + ROPE_D) ** -0.5)

    def full(a):
        return pl.BlockSpec(a.shape, lambda i: (0, 0))

    consts = [p["norm_mix_pre"].reshape(1, d), w_in, p["mla_q_norm"][j].reshape(1, Q_LORA), w_uq,
              p["mla_kv_norm"][j].reshape(1, KV_LORA), w_ukv]
    q, kv, kpe = pl.pallas_call(
        functools.partial(_mla_proj_kernel, scale=scale),
        out_shape=(jax.ShapeDtypeStruct((m, hq * hw), BF16),
                   jax.ShapeDtypeStruct((m, hq * (NOPE_D + V_D)), BF16),
                   jax.ShapeDtypeStruct((m, LANES), BF16)),
        grid=(m // bm,),
        in_specs=[pl.BlockSpec((bm, d), lambda i: (i, 0))] + [full(a) for a in consts]
                 + [pl.BlockSpec((bm, LANES), lambda i: (i, 0))] * 2,
        out_specs=(pl.BlockSpec((bm, hq * hw), lambda i: (i, 0)),
                   pl.BlockSpec((bm, hq * (NOPE_D + V_D)), lambda i: (i, 0)),
                   pl.BlockSpec((bm, LANES), lambda i: (i, 0))),
        compiler_params=_cparams(("parallel",)),
        name="mla_proj",
    )(h3.reshape(m, d), *consts, cos4, sin4)

    o = pl.pallas_call(
        _mla_attn_kernel,
        out_shape=jax.ShapeDtypeStruct((b, lp, hq * V_D), BF16),
        grid=(b, hq),
        in_specs=[pl.BlockSpec((1, lp, hw), lambda bi, h: (bi, 0, h)),
                  pl.BlockSpec((1, lp, NOPE_D + V_D), lambda bi, h: (bi, 0, h)),
                  pl.BlockSpec((1, lp, LANES), lambda bi, h: (bi, 0, 0))],
        out_specs=pl.BlockSpec((1, lp, V_D), lambda bi, h: (bi, 0, h)),
        scratch_shapes=[pltpu.VMEM((lp, hw), BF16), pltpu.VMEM((lp, V_D + LANES), BF16)],
        compiler_params=_cparams(("parallel", "parallel")),
        name="mla_attention",
    )(q.reshape(b, lp, hq * hw), kv.reshape(b, lp, hq * (NOPE_D + V_D)), kpe.reshape(b, lp, LANES))

    h2 = _proj_residual(o.reshape(m, hq * V_D), p["mla_w_o"], j, h3.reshape(m, d), p["norm_mix_post"])
    return h2.reshape(b, lp, d)


POOL_HALO = 16


def _pool_kernel(h_ref, halo_ref, gpre_ref, w_ref, sc_ref, gpost_ref, o_ref):
    i = pl.program_id(1)
    tl, d = h_ref.shape[1], h_ref.shape[2]
    gw = d // len(POOL_WINDOWS)
    h = h_ref[0]
    hn = _rms(h, gpre_ref[...])
    hist = jnp.where(i == 0, 0.0, _rms(halo_ref[0], gpre_ref[...]))
    ext = jnp.concatenate([hist, hn], axis=0)
    t = i * tl + lax.broadcasted_iota(jnp.int32, (tl, gw), 0)
    ys = []
    for gi, win in enumerate(POOL_WINDOWS):
        cols = slice(gi * gw, (gi + 1) * gw)
        s = ext[:, cols]
        step = 1
        while step < win:
            s = s + pltpu.roll(s, step, axis=0)
            step *= 2
        cnt = jnp.minimum(t + 1, win).astype(F32)
        pooled = (s[POOL_HALO:] / cnt - hn[:, cols]).astype(BF16)
        ys.append(_dot(pooled, w_ref[gi]) * sc_ref[:, cols])
    o_ref[0] = h + _rms(jnp.concatenate(ys, axis=1), gpost_ref[...])


def _pool_layer(h3, j, p):
    b, lp, d = h3.shape
    assert max(POOL_WINDOWS) <= POOL_HALO
    tl = lp // 4 if lp % (4 * POOL_HALO) == 0 else lp
    halo_blocks = tl // POOL_HALO

    def full(a):
        return pl.BlockSpec(a.shape, lambda bi, i, nd=a.ndim: (0,) * nd)

    consts = [p["norm_mix_pre"].reshape(1, d), p["pool_w"][j].astype(BF16), p["pool_scale"][j].reshape(1, d),
              p["norm_mix_post"].reshape(1, d)]
    return pl.pallas_call(
        _pool_kernel,
        out_shape=jax.ShapeDtypeStruct((b, lp, d), F32),
        grid=(b, lp // tl),
        in_specs=[pl.BlockSpec((1, tl, d), lambda bi, i: (bi, i, 0)),
                  pl.BlockSpec((1, POOL_HALO, d), lambda bi, i: (bi, jnp.maximum(i * halo_blocks - 1, 0), 0))]
                 + [full(a) for a in consts],
        out_specs=pl.BlockSpec((1, tl, d), lambda bi, i: (bi, i, 0)),
        compiler_params=_cparams(("parallel", "parallel")),
        name="pool_layer",
    )(h3, h3, *consts)


def kernel(x, meta_tokens, norm_mix_pre, norm_mix_post, norm_ffn_pre, norm_ffn_post, ffn_w_gate, ffn_w_up, ffn_w_down, rwkv_mix, rwkv_w0, rwkv_w1, rwkv_w2, rwkv_a0, rwkv_a1, rwkv_a2, rwkv_v0, rwkv_v1, rwkv_v2, rwkv_g1, rwkv_g2, rwkv_k_k, rwkv_k_a, rwkv_r_k, rwkv_w_rkv, rwkv_w_o, rwkv_lnx_g, rwkv_lnx_b, mla_w_in, mla_q_norm, mla_w_uq, mla_kv_norm, mla_w_ukv, mla_w_o, pool_w, pool_scale):
    b, seq, d = x.shape
    depth = norm_mix_pre.shape[0]
    ltot = N_META + seq
    lp = -(-ltot // SEQ_ALIGN) * SEQ_ALIGN
    meta = jnp.broadcast_to(meta_tokens.astype(x.dtype)[None], (b, N_META, d))
    h = jnp.concatenate([meta, x, jnp.zeros((b, lp - ltot, d), x.dtype)], axis=1)
    shared = dict(rwkv_mix=rwkv_mix, rwkv_w0=rwkv_w0, rwkv_w1=rwkv_w1, rwkv_w2=rwkv_w2, rwkv_a0=rwkv_a0,
                  rwkv_a1=rwkv_a1, rwkv_a2=rwkv_a2, rwkv_v0=rwkv_v0, rwkv_v1=rwkv_v1, rwkv_v2=rwkv_v2,
                  rwkv_g1=rwkv_g1, rwkv_g2=rwkv_g2, rwkv_k_k=rwkv_k_k, rwkv_k_a=rwkv_k_a, rwkv_r_k=rwkv_r_k,
                  rwkv_w_rkv=rwkv_w_rkv.astype(BF16), rwkv_w_o=rwkv_w_o.astype(BF16),
                  rwkv_lnx_g=rwkv_lnx_g, rwkv_lnx_b=rwkv_lnx_b,
                  mla_w_in=mla_w_in, mla_q_norm=mla_q_norm, mla_w_uq=mla_w_uq, mla_kv_norm=mla_kv_norm,
                  mla_w_ukv=mla_w_ukv, mla_w_o=mla_w_o.astype(BF16), pool_w=pool_w, pool_scale=pool_scale)
    w_gate, w_up, w_down = ffn_w_gate.astype(BF16), ffn_w_up.astype(BF16), ffn_w_down.astype(BF16)
    v_first = None
    for i in range(depth):
        p = dict(shared, norm_mix_pre=norm_mix_pre[i], norm_mix_post=norm_mix_post[i])
        kind, j = i % 3, i // 3
        if kind == 0:
            h, v_first = _rwkv_layer(h, j, v_first, p)
        elif kind == 1:
            h = _mla_layer(h, j, p)
        else:
            h = _pool_layer(h, j, p)
        h = _ffn(h.reshape(b * lp, d), norm_ffn_pre[i], norm_ffn_post[i], w_gate, w_up, w_down, i).reshape(b, lp, d)
    return h[:, N_META:ltot]
```

```python
import functools

import jax
import jax.numpy as jnp
from jax import lax
from jax.experimental import pallas as pl
from jax.experimental.pallas import tpu as pltpu

F32 = jnp.float32
BF16 = jnp.bfloat16

N_META = 16
RMS_EPS = 1e-6
RWKV_HEAD = 64
LNX_EPS = 1e-5 * RWKV_HEAD
MLA_HEADS = 16
Q_LORA = 512
KV_LORA = 512
NOPE_D = 128
ROPE_D = 64
V_D = 128
ROPE_THETA = 10000.0
POOL_WINDOWS = (2, 4, 8, 16)

LANES = 128
WKV_CHUNK = 64
SEQ_ALIGN = WKV_CHUNK
WKV_HEADS_PER_GROUP = 4
WKV_GROUP = WKV_HEADS_PER_GROUP * RWKV_HEAD
WKV_GROUPS_PER_STEP = 8
WKV_CHUNKS_PER_STEP = 3
ATTN_Q_TILE = 256
VMEM_LIMIT = 56 * 1024 * 1024


def _cparams(sem):
    return pltpu.CompilerParams(dimension_semantics=sem, vmem_limit_bytes=VMEM_LIMIT)


def _dot(a, b):
    return jnp.dot(a, b, preferred_element_type=F32)


def _dot_nt(a, b):
    return lax.dot_general(a, b, (((1,), (1,)), ((), ())), preferred_element_type=F32)


def _dot_tn(a, b):
    return lax.dot_general(a, b, (((0,), (0,)), ((), ())), preferred_element_type=F32)


def _rms(x, g, eps=RMS_EPS):
    ms = jnp.mean(x * x, axis=-1, keepdims=True)
    return x * lax.rsqrt(ms + eps) * g


def _sigmoid(x):
    return 1.0 / (1.0 + jnp.exp(-x))


def _row_tile(m, cap=768):
    for t in (768, 512, 384, 256, 128, 64, 32, 16):
        if t <= cap and m % t == 0:
            return t
    raise ValueError(f"unsupported row count {m}")


def _proj_residual_kernel(x_ref, w_ref, h_ref, g_ref, o_ref):
    y = _dot(x_ref[...], w_ref[...])
    o_ref[...] = h_ref[...] + _rms(y, g_ref[...])


def _proj_residual(x2, w_all, layer, h2, g):
    m, k = x2.shape
    d = w_all.shape[2]
    bm = _row_tile(m)
    return pl.pallas_call(
        _proj_residual_kernel,
        out_shape=jax.ShapeDtypeStruct((m, d), F32),
        grid=(m // bm,),
        in_specs=[pl.BlockSpec((bm, k), lambda i: (i, 0)),
                  pl.BlockSpec((None, k, d), lambda i: (layer, 0, 0)),
                  pl.BlockSpec((bm, d), lambda i: (i, 0)),
                  pl.BlockSpec((1, d), lambda i: (0, 0))],
        out_specs=pl.BlockSpec((bm, d), lambda i: (i, 0)),
        compiler_params=_cparams(("parallel",)),
        name="proj_residual",
    )(x2, w_all, h2, g.reshape(1, d))


FFN_HIDDEN_TILE = 512
FFN_OUT_TILE = 512
FFN_ROW_SPLIT = 2


def _ffn_kernel(h_ref, gpre_ref, gpost_ref, wg_ref, wu_ref, wd_ref, o_ref, xn_ref):
    f = pl.program_id(1)
    bm, d = o_ref.shape

    @pl.when(f == 0)
    def _():
        xn_ref[...] = _rms(h_ref[...], gpre_ref[...]).astype(BF16)
        o_ref[...] = jnp.zeros_like(o_ref)

    half = bm // FFN_ROW_SPLIT
    for r0 in range(0, bm, half):
        x = xn_ref[r0:r0 + half, :]
        gate = _dot(x, wg_ref[...])
        up = _dot(x, wu_ref[...])
        mid = (gate * _sigmoid(gate) * up).astype(BF16)
        for n0 in range(0, d, FFN_OUT_TILE):
            o_ref[r0:r0 + half, n0:n0 + FFN_OUT_TILE] += _dot(mid, wd_ref[:, n0:n0 + FFN_OUT_TILE])

    @pl.when(f == pl.num_programs(1) - 1)
    def _():
        o_ref[...] = h_ref[...] + _rms(o_ref[...], gpost_ref[...])


def _ffn(h2, gpre, gpost, wg, wu, wd, layer):
    m, d = h2.shape
    f = wg.shape[2]
    bm = _row_tile(m)
    bf = FFN_HIDDEN_TILE if f % FFN_HIDDEN_TILE == 0 else f
    assert d % FFN_OUT_TILE == 0 and bm % (16 * FFN_ROW_SPLIT) == 0
    return pl.pallas_call(
        _ffn_kernel,
        out_shape=jax.ShapeDtypeStruct((m, d), F32),
        grid=(m // bm, f // bf),
        in_specs=[pl.BlockSpec((bm, d), lambda i, j: (i, 0)),
                  pl.BlockSpec((1, d), lambda i, j: (0, 0)),
                  pl.BlockSpec((1, d), lambda i, j: (0, 0)),
                  pl.BlockSpec((None, d, bf), lambda i, j: (layer, 0, j)),
                  pl.BlockSpec((None, d, bf), lambda i, j: (layer, 0, j)),
                  pl.BlockSpec((None, bf, d), lambda i, j: (layer, j, 0))],
        out_specs=pl.BlockSpec((bm, d), lambda i, j: (i, 0)),
        scratch_shapes=[pltpu.VMEM((bm, d), BF16)],
        compiler_params=_cparams(("parallel", "arbitrary")),
        name="ffn",
    )(h2, gpre.reshape(1, d), gpost.reshape(1, d), wg, wu, wd)


def _rkv_kernel(hn_ref, xx_ref, mix_ref, w_ref, o_ref):
    xs = (hn_ref[...].astype(F32) + xx_ref[...].astype(F32) * mix_ref[0]).astype(BF16)
    o_ref[...] = _dot(xs, w_ref[...]).astype(BF16)


def _rkv_proj(hn2, xx2, mix3, w_rkv_all, layer):
    m, d = hn2.shape
    bm = _row_tile(m)
    return pl.pallas_call(
        _rkv_kernel,
        out_shape=jax.ShapeDtypeStruct((m, 3 * d), BF16),
        grid=(m // bm, 3),
        in_specs=[pl.BlockSpec((bm, d), lambda i, s: (i, 0)),
                  pl.BlockSpec((bm, d), lambda i, s: (i, 0)),
                  pl.BlockSpec((1, 1, d), lambda i, s: (s, 0, 0)),
                  pl.BlockSpec((None, None, d, d), lambda i, s: (layer, s, 0, 0))],
        out_specs=pl.BlockSpec((bm, d), lambda i, s: (i, s)),
        compiler_params=_cparams(("parallel", "parallel")),
        name="rkv_proj",
    )(hn2, xx2, mix3, w_rkv_all)


LORA_SMALL_TILE = 256


def _shift_lora_kernel(*refs, has_vres, w_rank):
    if has_vres:
        (h_ref, halo_ref, gpre_ref, wf_ref, w0_ref, w2_ref, a0_ref, a2_ref, g2_ref, v0_ref, v2_ref,
         hn_ref, xx_ref, ld_ref, a_ref, g_ref, vg_ref) = refs
    else:
        (h_ref, halo_ref, gpre_ref, wf_ref, w0_ref, w2_ref, a0_ref, a2_ref, g2_ref,
         hn_ref, xx_ref, ld_ref, a_ref, g_ref) = refs
    gpre = gpre_ref[...]
    hn = _rms(h_ref[0], gpre)
    prev = jnp.where(pl.program_id(1) == 0, 0.0, _rms(halo_ref[0], gpre)[7:8])
    row = lax.broadcasted_iota(jnp.int32, hn.shape, 0)
    shifted = jnp.where(row == 0, prev, pltpu.roll(hn, 1, axis=0))
    hn_b = hn.astype(BF16)
    xx_b = (shifted - hn).astype(BF16)
    hn_ref[0] = hn_b
    xx_ref[0] = xx_b
    t = _dot(jnp.concatenate([hn_b, xx_b], axis=1), wf_ref[...])
    small = t[:, :LORA_SMALL_TILE]
    lane = lax.broadcasted_iota(jnp.int32, (1, LORA_SMALL_TILE), 1)
    act = jnp.where(lane < w_rank, jnp.tanh(small), small).astype(BF16)
    z = w0_ref[...] + _dot(act, w2_ref[...])
    ld_ref[0] = -jnp.exp(F32(-0.5)) * _sigmoid(z)
    a_ref[0] = _sigmoid(a0_ref[...] + _dot(act, a2_ref[...])).astype(BF16)
    g_ref[0] = _dot(_sigmoid(t[:, LORA_SMALL_TILE:]).astype(BF16), g2_ref[...]).astype(BF16)
    if has_vres:
        vg_ref[0] = _sigmoid(v0_ref[...] + _dot(act, v2_ref[...])).astype(BF16)


def _shift_lora(h3, gpre, mix, w0, w1, w2, a0, a1, a2, g1, g2, vres):
    b, lp, d = h3.shape
    tl = next(t for t in range(min(lp, 384), 0, -16) if lp % t == 0)
    halo_blocks = tl // 8
    has_vres = vres is not None
    downs = [(w1, mix[0]), (a1, mix[1])] + ([(vres[1], mix[3])] if has_vres else [])
    ups = [w2, a2] + ([vres[2]] if has_vres else [])
    ranks = [w.shape[1] for w, _ in downs]
    assert sum(ranks) <= LORA_SMALL_TILE
    pad = LORA_SMALL_TILE - sum(ranks)
    plain = jnp.concatenate([w for w, _ in downs] + [jnp.zeros((d, pad), F32), g1], axis=1)
    mixed = jnp.concatenate([w * mx[:, None] for w, mx in downs] + [jnp.zeros((d, pad), F32), g1 * mix[2][:, None]],
                            axis=1)
    w_first = jnp.concatenate([plain, mixed], axis=0).astype(BF16)

    def placed(up, offset):
        return jnp.pad(up, ((offset, LORA_SMALL_TILE - offset - up.shape[0]), (0, 0))).astype(BF16)

    offsets = [sum(ranks[:i]) for i in range(len(ranks))]
    ups = [placed(u, o) for u, o in zip(ups, offsets)]
    consts = [gpre.reshape(1, d), w_first, w0.reshape(1, d), ups[0], a0.reshape(1, d), ups[1], g2.astype(BF16)]
    if has_vres:
        consts += [vres[0].reshape(1, d), ups[2]]
    row = pl.BlockSpec((1, tl, d), lambda bi, i: (bi, i, 0))
    halo = pl.BlockSpec((1, 8, d), lambda bi, i: (bi, jnp.maximum(i * halo_blocks - 1, 0), 0))

    def full(a):
        return pl.BlockSpec(a.shape, lambda bi, i: (0, 0))

    dtypes = [BF16, BF16, F32, BF16, BF16] + ([BF16] if has_vres else [])
    return pl.pallas_call(
        functools.partial(_shift_lora_kernel, has_vres=has_vres, w_rank=ranks[0]),
        out_shape=tuple(jax.ShapeDtypeStruct((b, lp, d), t) for t in dtypes),
        grid=(b, lp // tl),
        in_specs=[row, halo] + [full(a) for a in consts],
        out_specs=tuple(row for _ in dtypes),
        compiler_params=_cparams(("parallel", "parallel")),
        name="shift_lora",
    )(h3, h3, *consts)


def _split_dot(a_bf16, x):
    hi = x.astype(BF16)
    lo = (x - hi.astype(F32)).astype(BF16)
    return _dot(a_bf16, hi) + _dot(a_bf16, lo)


def _split_dot_rhs(x, w_bf16):
    hi = x.astype(BF16)
    lo = (x - hi.astype(F32)).astype(BF16)
    return _dot(hi, w_bf16) + _dot(lo, w_bf16)


def _wkv_kernel(*refs, n_chunks, n_groups, has_vres):
    if has_vres:
        (r_ref, k_ref, v_ref, ld_ref, a_ref, g_ref, vf_ref, vg_ref,
         kk_ref, ka_ref, rk_ref, lg_ref, lb_ref, mbd_ref, msk_ref, tri_ref, o_ref, s_ref) = refs
    else:
        (r_ref, k_ref, v_ref, ld_ref, a_ref, g_ref,
         kk_ref, ka_ref, rk_ref, lg_ref, lb_ref, mbd_ref, msk_ref, tri_ref, o_ref, s_ref) = refs
    C, G, HG, N = WKV_CHUNK, WKV_GROUP, WKV_HEADS_PER_GROUP, RWKV_HEAD

    @pl.when(pl.program_id(2) == 0)
    def _():
        s_ref[...] = jnp.zeros_like(s_ref)

    def bd(x):
        xb = x.astype(BF16)
        return jnp.concatenate([xb] * HG, axis=0) * mbd_ref[...]

    gs = range(n_groups)
    lanes = [slice(gi * G, (gi + 1) * G) for gi in gs]

    def group_rows(x, n):
        return [x[i * n:(i + 1) * n] for i in gs]

    def chunk(j, carry):
        rows = pl.ds(pl.multiple_of(j * C, C), C)
        strict, incl, eye = msk_ref[0], msk_ref[1], msk_ref[2]
        r = [r_ref[0, rows, lanes[i]].astype(F32) for i in gs]
        k = [k_ref[0, rows, lanes[i]].astype(F32) for i in gs]
        v = [v_ref[0, rows, lanes[i]].astype(F32) for i in gs]
        a_s = [a_ref[0, rows, lanes[i]].astype(F32) for i in gs]
        if has_vres:
            v = [v[i] + (vf_ref[0, rows, lanes[i]].astype(F32) - v[i]) * vg_ref[0, rows, lanes[i]].astype(F32)
                 for i in gs]
        kk = [k[i] * kk_ref[:, lanes[i]] for i in gs]
        kp = [k[i] * (1.0 + (a_s[i] - 1.0) * ka_ref[:, lanes[i]]) for i in gs]
        lw = [_split_dot(tri_ref[...], ld_ref[0, rows, lanes[i]]) for i in gs]
        sums = group_rows(_dot(jnp.concatenate(
            [jnp.concatenate([kk[i] * kk[i], r[i] * kp[i] * rk_ref[:, lanes[i]]], axis=0).astype(BF16) for i in gs],
            axis=0), mbd_ref[...]), 2 * C)
        kk = [kk[i] / jnp.maximum(jnp.sqrt(sums[i][:C]), 1e-12) for i in gs]
        bb = [kk[i] * a_s[i] for i in gs]
        lw_last = [lw[i][C - 1:C, :] for i in gs]
        w_inv = [jnp.exp(-lw[i]) for i in gs]
        w_tail = [jnp.exp(lw_last[i] - lw[i]) for i in gs]
        a_t = [(-kk[i] * jnp.exp(lw[i] - ld_ref[0, rows, lanes[i]])).astype(BF16) for i in gs]
        r_t = [(r[i] * jnp.exp(lw[i])).astype(BF16) for i in gs]

        sc = [_dot_nt(jnp.concatenate([a_t[i], r_t[i]], axis=0),
                      jnp.concatenate([bd(bb[i] * w_inv[i]), bd(kp[i] * w_inv[i])], axis=0)) for i in gs]
        l_ab = [sc[i][:C, :G] * strict for i in gs]
        l_rb = [(sc[i][C:, :G] * incl).astype(BF16) for i in gs]
        l_k = [jnp.concatenate([sc[i][:C, G:] * strict, sc[i][C:, G:] * incl], axis=0).astype(BF16) for i in gs]
        v_bd = [bd(v[i]) for i in gs]
        lkv = [_dot(l_k[i], v_bd[i]) for i in gs]

        p = [l_ab[i].astype(BF16) for i in gs]
        t = [eye + l_ab[i] for i in gs]
        power = 1
        while power < C:
            if power == 1:
                p = [_dot(p[i], bd(p[i])).astype(BF16) for i in gs]
            elif power * 2 < C:
                res = [_dot(jnp.concatenate([p[i], t[i].astype(BF16)], axis=0), bd(p[i])) for i in gs]
                p = [res[i][:C].astype(BF16) for i in gs]
                t = [t[i] + res[i][C:] for i in gs]
            else:
                t = [t[i] + _dot(t[i].astype(BF16), bd(p[i])) for i in gs]
            power *= 2

        ta = [_dot(t[i].astype(BF16), jnp.concatenate([bd(a_t[i]), bd(lkv[i][:C])], axis=1)) for i in gs]
        s0 = [s_ref[i] for i in gs]
        res = [_dot_nt(jnp.concatenate([ta[i][:, :G].astype(BF16), r_t[i]], axis=0), s0[i].astype(BF16)) for i in gs]
        u_bd = [bd(res[i][:C] + ta[i][:, G:]) for i in gs]
        y = [res[i][C:] + lkv[i][C:] + _dot(l_rb[i], u_bd[i]) for i in gs]
        uv_stack = [jnp.concatenate([u_bd[i], v_bd[i]], axis=0) for i in gs]
        for i in gs:
            s_ref[i] = s0[i] * jnp.exp(lw_last[i]) + _dot_tn(
                uv_stack[i], jnp.concatenate([bd(bb[i] * w_tail[i]), bd(kp[i] * w_tail[i])], axis=0))

        y_hi = [y[i].astype(BF16) for i in gs]
        y_lo = [(y[i] - y_hi[i].astype(F32)).astype(BF16) for i in gs]
        mu2 = group_rows(_dot(jnp.concatenate([jnp.concatenate([y_hi[i], y_lo[i]], axis=0) for i in gs], axis=0),
                              mbd_ref[...]), 2 * C)
        dy = [y[i] - (mu2[i][:C] + mu2[i][C:]) * (1.0 / N) for i in gs]
        var = group_rows(_dot(jnp.concatenate([(dy[i] * dy[i]).astype(BF16) for i in gs], axis=0), mbd_ref[...]), C)
        var = [var[i] * (1.0 / N) for i in gs]
        for i in gs:
            yn = dy[i] * lax.rsqrt(var[i] + LNX_EPS) * lg_ref[:, lanes[i]] + lb_ref[:, lanes[i]]
            o_ref[0, rows, lanes[i]] = ((yn + sums[i][C:] * v[i]) * g_ref[0, rows, lanes[i]].astype(F32)).astype(BF16)
        return carry

    lax.fori_loop(0, n_chunks, chunk, 0, unroll=True)


def _wkv_tables():
    C, G, N = WKV_CHUNK, WKV_GROUP, RWKV_HEAD
    idx = jnp.arange(G)
    mbd = (idx[:, None] // N == idx[None, :] // N).astype(BF16)
    t = jnp.arange(C)[:, None]
    s = (idx % N)[None, :]
    masks = jnp.stack([s < t, s <= t, s == t]).astype(F32)
    tri = (jnp.arange(C)[None, :] <= t).astype(BF16)
    return mbd, masks, tri


def _wkv(rkv, ld, a, g, rkv_first, vg, k_k, k_a, r_k, lnx_g, lnx_b):
    b, lp, d = ld.shape
    n_groups = next(n for n in (WKV_GROUPS_PER_STEP, 4, 2, 1) if d % (n * WKV_GROUP) == 0)
    width = n_groups * WKV_GROUP
    n_lane_blocks = d // width
    n_chunks = next(n for n in (WKV_CHUNKS_PER_STEP, 2, 1) if (lp // WKV_CHUNK) % n == 0)
    tl = n_chunks * WKV_CHUNK
    has_vres = rkv_first is not None

    def seq(col):
        return pl.BlockSpec((1, tl, width), lambda bi, hg, c: (bi, c, hg + col * n_lane_blocks))

    par = pl.BlockSpec((1, width), lambda bi, hg, c: (0, hg))
    args = [rkv, rkv, rkv, ld, a, g] + ([rkv_first, vg] if has_vres else [])
    specs = [seq(0), seq(1), seq(2), seq(0), seq(0), seq(0)] + ([seq(2), seq(0)] if has_vres else [])
    params = [p.reshape(1, d) for p in (k_k, k_a, r_k, lnx_g, lnx_b)]
    tables = _wkv_tables()
    table_specs = [pl.BlockSpec(t.shape, lambda bi, hg, c, nd=t.ndim: (0,) * nd) for t in tables]
    return pl.pallas_call(
        functools.partial(_wkv_kernel, n_chunks=n_chunks, n_groups=n_groups, has_vres=has_vres),
        out_shape=jax.ShapeDtypeStruct((b, lp, d), BF16),
        grid=(b, n_lane_blocks, lp // tl),
        in_specs=specs + [par] * len(params) + table_specs,
        out_specs=seq(0),
        scratch_shapes=[pltpu.VMEM((n_groups, WKV_GROUP, WKV_GROUP), F32)],
        compiler_params=_cparams(("parallel", "parallel", "arbitrary")),
        name="wkv7",
    )(*args, *params, *tables)


def _rwkv_layer(h3, j, rkv_first, p):
    b, lp, d = h3.shape
    m = b * lp
    mix = p["rwkv_mix"][j]
    vres = None if j == 0 else (p["rwkv_v0"][j - 1], p["rwkv_v1"][j - 1], p["rwkv_v2"][j - 1])
    outs = _shift_lora(h3, p["norm_mix_pre"], jnp.stack([mix[1], mix[4], mix[5], mix[3]]), p["rwkv_w0"][j],
                       p["rwkv_w1"][j], p["rwkv_w2"][j], p["rwkv_a0"][j], p["rwkv_a1"][j], p["rwkv_a2"][j],
                       p["rwkv_g1"][j], p["rwkv_g2"][j], vres)
    hn, xx, ld, a, g = outs[:5]
    vg = outs[5] if vres is not None else None
    rkv = _rkv_proj(hn.reshape(m, d), xx.reshape(m, d), jnp.stack([mix[0], mix[2], mix[3]]).reshape(3, 1, d),
                    p["rwkv_w_rkv"], j).reshape(b, lp, 3 * d)
    if vres is None:
        rkv_first = rkv
    yg = _wkv(rkv, ld, a, g, rkv_first if vres is not None else None, vg,
              p["rwkv_k_k"][j], p["rwkv_k_a"][j], p["rwkv_r_k"][j].reshape(d),
              p["rwkv_lnx_g"][j], p["rwkv_lnx_b"][j])
    h2 = _proj_residual(yg.reshape(m, d), p["rwkv_w_o"], j, h3.reshape(m, d), p["norm_mix_post"])
    return h2.reshape(b, lp, d), rkv_first


def _mla_proj_kernel(h_ref, g_ref, win_ref, qn_ref, wuq_ref, kvn_ref, wukv_ref, cos_ref, sin_ref,
                     q_ref, kv_ref, kpe_ref, *, scale):
    hn = _rms(h_ref[...], g_ref[...]).astype(BF16)
    lat = _dot(hn, win_ref[...])
    cq = _rms(lat[:, :Q_LORA], qn_ref[...]).astype(BF16)
    ckv = _rms(lat[:, Q_LORA:Q_LORA + KV_LORA], kvn_ref[...]).astype(BF16)
    cos, sin = cos_ref[...], sin_ref[...]

    def rope(x):
        return x * cos + pltpu.roll(x, ROPE_D // 2, axis=1) * sin

    keep = (lax.broadcasted_iota(jnp.int32, (1, LANES), 1) < ROPE_D).astype(F32)
    kpe_ref[...] = (rope(lat[:, Q_LORA + KV_LORA:]) * keep).astype(BF16)
    kv_ref[...] = _dot(ckv, wukv_ref[...]).astype(BF16)
    q = _dot(cq, wuq_ref[...])
    hw = NOPE_D + LANES
    for h in range(MLA_HEADS):
        q_ref[:, h * hw:h * hw + NOPE_D] = (q[:, h * hw:h * hw + NOPE_D] * scale).astype(BF16)
        q_ref[:, h * hw + NOPE_D:(h + 1) * hw] = (rope(q[:, h * hw + NOPE_D:(h + 1) * hw]) * scale).astype(BF16)


def _mla_attn_kernel(q_ref, kv_ref, kpe_ref, o_ref, kcat_ref, vone_ref):
    lp = q_ref.shape[1]
    kcat_ref[:, :NOPE_D] = kv_ref[0, :, :NOPE_D]
    kcat_ref[:, NOPE_D:] = kpe_ref[0]
    vone_ref[:, :V_D] = kv_ref[0, :, NOPE_D:]
    vone_ref[:, V_D:] = jnp.ones((lp, LANES), BF16)
    tiles = [(q0, min(q0 + ATTN_Q_TILE, lp)) for q0 in range(0, lp, ATTN_Q_TILE)]

    def scores(q0, q1):
        q = q_ref[0, q0:q1, :]
        sd = _dot_nt(q, kcat_ref[q0:q1, :])
        sp = _dot_nt(q, kcat_ref[:q0, :]) if q0 > 0 else None
        return sd, sp

    def finish(q0, q1, sd, sp):
        row = lax.broadcasted_iota(jnp.int32, sd.shape, 0)
        col = lax.broadcasted_iota(jnp.int32, sd.shape, 1)
        sd = jnp.where(col <= row, sd, -1e30)
        mx = jnp.max(sd, axis=-1, keepdims=True)
        if sp is not None:
            mx = jnp.maximum(mx, jnp.max(sp, axis=-1, keepdims=True))
        acc = _dot(jnp.exp(sd - mx).astype(BF16), vone_ref[q0:q1, :])
        if sp is not None:
            acc = acc + _dot(jnp.exp(sp - mx).astype(BF16), vone_ref[:q0, :])
        o_ref[0, q0:q1, :] = (acc[:, :V_D] / acc[:, V_D:V_D + 1]).astype(BF16)

    pending = scores(*tiles[0])
    for idx, (q0, q1) in enumerate(tiles):
        upcoming = scores(*tiles[idx + 1]) if idx + 1 < len(tiles) else None
        finish(q0, q1, *pending)
        pending = upcoming


def _rope_tables(b, lp):
    half = ROPE_D // 2
    inv_freq = 1.0 / (ROPE_THETA ** (jnp.arange(0, ROPE_D, 2, dtype=F32) / ROPE_D))
    ang = jnp.arange(lp, dtype=jnp.int32).astype(F32)[:, None] * inv_freq[None, :]
    cos, sin = jnp.cos(ang), jnp.sin(ang)
    cos4 = jnp.tile(cos, (b, LANES // half))
    sin4 = jnp.tile(jnp.concatenate([-sin, sin], axis=1), (b, LANES // ROPE_D))
    return cos4, sin4


def _mla_layer(h3, j, p):
    b, lp, d = h3.shape
    m = b * lp
    hq = MLA_HEADS
    hw = NOPE_D + LANES
    bm = _row_tile(m, 384)
    w_in = p["mla_w_in"][j]
    w_in = jnp.concatenate([w_in, w_in[:, Q_LORA + KV_LORA:]], axis=1).astype(BF16)
    w_uq = p["mla_w_uq"][j].reshape(Q_LORA, hq, NOPE_D + ROPE_D)
    w_uq = jnp.concatenate([w_uq, w_uq[:, :, NOPE_D:]], axis=2).reshape(Q_LORA, hq * hw).astype(BF16)
    w_ukv = p["mla_w_ukv"][j].astype(BF16)
    cos4, sin4 = _rope_tables(b, lp)
    scale = float((NOPE_D + ROPE_D) ** -0.5)

    def full(a):
        return pl.BlockSpec(a.shape, lambda i: (0, 0))

    consts = [p["norm_mix_pre"].reshape(1, d), w_in, p["mla_q_norm"][j].reshape(1, Q_LORA), w_uq,
              p["mla_kv_norm"][j].reshape(1, KV_LORA), w_ukv]
    q, kv, kpe = pl.pallas_call(
        functools.partial(_mla_proj_kernel, scale=scale),
        out_shape=(jax.ShapeDtypeStruct((m, hq * hw), BF16),
                   jax.ShapeDtypeStruct((m, hq * (NOPE_D + V_D)), BF16),
                   jax.ShapeDtypeStruct((m, LANES), BF16)),
        grid=(m // bm,),
        in_specs=[pl.BlockSpec((bm, d), lambda i: (i, 0))] + [full(a) for a in consts]
                 + [pl.BlockSpec((bm, LANES), lambda i: (i, 0))] * 2,
        out_specs=(pl.BlockSpec((bm, hq * hw), lambda i: (i, 0)),
                   pl.BlockSpec((bm, hq * (NOPE_D + V_D)), lambda i: (i, 0)),
                   pl.BlockSpec((bm, LANES), lambda i: (i, 0))),
        compiler_params=_cparams(("parallel",)),
        name="mla_proj",
    )(h3.reshape(m, d), *consts, cos4, sin4)

    o = pl.pallas_call(
        _mla_attn_kernel,
        out_shape=jax.ShapeDtypeStruct((b, lp, hq * V_D), BF16),
        grid=(b, hq),
        in_specs=[pl.BlockSpec((1, lp, hw), lambda bi, h: (bi, 0, h)),
                  pl.BlockSpec((1, lp, NOPE_D + V_D), lambda bi, h: (bi, 0, h)),
                  pl.BlockSpec((1, lp, LANES), lambda bi, h: (bi, 0, 0))],
        out_specs=pl.BlockSpec((1, lp, V_D), lambda bi, h: (bi, 0, h)),
        scratch_shapes=[pltpu.VMEM((lp, hw), BF16), pltpu.VMEM((lp, V_D + LANES), BF16)],
        compiler_params=_cparams(("parallel", "parallel")),
        name="mla_attention",
    )(q.reshape(b, lp, hq * hw), kv.reshape(b, lp, hq * (NOPE_D + V_D)), kpe.reshape(b, lp, LANES))

    h2 = _proj_residual(o.reshape(m, hq * V_D), p["mla_w_o"], j, h3.reshape(m, d), p["norm_mix_post"])
    return h2.reshape(b, lp, d)


POOL_HALO = 16


def _pool_kernel(h_ref, halo_ref, gpre_ref, w_ref, sc_ref, gpost_ref, o_ref):
    i = pl.program_id(1)
    tl, d = h_ref.shape[1], h_ref.shape[2]
    gw = d // len(POOL_WINDOWS)
    h = h_ref[0]
    hn = _rms(h, gpre_ref[...])
    hist = jnp.where(i == 0, 0.0, _rms(halo_ref[0], gpre_ref[...]))
    ext = jnp.concatenate([hist, hn], axis=0)
    t = i * tl + lax.broadcasted_iota(jnp.int32, (tl, gw), 0)
    ys = []
    for gi, win in enumerate(POOL_WINDOWS):
        cols = slice(gi * gw, (gi + 1) * gw)
        s = ext[:, cols]
        step = 1
        while step < win:
            s = s + pltpu.roll(s, step, axis=0)
            step *= 2
        cnt = jnp.minimum(t + 1, win).astype(F32)
        pooled = (s[POOL_HALO:] / cnt - hn[:, cols]).astype(BF16)
        ys.append(_dot(pooled, w_ref[gi]) * sc_ref[:, cols])
    o_ref[0] = h + _rms(jnp.concatenate(ys, axis=1), gpost_ref[...])


def _pool_layer(h3, j, p):
    b, lp, d = h3.shape
    assert max(POOL_WINDOWS) <= POOL_HALO
    tl = lp // 4 if lp % (4 * POOL_HALO) == 0 else lp
    halo_blocks = tl // POOL_HALO

    def full(a):
        return pl.BlockSpec(a.shape, lambda bi, i, nd=a.ndim: (0,) * nd)

    consts = [p["norm_mix_pre"].reshape(1, d), p["pool_w"][j].astype(BF16), p["pool_scale"][j].reshape(1, d),
              p["norm_mix_post"].reshape(1, d)]
    return pl.pallas_call(
        _pool_kernel,
        out_shape=jax.ShapeDtypeStruct((b, lp, d), F32),
        grid=(b, lp // tl),
        in_specs=[pl.BlockSpec((1, tl, d), lambda bi, i: (bi, i, 0)),
                  pl.BlockSpec((1, POOL_HALO, d), lambda bi, i: (bi, jnp.maximum(i * halo_blocks - 1, 0), 0))]
                 + [full(a) for a in consts],
        out_specs=pl.BlockSpec((1, tl, d), lambda bi, i: (bi, i, 0)),
        compiler_params=_cparams(("parallel", "parallel")),
        name="pool_layer",
    )(h3, h3, *consts)


def kernel(x, meta_tokens, norm_mix_pre, norm_mix_post, norm_ffn_pre, norm_ffn_post, ffn_w_gate, ffn_w_up, ffn_w_down, rwkv_mix, rwkv_w0, rwkv_w1, rwkv_w2, rwkv_a0, rwkv_a1, rwkv_a2, rwkv_v0, rwkv_v1, rwkv_v2, rwkv_g1, rwkv_g2, rwkv_k_k, rwkv_k_a, rwkv_r_k, rwkv_w_rkv, rwkv_w_o, rwkv_lnx_g, rwkv_lnx_b, mla_w_in, mla_q_norm, mla_w_uq, mla_kv_norm, mla_w_ukv, mla_w_o, pool_w, pool_scale):
    b, seq, d = x.shape
    depth = norm_mix_pre.shape[0]
    ltot = N_META + seq
    lp = -(-ltot // SEQ_ALIGN) * SEQ_ALIGN
    meta = jnp.broadcast_to(meta_tokens.astype(x.dtype)[None], (b, N_META, d))
    h = jnp.concatenate([meta, x, jnp.zeros((b, lp - ltot, d), x.dtype)], axis=1)
    shared = dict(rwkv_mix=rwkv_mix, rwkv_w0=rwkv_w0, rwkv_w1=rwkv_w1, rwkv_w2=rwkv_w2, rwkv_a0=rwkv_a0,
                  rwkv_a1=rwkv_a1, rwkv_a2=rwkv_a2, rwkv_v0=rwkv_v0, rwkv_v1=rwkv_v1, rwkv_v2=rwkv_v2,
                  rwkv_g1=rwkv_g1, rwkv_g2=rwkv_g2, rwkv_k_k=rwkv_k_k, rwkv_k_a=rwkv_k_a, rwkv_r_k=rwkv_r_k,
                  rwkv_w_rkv=rwkv_w_rkv.astype(BF16), rwkv_w_o=rwkv_w_o.astype(BF16),
                  rwkv_lnx_g=rwkv_lnx_g, rwkv_lnx_b=rwkv_lnx_b,
                  mla_w_in=mla_w_in, mla_q_norm=mla_q_norm, mla_w_uq=mla_w_uq, mla_kv_norm=mla_kv_norm,
                  mla_w_ukv=mla_w_ukv, mla_w_o=mla_w_o.astype(BF16), pool_w=pool_w, pool_scale=pool_scale)
    w_gate, w_up, w_down = ffn_w_gate.astype(BF16), ffn_w_up.astype(BF16), ffn_w_down.astype(BF16)
    v_first = None
    for i in range(depth):
        p = dict(shared, norm_mix_pre=norm_mix_pre[i], norm_mix_post=norm_mix_post[i])
        kind, j = i % 3, i // 3
        if kind == 0:
            h, v_first = _rwkv_layer(h, j, v_first, p)
        elif kind == 1:
            h = _mla_layer(h, j, p)
        else:
            h = _pool_layer(h, j, p)
        h = _ffn(h.reshape(b * lp, d), norm_ffn_pre[i], norm_ffn_post[i], w_gate, w_up, w_down, i).reshape(b, lp, d)
    return h[:, N_META:ltot]
```

```python
import functools

import jax
import jax.numpy as jnp
from jax import lax
from jax.experimental import pallas as pl
from jax.experimental.pallas import tpu as pltpu

F32 = jnp.float32
BF16 = jnp.bfloat16

N_META = 16
RMS_EPS = 1e-6
RWKV_HEAD = 64
LNX_EPS = 1e-5 * RWKV_HEAD
MLA_HEADS = 16
Q_LORA = 512
KV_LORA = 512
NOPE_D = 128
ROPE_D = 64
V_D = 128
ROPE_THETA = 10000.0
POOL_WINDOWS = (2, 4, 8, 16)

LANES = 128
WKV_CHUNK = 64
SEQ_ALIGN = WKV_CHUNK
WKV_HEADS_PER_GROUP = 4
WKV_GROUP = WKV_HEADS_PER_GROUP * RWKV_HEAD
WKV_GROUPS_PER_STEP = 8
WKV_CHUNKS_PER_STEP = 3
ATTN_Q_TILE = 256
VMEM_LIMIT = 56 * 1024 * 1024


def _cparams(sem):
    return pltpu.CompilerParams(dimension_semantics=sem, vmem_limit_bytes=VMEM_LIMIT)


def _dot(a, b):
    return jnp.dot(a, b, preferred_element_type=F32)


def _dot_nt(a, b):
    return lax.dot_general(a, b, (((1,), (1,)), ((), ())), preferred_element_type=F32)


def _dot_tn(a, b):
    return lax.dot_general(a, b, (((0,), (0,)), ((), ())), preferred_element_type=F32)


def _rms(x, g, eps=RMS_EPS):
    ms = jnp.mean(x * x, axis=-1, keepdims=True)
    return x * lax.rsqrt(ms + eps) * g


def _sigmoid(x):
    return 1.0 / (1.0 + jnp.exp(-x))


def _row_tile(m, cap=768):
    for t in (768, 512, 384, 256, 128, 64, 32, 16):
        if t <= cap and m % t == 0:
            return t
    raise ValueError(f"unsupported row count {m}")


def _proj_residual_kernel(x_ref, w_ref, h_ref, g_ref, o_ref):
    half = o_ref.shape[0] // 2
    for r0 in (0, half):
        rs = slice(r0, r0 + half)
        o_ref[rs, :] = h_ref[rs, :] + _rms(_dot(x_ref[rs, :], w_ref[...]), g_ref[...])


def _proj_residual(x2, w_all, layer, h2, g):
    m, k = x2.shape
    d = w_all.shape[2]
    bm = _row_tile(m)
    return pl.pallas_call(
        _proj_residual_kernel,
        out_shape=jax.ShapeDtypeStruct((m, d), F32),
        grid=(m // bm,),
        in_specs=[pl.BlockSpec((bm, k), lambda i: (i, 0)),
                  pl.BlockSpec((None, k, d), lambda i: (layer, 0, 0)),
                  pl.BlockSpec((bm, d), lambda i: (i, 0)),
                  pl.BlockSpec((1, d), lambda i: (0, 0))],
        out_specs=pl.BlockSpec((bm, d), lambda i: (i, 0)),
        compiler_params=_cparams(("parallel",)),
        name="proj_residual",
    )(x2, w_all, h2, g.reshape(1, d))


FFN_HIDDEN_TILE = 512
FFN_OUT_TILE = 512
FFN_ROW_SPLIT = 2


def _ffn_kernel(h_ref, gpre_ref, gpost_ref, wg_ref, wu_ref, wd_ref, o_ref, xn_ref):
    f = pl.program_id(1)
    last_f = pl.num_programs(1) - 1
    bm, d = o_ref.shape
    rows = bm // FFN_ROW_SPLIT

    def step(first, last):
        for r0 in range(0, bm, rows):
            rs = slice(r0, r0 + rows)
            if first:
                xn_ref[rs, :] = _rms(h_ref[rs, :], gpre_ref[...]).astype(BF16)
            x = xn_ref[rs, :]
            gate = _dot(x, wg_ref[...])
            up = _dot(x, wu_ref[...])
            mid = (gate * _sigmoid(gate) * up).astype(BF16)
            for n0 in range(0, d, FFN_OUT_TILE):
                ns = slice(n0, n0 + FFN_OUT_TILE)
                part = _dot(mid, wd_ref[:, ns])
                if first:
                    o_ref[rs, ns] = part
                else:
                    o_ref[rs, ns] += part
            if last:
                o_ref[rs, :] = h_ref[rs, :] + _rms(o_ref[rs, :], gpost_ref[...])

    pl.when(f == 0)(lambda: step(True, False))
    pl.when(jnp.logical_and(f > 0, f < last_f))(lambda: step(False, False))
    pl.when(f == last_f)(lambda: step(False, True))


def _ffn(h2, gpre, gpost, wg, wu, wd, layer):
    m, d = h2.shape
    f = wg.shape[2]
    bm = _row_tile(m)
    bf = FFN_HIDDEN_TILE if f % FFN_HIDDEN_TILE == 0 else f
    assert d % FFN_OUT_TILE == 0 and bm % (16 * FFN_ROW_SPLIT) == 0
    return pl.pallas_call(
        _ffn_kernel,
        out_shape=jax.ShapeDtypeStruct((m, d), F32),
        grid=(m // bm, f // bf),
        in_specs=[pl.BlockSpec((bm, d), lambda i, j: (i, 0)),
                  pl.BlockSpec((1, d), lambda i, j: (0, 0)),
                  pl.BlockSpec((1, d), lambda i, j: (0, 0)),
                  pl.BlockSpec((None, d, bf), lambda i, j: (layer, 0, j)),
                  pl.BlockSpec((None, d, bf), lambda i, j: (layer, 0, j)),
                  pl.BlockSpec((None, bf, d), lambda i, j: (layer, j, 0))],
        out_specs=pl.BlockSpec((bm, d), lambda i, j: (i, 0)),
        scratch_shapes=[pltpu.VMEM((bm, d), BF16)],
        compiler_params=_cparams(("parallel", "arbitrary")),
        name="ffn",
    )(h2, gpre.reshape(1, d), gpost.reshape(1, d), wg, wu, wd)


def _rkv_kernel(hn_ref, xx_ref, mix_ref, w_ref, o_ref):
    xs = (hn_ref[...].astype(F32) + xx_ref[...].astype(F32) * mix_ref[0]).astype(BF16)
    o_ref[...] = _dot(xs, w_ref[...]).astype(BF16)


def _rkv_proj(hn2, xx2, mix3, w_rkv_all, layer):
    m, d = hn2.shape
    bm = _row_tile(m)
    return pl.pallas_call(
        _rkv_kernel,
        out_shape=jax.ShapeDtypeStruct((m, 3 * d), BF16),
        grid=(m // bm, 3),
        in_specs=[pl.BlockSpec((bm, d), lambda i, s: (i, 0)),
                  pl.BlockSpec((bm, d), lambda i, s: (i, 0)),
                  pl.BlockSpec((1, 1, d), lambda i, s: (s, 0, 0)),
                  pl.BlockSpec((None, None, d, d), lambda i, s: (layer, s, 0, 0))],
        out_specs=pl.BlockSpec((bm, d), lambda i, s: (i, s)),
        compiler_params=_cparams(("parallel", "parallel")),
        name="rkv_proj",
    )(hn2, xx2, mix3, w_rkv_all)


LORA_SMALL_TILE = 256


def _shift_lora_kernel(*refs, has_vres, w_rank):
    if has_vres:
        (h_ref, halo_ref, gpre_ref, wf_ref, w0_ref, w2_ref, a0_ref, a2_ref, g2_ref, v0_ref, v2_ref,
         hn_ref, xx_ref, ld_ref, a_ref, g_ref, vg_ref) = refs
    else:
        (h_ref, halo_ref, gpre_ref, wf_ref, w0_ref, w2_ref, a0_ref, a2_ref, g2_ref,
         hn_ref, xx_ref, ld_ref, a_ref, g_ref) = refs
    gpre = gpre_ref[...]
    hn = _rms(h_ref[0], gpre)
    prev = jnp.where(pl.program_id(1) == 0, 0.0, _rms(halo_ref[0], gpre)[7:8])
    row = lax.broadcasted_iota(jnp.int32, hn.shape, 0)
    shifted = jnp.where(row == 0, prev, pltpu.roll(hn, 1, axis=0))
    hn_b = hn.astype(BF16)
    xx_b = (shifted - hn).astype(BF16)
    hn_ref[0] = hn_b
    xx_ref[0] = xx_b
    t = _dot(jnp.concatenate([hn_b, xx_b], axis=1), wf_ref[...])
    small = t[:, :LORA_SMALL_TILE]
    lane = lax.broadcasted_iota(jnp.int32, (1, LORA_SMALL_TILE), 1)
    act = jnp.where(lane < w_rank, jnp.tanh(small), small).astype(BF16)
    z = w0_ref[...] + _dot(act, w2_ref[...])
    ld_ref[0] = -jnp.exp(F32(-0.5)) * _sigmoid(z)
    a_ref[0] = _sigmoid(a0_ref[...] + _dot(act, a2_ref[...])).astype(BF16)
    g_ref[0] = _dot(_sigmoid(t[:, LORA_SMALL_TILE:]).astype(BF16), g2_ref[...]).astype(BF16)
    if has_vres:
        vg_ref[0] = _sigmoid(v0_ref[...] + _dot(act, v2_ref[...])).astype(BF16)


def _shift_lora(h3, gpre, mix, w0, w1, w2, a0, a1, a2, g1, g2, vres):
    b, lp, d = h3.shape
    tl = next(t for t in range(min(lp, 384), 0, -16) if lp % t == 0)
    halo_blocks = tl // 8
    has_vres = vres is not None
    downs = [(w1, mix[0]), (a1, mix[1])] + ([(vres[1], mix[3])] if has_vres else [])
    ups = [w2, a2] + ([vres[2]] if has_vres else [])
    ranks = [w.shape[1] for w, _ in downs]
    assert sum(ranks) <= LORA_SMALL_TILE
    pad = LORA_SMALL_TILE - sum(ranks)
    plain = jnp.concatenate([w for w, _ in downs] + [jnp.zeros((d, pad), F32), g1], axis=1)
    mixed = jnp.concatenate([w * mx[:, None] for w, mx in downs] + [jnp.zeros((d, pad), F32), g1 * mix[2][:, None]],
                            axis=1)
    w_first = jnp.concatenate([plain, mixed], axis=0).astype(BF16)

    def placed(up, offset):
        return jnp.pad(up, ((offset, LORA_SMALL_TILE - offset - up.shape[0]), (0, 0))).astype(BF16)

    offsets = [sum(ranks[:i]) for i in range(len(ranks))]
    ups = [placed(u, o) for u, o in zip(ups, offsets)]
    consts = [gpre.reshape(1, d), w_first, w0.reshape(1, d), ups[0], a0.reshape(1, d), ups[1], g2.astype(BF16)]
    if has_vres:
        consts += [vres[0].reshape(1, d), ups[2]]
    row = pl.BlockSpec((1, tl, d), lambda bi, i: (bi, i, 0))
    halo = pl.BlockSpec((1, 8, d), lambda bi, i: (bi, jnp.maximum(i * halo_blocks - 1, 0), 0))

    def full(a):
        return pl.BlockSpec(a.shape, lambda bi, i: (0, 0))

    dtypes = [BF16, BF16, F32, BF16, BF16] + ([BF16] if has_vres else [])
    return pl.pallas_call(
        functools.partial(_shift_lora_kernel, has_vres=has_vres, w_rank=ranks[0]),
        out_shape=tuple(jax.ShapeDtypeStruct((b, lp, d), t) for t in dtypes),
        grid=(b, lp // tl),
        in_specs=[row, halo] + [full(a) for a in consts],
        out_specs=tuple(row for _ in dtypes),
        compiler_params=_cparams(("parallel", "parallel")),
        name="shift_lora",
    )(h3, h3, *consts)


def _split_dot(a_bf16, x):
    hi = x.astype(BF16)
    lo = (x - hi.astype(F32)).astype(BF16)
    return _dot(a_bf16, hi) + _dot(a_bf16, lo)


def _split_dot_rhs(x, w_bf16):
    hi = x.astype(BF16)
    lo = (x - hi.astype(F32)).astype(BF16)
    return _dot(hi, w_bf16) + _dot(lo, w_bf16)


def _wkv_kernel(*refs, n_chunks, n_groups, has_vres):
    if has_vres:
        (r_ref, k_ref, v_ref, ld_ref, a_ref, g_ref, vf_ref, vg_ref,
         kk_ref, ka_ref, rk_ref, lg_ref, lb_ref, mbd_ref, msk_ref, tri_ref, o_ref, s_ref) = refs
    else:
        (r_ref, k_ref, v_ref, ld_ref, a_ref, g_ref,
         kk_ref, ka_ref, rk_ref, lg_ref, lb_ref, mbd_ref, msk_ref, tri_ref, o_ref, s_ref) = refs
    C, G, HG, N = WKV_CHUNK, WKV_GROUP, WKV_HEADS_PER_GROUP, RWKV_HEAD

    @pl.when(pl.program_id(2) == 0)
    def _():
        s_ref[...] = jnp.zeros_like(s_ref)

    def bd(x):
        xb = x.astype(BF16)
        return jnp.concatenate([xb] * HG, axis=0) * mbd_ref[...]

    gs = range(n_groups)
    lanes = [slice(gi * G, (gi + 1) * G) for gi in gs]

    def group_rows(x, n):
        return [x[i * n:(i + 1) * n] for i in gs]

    def chunk(j, carry):
        rows = pl.ds(pl.multiple_of(j * C, C), C)
        strict, incl, eye = msk_ref[0], msk_ref[1], msk_ref[2]
        r = [r_ref[0, rows, lanes[i]].astype(F32) for i in gs]
        k = [k_ref[0, rows, lanes[i]].astype(F32) for i in gs]
        v = [v_ref[0, rows, lanes[i]].astype(F32) for i in gs]
        a_s = [a_ref[0, rows, lanes[i]].astype(F32) for i in gs]
        if has_vres:
            v = [v[i] + (vf_ref[0, rows, lanes[i]].astype(F32) - v[i]) * vg_ref[0, rows, lanes[i]].astype(F32)
                 for i in gs]
        kk = [k[i] * kk_ref[:, lanes[i]] for i in gs]
        kp = [k[i] * (1.0 + (a_s[i] - 1.0) * ka_ref[:, lanes[i]]) for i in gs]
        lw = [_split_dot(tri_ref[...], ld_ref[0, rows, lanes[i]]) for i in gs]
        sums = group_rows(_dot(jnp.concatenate(
            [jnp.concatenate([kk[i] * kk[i], r[i] * kp[i] * rk_ref[:, lanes[i]]], axis=0).astype(BF16) for i in gs],
            axis=0), mbd_ref[...]), 2 * C)
        kk = [kk[i] / jnp.maximum(jnp.sqrt(sums[i][:C]), 1e-12) for i in gs]
        bb = [kk[i] * a_s[i] for i in gs]
        lw_last = [lw[i][C - 1:C, :] for i in gs]
        w_inv = [jnp.exp(-lw[i]) for i in gs]
        w_tail = [jnp.exp(lw_last[i] - lw[i]) for i in gs]
        a_t = [(-kk[i] * jnp.exp(lw[i] - ld_ref[0, rows, lanes[i]])).astype(BF16) for i in gs]
        r_t = [(r[i] * jnp.exp(lw[i])).astype(BF16) for i in gs]

        sc = [_dot_nt(jnp.concatenate([a_t[i], r_t[i]], axis=0),
                      jnp.concatenate([bd(bb[i] * w_inv[i]), bd(kp[i] * w_inv[i])], axis=0)) for i in gs]
        l_ab = [sc[i][:C, :G] * strict for i in gs]
        l_rb = [(sc[i][C:, :G] * incl).astype(BF16) for i in gs]
        l_k = [jnp.concatenate([sc[i][:C, G:] * strict, sc[i][C:, G:] * incl], axis=0).astype(BF16) for i in gs]
        v_bd = [bd(v[i]) for i in gs]
        lkv = [_dot(l_k[i], v_bd[i]) for i in gs]

        p = [l_ab[i].astype(BF16) for i in gs]
        t = [eye + l_ab[i] for i in gs]
        power = 1
        while power < C:
            if power == 1:
                p = [_dot(p[i], bd(p[i])).astype(BF16) for i in gs]
            elif power * 2 < C:
                res = [_dot(jnp.concatenate([p[i], t[i].astype(BF16)], axis=0), bd(p[i])) for i in gs]
                p = [res[i][:C].astype(BF16) for i in gs]
                t = [t[i] + res[i][C:] for i in gs]
            else:
                t = [t[i] + _dot(t[i].astype(BF16), bd(p[i])) for i in gs]
            power *= 2

        ta = [_dot(t[i].astype(BF16), jnp.concatenate([bd(a_t[i]), bd(lkv[i][:C])], axis=1)) for i in gs]
        s0 = [s_ref[i] for i in gs]
        res = [_dot_nt(jnp.concatenate([ta[i][:, :G].astype(BF16), r_t[i]], axis=0), s0[i].astype(BF16)) for i in gs]
        u_bd = [bd(res[i][:C] + ta[i][:, G:]) for i in gs]
        y = [res[i][C:] + lkv[i][C:] + _dot(l_rb[i], u_bd[i]) for i in gs]
        uv_stack = [jnp.concatenate([u_bd[i], v_bd[i]], axis=0) for i in gs]
        for i in gs:
            s_ref[i] = s0[i] * jnp.exp(lw_last[i]) + _dot_tn(
                uv_stack[i], jnp.concatenate([bd(bb[i] * w_tail[i]), bd(kp[i] * w_tail[i])], axis=0))

        y_hi = [y[i].astype(BF16) for i in gs]
        y_lo = [(y[i] - y_hi[i].astype(F32)).astype(BF16) for i in gs]
        mu2 = group_rows(_dot(jnp.concatenate([jnp.concatenate([y_hi[i], y_lo[i]], axis=0) for i in gs], axis=0),
                              mbd_ref[...]), 2 * C)
        dy = [y[i] - (mu2[i][:C] + mu2[i][C:]) * (1.0 / N) for i in gs]
        var = group_rows(_dot(jnp.concatenate([(dy[i] * dy[i]).astype(BF16) for i in gs], axis=0), mbd_ref[...]), C)
        var = [var[i] * (1.0 / N) for i in gs]
        for i in gs:
            yn = dy[i] * lax.rsqrt(var[i] + LNX_EPS) * lg_ref[:, lanes[i]] + lb_ref[:, lanes[i]]
            o_ref[0, rows, lanes[i]] = ((yn + sums[i][C:] * v[i]) * g_ref[0, rows, lanes[i]].astype(F32)).astype(BF16)
        return carry

    lax.fori_loop(0, n_chunks, chunk, 0, unroll=True)


def _wkv_tables():
    C, G, N = WKV_CHUNK, WKV_GROUP, RWKV_HEAD
    idx = jnp.arange(G)
    mbd = (idx[:, None] // N == idx[None, :] // N).astype(BF16)
    t = jnp.arange(C)[:, None]
    s = (idx % N)[None, :]
    masks = jnp.stack([s < t, s <= t, s == t]).astype(F32)
    tri = (jnp.arange(C)[None, :] <= t).astype(BF16)
    return mbd, masks, tri


def _wkv(rkv, ld, a, g, rkv_first, vg, k_k, k_a, r_k, lnx_g, lnx_b):
    b, lp, d = ld.shape
    n_groups = next(n for n in (WKV_GROUPS_PER_STEP, 4, 2, 1) if d % (n * WKV_GROUP) == 0)
    width = n_groups * WKV_GROUP
    n_lane_blocks = d // width
    n_chunks = next(n for n in (WKV_CHUNKS_PER_STEP, 2, 1) if (lp // WKV_CHUNK) % n == 0)
    tl = n_chunks * WKV_CHUNK
    has_vres = rkv_first is not None

    def seq(col):
        return pl.BlockSpec((1, tl, width), lambda bi, hg, c: (bi, c, hg + col * n_lane_blocks))

    par = pl.BlockSpec((1, width), lambda bi, hg, c: (0, hg))
    args = [rkv, rkv, rkv, ld, a, g] + ([rkv_first, vg] if has_vres else [])
    specs = [seq(0), seq(1), seq(2), seq(0), seq(0), seq(0)] + ([seq(2), seq(0)] if has_vres else [])
    params = [p.reshape(1, d) for p in (k_k, k_a, r_k, lnx_g, lnx_b)]
    tables = _wkv_tables()
    table_specs = [pl.BlockSpec(t.shape, lambda bi, hg, c, nd=t.ndim: (0,) * nd) for t in tables]
    return pl.pallas_call(
        functools.partial(_wkv_kernel, n_chunks=n_chunks, n_groups=n_groups, has_vres=has_vres),
        out_shape=jax.ShapeDtypeStruct((b, lp, d), BF16),
        grid=(b, n_lane_blocks, lp // tl),
        in_specs=specs + [par] * len(params) + table_specs,
        out_specs=seq(0),
        scratch_shapes=[pltpu.VMEM((n_groups, WKV_GROUP, WKV_GROUP), F32)],
        compiler_params=_cparams(("parallel", "parallel", "arbitrary")),
        name="wkv7",
    )(*args, *params, *tables)


def _rwkv_layer(h3, j, rkv_first, p):
    b, lp, d = h3.shape
    m = b * lp
    mix = p["rwkv_mix"][j]
    vres = None if j == 0 else (p["rwkv_v0"][j - 1], p["rwkv_v1"][j - 1], p["rwkv_v2"][j - 1])
    outs = _shift_lora(h3, p["norm_mix_pre"], jnp.stack([mix[1], mix[4], mix[5], mix[3]]), p["rwkv_w0"][j],
                       p["rwkv_w1"][j], p["rwkv_w2"][j], p["rwkv_a0"][j], p["rwkv_a1"][j], p["rwkv_a2"][j],
                       p["rwkv_g1"][j], p["rwkv_g2"][j], vres)
    hn, xx, ld, a, g = outs[:5]
    vg = outs[5] if vres is not None else None
    rkv = _rkv_proj(hn.reshape(m, d), xx.reshape(m, d), jnp.stack([mix[0], mix[2], mix[3]]).reshape(3, 1, d),
                    p["rwkv_w_rkv"], j).reshape(b, lp, 3 * d)
    if vres is None:
        rkv_first = rkv
    yg = _wkv(rkv, ld, a, g, rkv_first if vres is not None else None, vg,
              p["rwkv_k_k"][j], p["rwkv_k_a"][j], p["rwkv_r_k"][j].reshape(d),
              p["rwkv_lnx_g"][j], p["rwkv_lnx_b"][j])
    h2 = _proj_residual(yg.reshape(m, d), p["rwkv_w_o"], j, h3.reshape(m, d), p["norm_mix_post"])
    return h2.reshape(b, lp, d), rkv_first


def _mla_proj_kernel(h_ref, g_ref, win_ref, qn_ref, wuq_ref, kvn_ref, wukv_ref, cos_ref, sin_ref,
                     q_ref, kv_ref, kpe_ref, *, scale):
    hn = _rms(h_ref[...], g_ref[...]).astype(BF16)
    lat = _dot(hn, win_ref[...])
    cq = _rms(lat[:, :Q_LORA], qn_ref[...]).astype(BF16)
    ckv = _rms(lat[:, Q_LORA:Q_LORA + KV_LORA], kvn_ref[...]).astype(BF16)
    cos, sin = cos_ref[...], sin_ref[...]

    def rope(x):
        return x * cos + pltpu.roll(x, ROPE_D // 2, axis=1) * sin

    keep = (lax.broadcasted_iota(jnp.int32, (1, LANES), 1) < ROPE_D).astype(F32)
    kpe_ref[...] = (rope(lat[:, Q_LORA + KV_LORA:]) * keep).astype(BF16)
    kv_ref[...] = _dot(ckv, wukv_ref[...]).astype(BF16)
    q = _dot(cq, wuq_ref[...])
    hw = NOPE_D + LANES
    for h in range(MLA_HEADS):
        q_ref[:, h * hw:h * hw + NOPE_D] = (q[:, h * hw:h * hw + NOPE_D] * scale).astype(BF16)
        q_ref[:, h * hw + NOPE_D:(h + 1) * hw] = (rope(q[:, h * hw + NOPE_D:(h + 1) * hw]) * scale).astype(BF16)


def _mla_attn_kernel(q_ref, kv_ref, kpe_ref, o_ref, kcat_ref, vone_ref):
    lp = q_ref.shape[1]
    kcat_ref[:, :NOPE_D] = kv_ref[0, :, :NOPE_D]
    kcat_ref[:, NOPE_D:] = kpe_ref[0]
    vone_ref[:, :V_D] = kv_ref[0, :, NOPE_D:]
    vone_ref[:, V_D:] = jnp.ones((lp, LANES), BF16)
    tiles = [(q0, min(q0 + ATTN_Q_TILE, lp)) for q0 in range(0, lp, ATTN_Q_TILE)]

    def scores(q0, q1):
        q = q_ref[0, q0:q1, :]
        sd = _dot_nt(q, kcat_ref[q0:q1, :])
        sp = _dot_nt(q, kcat_ref[:q0, :]) if q0 > 0 else None
        return sd, sp

    def finish(q0, q1, sd, sp):
        row = lax.broadcasted_iota(jnp.int32, sd.shape, 0)
        col = lax.broadcasted_iota(jnp.int32, sd.shape, 1)
        sd = jnp.where(col <= row, sd, -1e30)
        mx = jnp.max(sd, axis=-1, keepdims=True)
        if sp is not None:
            mx = jnp.maximum(mx, jnp.max(sp, axis=-1, keepdims=True))
        acc = _dot(jnp.exp(sd - mx).astype(BF16), vone_ref[q0:q1, :])
        if sp is not None:
            acc = acc + _dot(jnp.exp(sp - mx).astype(BF16), vone_ref[:q0, :])
        o_ref[0, q0:q1, :] = (acc[:, :V_D] / acc[:, V_D:V_D + 1]).astype(BF16)

    pending = scores(*tiles[0])
    for idx, (q0, q1) in enumerate(tiles):
        upcoming = scores(*tiles[idx + 1]) if idx + 1 < len(tiles) else None
        finish(q0, q1, *pending)
        pending = upcoming


def _rope_tables(b, lp):
    half = ROPE_D // 2
    inv_freq = 1.0 / (ROPE_THETA ** (jnp.arange(0, ROPE_D, 2, dtype=F32) / ROPE_D))
    ang = jnp.arange(lp, dtype=jnp.int32).astype(F32)[:, None] * inv_freq[None, :]
    cos, sin = jnp.cos(ang), jnp.sin(ang)
    cos4 = jnp.tile(cos, (b, LANES // half))
    sin4 = jnp.tile(jnp.concatenate([-sin, sin], axis=1), (b, LANES // ROPE_D))
    return cos4, sin4


def _mla_layer(h3, j, p):
    b, lp, d = h3.shape
    m = b * lp
    hq = MLA_HEADS
    hw = NOPE_D + LANES
    bm = _row_tile(m, 384)
    w_in = p["mla_w_in"][j]
    w_in = jnp.concatenate([w_in, w_in[:, Q_LORA + KV_LORA:]], axis=1).astype(BF16)
    w_uq = p["mla_w_uq"][j].reshape(Q_LORA, hq, NOPE_D + ROPE_D)
    w_uq = jnp.concatenate([w_uq, w_uq[:, :, NOPE_D:]], axis=2).reshape(Q_LORA, hq * hw).astype(BF16)
    w_ukv = p["mla_w_ukv"][j].astype(BF16)
    cos4, sin4 = _rope_tables(b, lp)
    scale = float((NOPE_D + ROPE_D) ** -0.5)

    def full(a):
        return pl.BlockSpec(a.shape, lambda i: (0, 0))

    consts = [p["norm_mix_pre"].reshape(1, d), w_in, p["mla_q_norm"][j].reshape(1, Q_LORA), w_uq,
              p["mla_kv_norm"][j].reshape(1, KV_LORA), w_ukv]
    q, kv, kpe = pl.pallas_call(
        functools.partial(_mla_proj_kernel, scale=scale),
        out_shape=(jax.ShapeDtypeStruct((m, hq * hw), BF16),
                   jax.ShapeDtypeStruct((m, hq * (NOPE_D + V_D)), BF16),
                   jax.ShapeDtypeStruct((m, LANES), BF16)),
        grid=(m // bm,),
        in_specs=[pl.BlockSpec((bm, d), lambda i: (i, 0))] + [full(a) for a in consts]
                 + [pl.BlockSpec((bm, LANES), lambda i: (i, 0))] * 2,
        out_specs=(pl.BlockSpec((bm, hq * hw), lambda i: (i, 0)),
                   pl.BlockSpec((bm, hq * (NOPE_D + V_D)), lambda i: (i, 0)),
                   pl.BlockSpec((bm, LANES), lambda i: (i, 0))),
        compiler_params=_cparams(("parallel",)),
        name="mla_proj",
    )(h3.reshape(m, d), *consts, cos4, sin4)

    o = pl.pallas_call(
        _mla_attn_kernel,
        out_shape=jax.ShapeDtypeStruct((b, lp, hq * V_D), BF16),
        grid=(b, hq),
        in_specs=[pl.BlockSpec((1, lp, hw), lambda bi, h: (bi, 0, h)),
                  pl.BlockSpec((1, lp, NOPE_D + V_D), lambda bi, h: (bi, 0, h)),
                  pl.BlockSpec((1, lp, LANES), lambda bi, h: (bi, 0, 0))],
        out_specs=pl.BlockSpec((1, lp, V_D), lambda bi, h: (bi, 0, h)),
        scratch_shapes=[pltpu.VMEM((lp, hw), BF16), pltpu.VMEM((lp, V_D + LANES), BF16)],
        compiler_params=_cparams(("parallel", "parallel")),
        name="mla_attention",
    )(q.reshape(b, lp, hq * hw), kv.reshape(b, lp, hq * (NOPE_D + V_D)), kpe.reshape(b, lp, LANES))

    h2 = _proj_residual(o.reshape(m, hq * V_D), p["mla_w_o"], j, h3.reshape(m, d), p["norm_mix_post"])
    return h2.reshape(b, lp, d)


POOL_HALO = 16


def _pool_kernel(h_ref, halo_ref, gpre_ref, w_ref, sc_ref, gpost_ref, o_ref):
    i = pl.program_id(1)
    tl, d = h_ref.shape[1], h_ref.shape[2]
    gw = d // len(POOL_WINDOWS)
    h = h_ref[0]
    hn = _rms(h, gpre_ref[...])
    hist = jnp.where(i == 0, 0.0, _rms(halo_ref[0], gpre_ref[...]))
    ext = jnp.concatenate([hist, hn], axis=0)
    t = i * tl + lax.broadcasted_iota(jnp.int32, (tl, gw), 0)
    ys = []
    for gi, win in enumerate(POOL_WINDOWS):
        cols = slice(gi * gw, (gi + 1) * gw)
        s = ext[:, cols]
        step = 1
        while step < win:
            s = s + pltpu.roll(s, step, axis=0)
            step *= 2
        cnt = jnp.minimum(t + 1, win).astype(F32)
        pooled = (s[POOL_HALO:] / cnt - hn[:, cols]).astype(BF16)
        ys.append(_dot(pooled, w_ref[gi]) * sc_ref[:, cols])
    o_ref[0] = h + _rms(jnp.concatenate(ys, axis=1), gpost_ref[...])


def _pool_layer(h3, j, p):
    b, lp, d = h3.shape
    assert max(POOL_WINDOWS) <= POOL_HALO
    tl = lp // 4 if lp % (4 * POOL_HALO) == 0 else lp
    halo_blocks = tl // POOL_HALO

    def full(a):
        return pl.BlockSpec(a.shape, lambda bi, i, nd=a.ndim: (0,) * nd)

    consts = [p["norm_mix_pre"].reshape(1, d), p["pool_w"][j].astype(BF16), p["pool_scale"][j].reshape(1, d),
              p["norm_mix_post"].reshape(1, d)]
    return pl.pallas_call(
        _pool_kernel,
        out_shape=jax.ShapeDtypeStruct((b, lp, d), F32),
        grid=(b, lp // tl),
        in_specs=[pl.BlockSpec((1, tl, d), lambda bi, i: (bi, i, 0)),
                  pl.BlockSpec((1, POOL_HALO, d), lambda bi, i: (bi, jnp.maximum(i * halo_blocks - 1, 0), 0))]
                 + [full(a) for a in consts],
        out_specs=pl.BlockSpec((1, tl, d), lambda bi, i: (bi, i, 0)),
        compiler_params=_cparams(("parallel", "parallel")),
        name="pool_layer",
    )(h3, h3, *consts)


def kernel(x, meta_tokens, norm_mix_pre, norm_mix_post, norm_ffn_pre, norm_ffn_post, ffn_w_gate, ffn_w_up, ffn_w_down, rwkv_mix, rwkv_w0, rwkv_w1, rwkv_w2, rwkv_a0, rwkv_a1, rwkv_a2, rwkv_v0, rwkv_v1, rwkv_v2, rwkv_g1, rwkv_g2, rwkv_k_k, rwkv_k_a, rwkv_r_k, rwkv_w_rkv, rwkv_w_o, rwkv_lnx_g, rwkv_lnx_b, mla_w_in, mla_q_norm, mla_w_uq, mla_kv_norm, mla_w_ukv, mla_w_o, pool_w, pool_scale):
    b, seq, d = x.shape
    depth = norm_mix_pre.shape[0]
    ltot = N_META + seq
    lp = -(-ltot // SEQ_ALIGN) * SEQ_ALIGN
    meta = jnp.broadcast_to(meta_tokens.astype(x.dtype)[None], (b, N_META, d))
    h = jnp.concatenate([meta, x, jnp.zeros((b, lp - ltot, d), x.dtype)], axis=1)
    shared = dict(rwkv_mix=rwkv_mix, rwkv_w0=rwkv_w0, rwkv_w1=rwkv_w1, rwkv_w2=rwkv_w2, rwkv_a0=rwkv_a0,
                  rwkv_a1=rwkv_a1, rwkv_a2=rwkv_a2, rwkv_v0=rwkv_v0, rwkv_v1=rwkv_v1, rwkv_v2=rwkv_v2,
                  rwkv_g1=rwkv_g1, rwkv_g2=rwkv_g2, rwkv_k_k=rwkv_k_k, rwkv_k_a=rwkv_k_a, rwkv_r_k=rwkv_r_k,
                  rwkv_w_rkv=rwkv_w_rkv.astype(BF16), rwkv_w_o=rwkv_w_o.astype(BF16),
                  rwkv_lnx_g=rwkv_lnx_g, rwkv_lnx_b=rwkv_lnx_b,
                  mla_w_in=mla_w_in, mla_q_norm=mla_q_norm, mla_w_uq=mla_w_uq, mla_kv_norm=mla_kv_norm,
                  mla_w_ukv=mla_w_ukv, mla_w_o=mla_w_o.astype(BF16), pool_w=pool_w, pool_scale=pool_scale)
    w_gate, w_up, w_down = ffn_w_gate.astype(BF16), ffn_w_up.astype(BF16), ffn_w_down.astype(BF16)
    v_first = None
    for i in range(depth):
        p = dict(shared, norm_mix_pre=norm_mix_pre[i], norm_mix_post=norm_mix_post[i])
        kind, j = i % 3, i // 3
        if kind == 0:
            h, v_first = _rwkv_layer(h, j, v_first, p)
        elif kind == 1:
            h = _mla_layer(h, j, p)
        else:
            h = _pool_layer(h, j, p)
        h = _ffn(h.reshape(b * lp, d), norm_ffn_pre[i], norm_ffn_post[i], w_gate, w_up, w_down, i).reshape(b, lp, d)
    return h[:, N_META:ltot]
```

```python
import functools

import jax
import jax.numpy as jnp
from jax import lax
from jax.experimental import pallas as pl
from jax.experimental.pallas import tpu as pltpu

F32 = jnp.float32
BF16 = jnp.bfloat16

N_META = 16
RMS_EPS = 1e-6
RWKV_HEAD = 64
LNX_EPS = 1e-5 * RWKV_HEAD
MLA_HEADS = 16
Q_LORA = 512
KV_LORA = 512
NOPE_D = 128
ROPE_D = 64
V_D = 128
ROPE_THETA = 10000.0
POOL_WINDOWS = (2, 4, 8, 16)

LANES = 128
WKV_CHUNK = 64
SEQ_ALIGN = WKV_CHUNK
WKV_HEADS_PER_GROUP = 4
WKV_GROUP = WKV_HEADS_PER_GROUP * RWKV_HEAD
WKV_GROUPS_PER_STEP = 8
WKV_CHUNKS_PER_STEP = 3
ATTN_Q_TILE = 256
VMEM_LIMIT = 56 * 1024 * 1024


def _cparams(sem):
    return pltpu.CompilerParams(dimension_semantics=sem, vmem_limit_bytes=VMEM_LIMIT)


def _dot(a, b):
    return jnp.dot(a, b, preferred_element_type=F32)


def _dot_nt(a, b):
    return lax.dot_general(a, b, (((1,), (1,)), ((), ())), preferred_element_type=F32)


def _dot_tn(a, b):
    return lax.dot_general(a, b, (((0,), (0,)), ((), ())), preferred_element_type=F32)


def _rms(x, g, eps=RMS_EPS):
    ms = jnp.mean(x * x, axis=-1, keepdims=True)
    return x * lax.rsqrt(ms + eps) * g


def _sigmoid(x):
    return 1.0 / (1.0 + jnp.exp(-x))


def _row_tile(m, cap=768):
    for t in (768, 512, 384, 256, 128, 64, 32, 16):
        if t <= cap and m % t == 0:
            return t
    raise ValueError(f"unsupported row count {m}")


def _proj_residual_kernel(x_ref, w_ref, h_ref, g_ref, o_ref):
    half = o_ref.shape[0] // 2
    for r0 in (0, half):
        rs = slice(r0, r0 + half)
        o_ref[rs, :] = h_ref[rs, :] + _rms(_dot(x_ref[rs, :], w_ref[...]), g_ref[...])


def _proj_residual(x2, w_all, layer, h2, g):
    m, k = x2.shape
    d = w_all.shape[2]
    bm = _row_tile(m)
    return pl.pallas_call(
        _proj_residual_kernel,
        out_shape=jax.ShapeDtypeStruct((m, d), F32),
        grid=(m // bm,),
        in_specs=[pl.BlockSpec((bm, k), lambda i: (i, 0)),
                  pl.BlockSpec((None, k, d), lambda i: (layer, 0, 0)),
                  pl.BlockSpec((bm, d), lambda i: (i, 0)),
                  pl.BlockSpec((1, d), lambda i: (0, 0))],
        out_specs=pl.BlockSpec((bm, d), lambda i: (i, 0)),
        compiler_params=_cparams(("parallel",)),
        name="proj_residual",
    )(x2, w_all, h2, g.reshape(1, d))


FFN_HIDDEN_TILE = 512
FFN_OUT_TILE = 512
FFN_ROW_SPLIT = 2


def _ffn_kernel(h_ref, gpre_ref, gpost_ref, wg_ref, wu_ref, wd_ref, o_ref, xn_ref):
    f = pl.program_id(1)
    last_f = pl.num_programs(1) - 1
    bm, d = o_ref.shape
    rows = bm // FFN_ROW_SPLIT

    def step(first, last):
        for r0 in range(0, bm, rows):
            rs = slice(r0, r0 + rows)
            if first:
                xn_ref[rs, :] = _rms(h_ref[rs, :], gpre_ref[...]).astype(BF16)
            x = xn_ref[rs, :]
            gate = _dot(x, wg_ref[...])
            up = _dot(x, wu_ref[...])
            mid = (gate * _sigmoid(gate) * up).astype(BF16)
            for n0 in range(0, d, FFN_OUT_TILE):
                ns = slice(n0, n0 + FFN_OUT_TILE)
                part = _dot(mid, wd_ref[:, ns])
                if first:
                    o_ref[rs, ns] = part
                else:
                    o_ref[rs, ns] += part
            if last:
                o_ref[rs, :] = h_ref[rs, :] + _rms(o_ref[rs, :], gpost_ref[...])

    pl.when(f == 0)(lambda: step(True, False))
    pl.when(jnp.logical_and(f > 0, f < last_f))(lambda: step(False, False))
    pl.when(f == last_f)(lambda: step(False, True))


def _ffn(h2, gpre, gpost, wg, wu, wd, layer):
    m, d = h2.shape
    f = wg.shape[2]
    bm = _row_tile(m)
    bf = FFN_HIDDEN_TILE if f % FFN_HIDDEN_TILE == 0 else f
    assert d % FFN_OUT_TILE == 0 and bm % (16 * FFN_ROW_SPLIT) == 0
    return pl.pallas_call(
        _ffn_kernel,
        out_shape=jax.ShapeDtypeStruct((m, d), F32),
        grid=(m // bm, f // bf),
        in_specs=[pl.BlockSpec((bm, d), lambda i, j: (i, 0)),
                  pl.BlockSpec((1, d), lambda i, j: (0, 0)),
                  pl.BlockSpec((1, d), lambda i, j: (0, 0)),
                  pl.BlockSpec((None, d, bf), lambda i, j: (layer, 0, j)),
                  pl.BlockSpec((None, d, bf), lambda i, j: (layer, 0, j)),
                  pl.BlockSpec((None, bf, d), lambda i, j: (layer, j, 0))],
        out_specs=pl.BlockSpec((bm, d), lambda i, j: (i, 0)),
        scratch_shapes=[pltpu.VMEM((bm, d), BF16)],
        compiler_params=_cparams(("parallel", "arbitrary")),
        name="ffn",
    )(h2, gpre.reshape(1, d), gpost.reshape(1, d), wg, wu, wd)


def _ffn_window(h3, start, count, gpre, gpost, wg, wu, wd, layer):
    b, _, d = h3.shape
    f = wg.shape[2]
    bm = next(t for t in (1024, 512, 256, 128, 64, 32) if count % t == 0)
    bf = FFN_HIDDEN_TILE if f % FFN_HIDDEN_TILE == 0 else f
    assert d % FFN_OUT_TILE == 0 and bm % (16 * FFN_ROW_SPLIT) == 0 and start % 8 == 0
    def window_kernel(h_ref, *rest):
        _ffn_kernel(h_ref.at[0], *rest)

    return pl.pallas_call(
        window_kernel,
        out_shape=jax.ShapeDtypeStruct((b, count, d), F32),
        grid=(b * (count // bm), f // bf),
        in_specs=[pl.BlockSpec((pl.Element(1), pl.Element(bm), pl.Element(d)),
                               lambda i, j: (i // (count // bm),
                                             pl.multiple_of(start + (i % (count // bm)) * bm, 8), 0)),
                  pl.BlockSpec((1, d), lambda i, j: (0, 0)),
                  pl.BlockSpec((1, d), lambda i, j: (0, 0)),
                  pl.BlockSpec((None, d, bf), lambda i, j: (layer, 0, j)),
                  pl.BlockSpec((None, d, bf), lambda i, j: (layer, 0, j)),
                  pl.BlockSpec((None, bf, d), lambda i, j: (layer, j, 0))],
        out_specs=pl.BlockSpec((None, bm, d), lambda i, j: (i // (count // bm), i % (count // bm), 0)),
        scratch_shapes=[pltpu.VMEM((bm, d), BF16)],
        compiler_params=_cparams(("parallel", "arbitrary")),
        name="ffn_window",
    )(h3, gpre.reshape(1, d), gpost.reshape(1, d), wg, wu, wd)


def _rkv_kernel(hn_ref, xx_ref, mix_ref, w_ref, o_ref):
    xs = (hn_ref[...].astype(F32) + xx_ref[...].astype(F32) * mix_ref[0]).astype(BF16)
    o_ref[...] = _dot(xs, w_ref[...]).astype(BF16)


def _rkv_proj(hn2, xx2, mix3, w_rkv_all, layer):
    m, d = hn2.shape
    bm = _row_tile(m)
    return pl.pallas_call(
        _rkv_kernel,
        out_shape=jax.ShapeDtypeStruct((m, 3 * d), BF16),
        grid=(m // bm, 3),
        in_specs=[pl.BlockSpec((bm, d), lambda i, s: (i, 0)),
                  pl.BlockSpec((bm, d), lambda i, s: (i, 0)),
                  pl.BlockSpec((1, 1, d), lambda i, s: (s, 0, 0)),
                  pl.BlockSpec((None, None, d, d), lambda i, s: (layer, s, 0, 0))],
        out_specs=pl.BlockSpec((bm, d), lambda i, s: (i, s)),
        compiler_params=_cparams(("parallel", "parallel")),
        name="rkv_proj",
    )(hn2, xx2, mix3, w_rkv_all)


LORA_SMALL_TILE = 256


def _shift_lora_kernel(*refs, has_vres, w_rank):
    if has_vres:
        (h_ref, halo_ref, gpre_ref, wf_ref, w0_ref, w2_ref, a0_ref, a2_ref, g2_ref, v0_ref, v2_ref,
         hn_ref, xx_ref, ld_ref, a_ref, g_ref, vg_ref) = refs
    else:
        (h_ref, halo_ref, gpre_ref, wf_ref, w0_ref, w2_ref, a0_ref, a2_ref, g2_ref,
         hn_ref, xx_ref, ld_ref, a_ref, g_ref) = refs
    gpre = gpre_ref[...]
    hn = _rms(h_ref[0], gpre)
    prev = jnp.where(pl.program_id(1) == 0, 0.0, _rms(halo_ref[0], gpre)[7:8])
    row = lax.broadcasted_iota(jnp.int32, hn.shape, 0)
    shifted = jnp.where(row == 0, prev, pltpu.roll(hn, 1, axis=0))
    hn_b = hn.astype(BF16)
    xx_b = (shifted - hn).astype(BF16)
    hn_ref[0] = hn_b
    xx_ref[0] = xx_b
    t = _dot(jnp.concatenate([hn_b, xx_b], axis=1), wf_ref[...])
    small = t[:, :LORA_SMALL_TILE]
    lane = lax.broadcasted_iota(jnp.int32, (1, LORA_SMALL_TILE), 1)
    act = jnp.where(lane < w_rank, jnp.tanh(small), small).astype(BF16)
    z = w0_ref[...] + _dot(act, w2_ref[...])
    ld_ref[0] = -jnp.exp(F32(-0.5)) * _sigmoid(z)
    a_ref[0] = _sigmoid(a0_ref[...] + _dot(act, a2_ref[...])).astype(BF16)
    g_ref[0] = _dot(_sigmoid(t[:, LORA_SMALL_TILE:]).astype(BF16), g2_ref[...]).astype(BF16)
    if has_vres:
        vg_ref[0] = _sigmoid(v0_ref[...] + _dot(act, v2_ref[...])).astype(BF16)


def _shift_lora(h3, gpre, mix, w0, w1, w2, a0, a1, a2, g1, g2, vres):
    b, lp, d = h3.shape
    tl = next(t for t in range(min(lp, 384), 0, -16) if lp % t == 0)
    halo_blocks = tl // 8
    has_vres = vres is not None
    downs = [(w1, mix[0]), (a1, mix[1])] + ([(vres[1], mix[3])] if has_vres else [])
    ups = [w2, a2] + ([vres[2]] if has_vres else [])
    ranks = [w.shape[1] for w, _ in downs]
    assert sum(ranks) <= LORA_SMALL_TILE
    pad = LORA_SMALL_TILE - sum(ranks)
    plain = jnp.concatenate([w for w, _ in downs] + [jnp.zeros((d, pad), F32), g1], axis=1)
    mixed = jnp.concatenate([w * mx[:, None] for w, mx in downs] + [jnp.zeros((d, pad), F32), g1 * mix[2][:, None]],
                            axis=1)
    w_first = jnp.concatenate([plain, mixed], axis=0).astype(BF16)

    def placed(up, offset):
        return jnp.pad(up, ((offset, LORA_SMALL_TILE - offset - up.shape[0]), (0, 0))).astype(BF16)

    offsets = [sum(ranks[:i]) for i in range(len(ranks))]
    ups = [placed(u, o) for u, o in zip(ups, offsets)]
    consts = [gpre.reshape(1, d), w_first, w0.reshape(1, d), ups[0], a0.reshape(1, d), ups[1], g2.astype(BF16)]
    if has_vres:
        consts += [vres[0].reshape(1, d), ups[2]]
    row = pl.BlockSpec((1, tl, d), lambda bi, i: (bi, i, 0))
    halo = pl.BlockSpec((1, 8, d), lambda bi, i: (bi, jnp.maximum(i * halo_blocks - 1, 0), 0))

    def full(a):
        return pl.BlockSpec(a.shape, lambda bi, i: (0, 0))

    dtypes = [BF16, BF16, F32, BF16, BF16] + ([BF16] if has_vres else [])
    return pl.pallas_call(
        functools.partial(_shift_lora_kernel, has_vres=has_vres, w_rank=ranks[0]),
        out_shape=tuple(jax.ShapeDtypeStruct((b, lp, d), t) for t in dtypes),
        grid=(b, lp // tl),
        in_specs=[row, halo] + [full(a) for a in consts],
        out_specs=tuple(row for _ in dtypes),
        compiler_params=_cparams(("parallel", "parallel")),
        name="shift_lora",
    )(h3, h3, *consts)


def _split_dot(a_bf16, x):
    hi = x.astype(BF16)
    lo = (x - hi.astype(F32)).astype(BF16)
    return _dot(a_bf16, hi) + _dot(a_bf16, lo)


def _split_dot_rhs(x, w_bf16):
    hi = x.astype(BF16)
    lo = (x - hi.astype(F32)).astype(BF16)
    return _dot(hi, w_bf16) + _dot(lo, w_bf16)


def _wkv_kernel(*refs, n_chunks, n_groups, has_vres):
    if has_vres:
        (r_ref, k_ref, v_ref, ld_ref, a_ref, g_ref, vf_ref, vg_ref,
         kk_ref, ka_ref, rk_ref, lg_ref, lb_ref, mbd_ref, msk_ref, tri_ref, o_ref, s_ref) = refs
    else:
        (r_ref, k_ref, v_ref, ld_ref, a_ref, g_ref,
         kk_ref, ka_ref, rk_ref, lg_ref, lb_ref, mbd_ref, msk_ref, tri_ref, o_ref, s_ref) = refs
    C, G, HG, N = WKV_CHUNK, WKV_GROUP, WKV_HEADS_PER_GROUP, RWKV_HEAD

    @pl.when(pl.program_id(2) == 0)
    def _():
        s_ref[...] = jnp.zeros_like(s_ref)

    def bd(x):
        xb = x.astype(BF16)
        return jnp.concatenate([xb] * HG, axis=0) * mbd_ref[...]

    gs = range(n_groups)
    lanes = [slice(gi * G, (gi + 1) * G) for gi in gs]

    def group_rows(x, n):
        return [x[i * n:(i + 1) * n] for i in gs]

    def chunk(j, carry):
        rows = pl.ds(pl.multiple_of(j * C, C), C)
        strict, incl, eye = msk_ref[0], msk_ref[1], msk_ref[2]
        r = [r_ref[0, rows, lanes[i]].astype(F32) for i in gs]
        k = [k_ref[0, rows, lanes[i]].astype(F32) for i in gs]
        v = [v_ref[0, rows, lanes[i]].astype(F32) for i in gs]
        a_s = [a_ref[0, rows, lanes[i]].astype(F32) for i in gs]
        if has_vres:
            v = [v[i] + (vf_ref[0, rows, lanes[i]].astype(F32) - v[i]) * vg_ref[0, rows, lanes[i]].astype(F32)
                 for i in gs]
        kk = [k[i] * kk_ref[:, lanes[i]] for i in gs]
        kp = [k[i] * (1.0 + (a_s[i] - 1.0) * ka_ref[:, lanes[i]]) for i in gs]
        lw = [_split_dot(tri_ref[...], ld_ref[0, rows, lanes[i]]) for i in gs]
        sums = group_rows(_dot(jnp.concatenate(
            [jnp.concatenate([kk[i] * kk[i], r[i] * kp[i] * rk_ref[:, lanes[i]]], axis=0).astype(BF16) for i in gs],
            axis=0), mbd_ref[...]), 2 * C)
        kk = [kk[i] / jnp.maximum(jnp.sqrt(sums[i][:C]), 1e-12) for i in gs]
        bb = [kk[i] * a_s[i] for i in gs]
        lw_last = [lw[i][C - 1:C, :] for i in gs]
        w_inv = [jnp.exp(-lw[i]) for i in gs]
        w_tail = [jnp.exp(lw_last[i] - lw[i]) for i in gs]
        a_t = [(-kk[i] * jnp.exp(lw[i] - ld_ref[0, rows, lanes[i]])).astype(BF16) for i in gs]
        r_t = [(r[i] * jnp.exp(lw[i])).astype(BF16) for i in gs]

        sc = [_dot_nt(jnp.concatenate([a_t[i], r_t[i]], axis=0),
                      jnp.concatenate([bd(bb[i] * w_inv[i]), bd(kp[i] * w_inv[i])], axis=0)) for i in gs]
        l_ab = [sc[i][:C, :G] * strict for i in gs]
        l_rb = [(sc[i][C:, :G] * incl).astype(BF16) for i in gs]
        l_k = [jnp.concatenate([sc[i][:C, G:] * strict, sc[i][C:, G:] * incl], axis=0).astype(BF16) for i in gs]
        v_bd = [bd(v[i]) for i in gs]
        lkv = [_dot(l_k[i], v_bd[i]) for i in gs]

        p = [l_ab[i].astype(BF16) for i in gs]
        t = [eye + l_ab[i] for i in gs]
        power = 1
        while power < C:
            if power == 1:
                p = [_dot(p[i], bd(p[i])).astype(BF16) for i in gs]
            elif power * 2 < C:
                res = [_dot(jnp.concatenate([p[i], t[i].astype(BF16)], axis=0), bd(p[i])) for i in gs]
                p = [res[i][:C].astype(BF16) for i in gs]
                t = [t[i] + res[i][C:] for i in gs]
            else:
                t = [t[i] + _dot(t[i].astype(BF16), bd(p[i])) for i in gs]
            power *= 2

        ta = [_dot(t[i].astype(BF16), jnp.concatenate([bd(a_t[i]), bd(lkv[i][:C])], axis=1)) for i in gs]
        s0 = [s_ref[i] for i in gs]
        res = [_dot_nt(jnp.concatenate([ta[i][:, :G].astype(BF16), r_t[i]], axis=0), s0[i].astype(BF16)) for i in gs]
        u_bd = [bd(res[i][:C] + ta[i][:, G:]) for i in gs]
        y = [res[i][C:] + lkv[i][C:] + _dot(l_rb[i], u_bd[i]) for i in gs]
        uv_stack = [jnp.concatenate([u_bd[i], v_bd[i]], axis=0) for i in gs]
        for i in gs:
            s_ref[i] = s0[i] * jnp.exp(lw_last[i]) + _dot_tn(
                uv_stack[i], jnp.concatenate([bd(bb[i] * w_tail[i]), bd(kp[i] * w_tail[i])], axis=0))

        y_hi = [y[i].astype(BF16) for i in gs]
        y_lo = [(y[i] - y_hi[i].astype(F32)).astype(BF16) for i in gs]
        mu2 = group_rows(_dot(jnp.concatenate([jnp.concatenate([y_hi[i], y_lo[i]], axis=0) for i in gs], axis=0),
                              mbd_ref[...]), 2 * C)
        dy = [y[i] - (mu2[i][:C] + mu2[i][C:]) * (1.0 / N) for i in gs]
        var = group_rows(_dot(jnp.concatenate([(dy[i] * dy[i]).astype(BF16) for i in gs], axis=0), mbd_ref[...]), C)
        var = [var[i] * (1.0 / N) for i in gs]
        for i in gs:
            yn = dy[i] * lax.rsqrt(var[i] + LNX_EPS) * lg_ref[:, lanes[i]] + lb_ref[:, lanes[i]]
            o_ref[0, rows, lanes[i]] = ((yn + sums[i][C:] * v[i]) * g_ref[0, rows, lanes[i]].astype(F32)).astype(BF16)
        return carry

    lax.fori_loop(0, n_chunks, chunk, 0, unroll=True)


def _wkv_tables():
    C, G, N = WKV_CHUNK, WKV_GROUP, RWKV_HEAD
    idx = jnp.arange(G)
    mbd = (idx[:, None] // N == idx[None, :] // N).astype(BF16)
    t = jnp.arange(C)[:, None]
    s = (idx % N)[None, :]
    masks = jnp.stack([s < t, s <= t, s == t]).astype(F32)
    tri = (jnp.arange(C)[None, :] <= t).astype(BF16)
    return mbd, masks, tri


def _wkv(rkv, ld, a, g, rkv_first, vg, k_k, k_a, r_k, lnx_g, lnx_b):
    b, lp, d = ld.shape
    n_groups = next(n for n in (WKV_GROUPS_PER_STEP, 4, 2, 1) if d % (n * WKV_GROUP) == 0)
    width = n_groups * WKV_GROUP
    n_lane_blocks = d // width
    n_chunks = next(n for n in (WKV_CHUNKS_PER_STEP, 2, 1) if (lp // WKV_CHUNK) % n == 0)
    tl = n_chunks * WKV_CHUNK
    has_vres = rkv_first is not None

    def seq(col):
        return pl.BlockSpec((1, tl, width), lambda bi, hg, c: (bi, c, hg + col * n_lane_blocks))

    par = pl.BlockSpec((1, width), lambda bi, hg, c: (0, hg))
    args = [rkv, rkv, rkv, ld, a, g] + ([rkv_first, vg] if has_vres else [])
    specs = [seq(0), seq(1), seq(2), seq(0), seq(0), seq(0)] + ([seq(2), seq(0)] if has_vres else [])
    params = [p.reshape(1, d) for p in (k_k, k_a, r_k, lnx_g, lnx_b)]
    tables = _wkv_tables()
    table_specs = [pl.BlockSpec(t.shape, lambda bi, hg, c, nd=t.ndim: (0,) * nd) for t in tables]
    return pl.pallas_call(
        functools.partial(_wkv_kernel, n_chunks=n_chunks, n_groups=n_groups, has_vres=has_vres),
        out_shape=jax.ShapeDtypeStruct((b, lp, d), BF16),
        grid=(b, n_lane_blocks, lp // tl),
        in_specs=specs + [par] * len(params) + table_specs,
        out_specs=seq(0),
        scratch_shapes=[pltpu.VMEM((n_groups, WKV_GROUP, WKV_GROUP), F32)],
        compiler_params=_cparams(("parallel", "parallel", "arbitrary")),
        name="wkv7",
    )(*args, *params, *tables)


def _rwkv_layer(h3, j, rkv_first, p):
    b, lp, d = h3.shape
    m = b * lp
    mix = p["rwkv_mix"][j]
    vres = None if j == 0 else (p["rwkv_v0"][j - 1], p["rwkv_v1"][j - 1], p["rwkv_v2"][j - 1])
    outs = _shift_lora(h3, p["norm_mix_pre"], jnp.stack([mix[1], mix[4], mix[5], mix[3]]), p["rwkv_w0"][j],
                       p["rwkv_w1"][j], p["rwkv_w2"][j], p["rwkv_a0"][j], p["rwkv_a1"][j], p["rwkv_a2"][j],
                       p["rwkv_g1"][j], p["rwkv_g2"][j], vres)
    hn, xx, ld, a, g = outs[:5]
    vg = outs[5] if vres is not None else None
    rkv = _rkv_proj(hn.reshape(m, d), xx.reshape(m, d), jnp.stack([mix[0], mix[2], mix[3]]).reshape(3, 1, d),
                    p["rwkv_w_rkv"], j).reshape(b, lp, 3 * d)
    if vres is None:
        rkv_first = rkv
    yg = _wkv(rkv, ld, a, g, rkv_first if vres is not None else None, vg,
              p["rwkv_k_k"][j], p["rwkv_k_a"][j], p["rwkv_r_k"][j].reshape(d),
              p["rwkv_lnx_g"][j], p["rwkv_lnx_b"][j])
    h2 = _proj_residual(yg.reshape(m, d), p["rwkv_w_o"], j, h3.reshape(m, d), p["norm_mix_post"])
    return h2.reshape(b, lp, d), rkv_first


def _mla_proj_kernel(h_ref, g_ref, win_ref, qn_ref, wuq_ref, kvn_ref, wukv_ref, cos_ref, sin_ref,
                     q_ref, kv_ref, kpe_ref, *, scale):
    hn = _rms(h_ref[...], g_ref[...]).astype(BF16)
    lat = _dot(hn, win_ref[...])
    cq = _rms(lat[:, :Q_LORA], qn_ref[...]).astype(BF16)
    ckv = _rms(lat[:, Q_LORA:Q_LORA + KV_LORA], kvn_ref[...]).astype(BF16)
    cos, sin = cos_ref[...], sin_ref[...]

    def rope(x):
        return x * cos + pltpu.roll(x, ROPE_D // 2, axis=1) * sin

    keep = (lax.broadcasted_iota(jnp.int32, (1, LANES), 1) < ROPE_D).astype(F32)
    kpe_ref[...] = (rope(lat[:, Q_LORA + KV_LORA:]) * keep).astype(BF16)
    kv_ref[...] = _dot(ckv, wukv_ref[...]).astype(BF16)
    q = _dot(cq, wuq_ref[...])
    hw = NOPE_D + LANES
    for h in range(MLA_HEADS):
        q_ref[:, h * hw:h * hw + NOPE_D] = (q[:, h * hw:h * hw + NOPE_D] * scale).astype(BF16)
        q_ref[:, h * hw + NOPE_D:(h + 1) * hw] = (rope(q[:, h * hw + NOPE_D:(h + 1) * hw]) * scale).astype(BF16)


def _mla_attn_kernel(q_ref, kv_ref, kpe_ref, o_ref, kcat_ref, vone_ref):
    lp = q_ref.shape[1]
    kcat_ref[:, :NOPE_D] = kv_ref[0, :, :NOPE_D]
    kcat_ref[:, NOPE_D:] = kpe_ref[0]
    vone_ref[:, :V_D] = kv_ref[0, :, NOPE_D:]
    vone_ref[:, V_D:] = jnp.ones((lp, LANES), BF16)
    tiles = [(q0, min(q0 + ATTN_Q_TILE, lp)) for q0 in range(0, lp, ATTN_Q_TILE)]

    def scores(q0, q1):
        q = q_ref[0, q0:q1, :]
        sd = _dot_nt(q, kcat_ref[q0:q1, :])
        sp = _dot_nt(q, kcat_ref[:q0, :]) if q0 > 0 else None
        return sd, sp

    def finish(q0, q1, sd, sp):
        row = lax.broadcasted_iota(jnp.int32, sd.shape, 0)
        col = lax.broadcasted_iota(jnp.int32, sd.shape, 1)
        sd = jnp.where(col <= row, sd, -1e30)
        mx = jnp.max(sd, axis=-1, keepdims=True)
        if sp is not None:
            mx = jnp.maximum(mx, jnp.max(sp, axis=-1, keepdims=True))
        acc = _dot(jnp.exp(sd - mx).astype(BF16), vone_ref[q0:q1, :])
        if sp is not None:
            acc = acc + _dot(jnp.exp(sp - mx).astype(BF16), vone_ref[:q0, :])
        o_ref[0, q0:q1, :] = (acc[:, :V_D] / acc[:, V_D:V_D + 1]).astype(BF16)

    pending = scores(*tiles[0])
    for idx, (q0, q1) in enumerate(tiles):
        upcoming = scores(*tiles[idx + 1]) if idx + 1 < len(tiles) else None
        finish(q0, q1, *pending)
        pending = upcoming


def _rope_tables(b, lp):
    half = ROPE_D // 2
    inv_freq = 1.0 / (ROPE_THETA ** (jnp.arange(0, ROPE_D, 2, dtype=F32) / ROPE_D))
    ang = jnp.arange(lp, dtype=jnp.int32).astype(F32)[:, None] * inv_freq[None, :]
    cos, sin = jnp.cos(ang), jnp.sin(ang)
    cos4 = jnp.tile(cos, (b, LANES // half))
    sin4 = jnp.tile(jnp.concatenate([-sin, sin], axis=1), (b, LANES // ROPE_D))
    return cos4, sin4


def _mla_layer(h3, j, p):
    b, lp, d = h3.shape
    m = b * lp
    hq = MLA_HEADS
    hw = NOPE_D + LANES
    bm = _row_tile(m, 384)
    w_in = p["mla_w_in"][j]
    w_in = jnp.concatenate([w_in, w_in[:, Q_LORA + KV_LORA:]], axis=1).astype(BF16)
    w_uq = p["mla_w_uq"][j].reshape(Q_LORA, hq, NOPE_D + ROPE_D)
    w_uq = jnp.concatenate([w_uq, w_uq[:, :, NOPE_D:]], axis=2).reshape(Q_LORA, hq * hw).astype(BF16)
    w_ukv = p["mla_w_ukv"][j].astype(BF16)
    cos4, sin4 = _rope_tables(b, lp)
    scale = float((NOPE_D + ROPE_D) ** -0.5)

    def full(a):
        return pl.BlockSpec(a.shape, lambda i: (0, 0))

    consts = [p["norm_mix_pre"].reshape(1, d), w_in, p["mla_q_norm"][j].reshape(1, Q_LORA), w_uq,
              p["mla_kv_norm"][j].reshape(1, KV_LORA), w_ukv]
    q, kv, kpe = pl.pallas_call(
        functools.partial(_mla_proj_kernel, scale=scale),
        out_shape=(jax.ShapeDtypeStruct((m, hq * hw), BF16),
                   jax.ShapeDtypeStruct((m, hq * (NOPE_D + V_D)), BF16),
                   jax.ShapeDtypeStruct((m, LANES), BF16)),
        grid=(m // bm,),
        in_specs=[pl.BlockSpec((bm, d), lambda i: (i, 0))] + [full(a) for a in consts]
                 + [pl.BlockSpec((bm, LANES), lambda i: (i, 0))] * 2,
        out_specs=(pl.BlockSpec((bm, hq * hw), lambda i: (i, 0)),
                   pl.BlockSpec((bm, hq * (NOPE_D + V_D)), lambda i: (i, 0)),
                   pl.BlockSpec((bm, LANES), lambda i: (i, 0))),
        compiler_params=_cparams(("parallel",)),
        name="mla_proj",
    )(h3.reshape(m, d), *consts, cos4, sin4)

    o = pl.pallas_call(
        _mla_attn_kernel,
        out_shape=jax.ShapeDtypeStruct((b, lp, hq * V_D), BF16),
        grid=(b, hq),
        in_specs=[pl.BlockSpec((1, lp, hw), lambda bi, h: (bi, 0, h)),
                  pl.BlockSpec((1, lp, NOPE_D + V_D), lambda bi, h: (bi, 0, h)),
                  pl.BlockSpec((1, lp, LANES), lambda bi, h: (bi, 0, 0))],
        out_specs=pl.BlockSpec((1, lp, V_D), lambda bi, h: (bi, 0, h)),
        scratch_shapes=[pltpu.VMEM((lp, hw), BF16), pltpu.VMEM((lp, V_D + LANES), BF16)],
        compiler_params=_cparams(("parallel", "parallel")),
        name="mla_attention",
    )(q.reshape(b, lp, hq * hw), kv.reshape(b, lp, hq * (NOPE_D + V_D)), kpe.reshape(b, lp, LANES))

    h2 = _proj_residual(o.reshape(m, hq * V_D), p["mla_w_o"], j, h3.reshape(m, d), p["norm_mix_post"])
    return h2.reshape(b, lp, d)


POOL_HALO = 16


def _pool_kernel(h_ref, halo_ref, gpre_ref, w_ref, sc_ref, gpost_ref, o_ref):
    i = pl.program_id(1)
    tl, d = h_ref.shape[1], h_ref.shape[2]
    gw = d // len(POOL_WINDOWS)
    h = h_ref[0]
    hn = _rms(h, gpre_ref[...])
    hist = jnp.where(i == 0, 0.0, _rms(halo_ref[0], gpre_ref[...]))
    ext = jnp.concatenate([hist, hn], axis=0)
    t = i * tl + lax.broadcasted_iota(jnp.int32, (tl, gw), 0)
    ys = []
    for gi, win in enumerate(POOL_WINDOWS):
        cols = slice(gi * gw, (gi + 1) * gw)
        s = ext[:, cols]
        step = 1
        while step < win:
            s = s + pltpu.roll(s, step, axis=0)
            step *= 2
        cnt = jnp.minimum(t + 1, win).astype(F32)
        pooled = (s[POOL_HALO:] / cnt - hn[:, cols]).astype(BF16)
        ys.append(_dot(pooled, w_ref[gi]) * sc_ref[:, cols])
    o_ref[0] = h + _rms(jnp.concatenate(ys, axis=1), gpost_ref[...])


def _pool_layer(h3, j, p):
    b, lp, d = h3.shape
    assert max(POOL_WINDOWS) <= POOL_HALO
    tl = lp // 4 if lp % (4 * POOL_HALO) == 0 else lp
    halo_blocks = tl // POOL_HALO

    def full(a):
        return pl.BlockSpec(a.shape, lambda bi, i, nd=a.ndim: (0,) * nd)

    consts = [p["norm_mix_pre"].reshape(1, d), p["pool_w"][j].astype(BF16), p["pool_scale"][j].reshape(1, d),
              p["norm_mix_post"].reshape(1, d)]
    return pl.pallas_call(
        _pool_kernel,
        out_shape=jax.ShapeDtypeStruct((b, lp, d), F32),
        grid=(b, lp // tl),
        in_specs=[pl.BlockSpec((1, tl, d), lambda bi, i: (bi, i, 0)),
                  pl.BlockSpec((1, POOL_HALO, d), lambda bi, i: (bi, jnp.maximum(i * halo_blocks - 1, 0), 0))]
                 + [full(a) for a in consts],
        out_specs=pl.BlockSpec((1, tl, d), lambda bi, i: (bi, i, 0)),
        compiler_params=_cparams(("parallel", "parallel")),
        name="pool_layer",
    )(h3, h3, *consts)


def kernel(x, meta_tokens, norm_mix_pre, norm_mix_post, norm_ffn_pre, norm_ffn_post, ffn_w_gate, ffn_w_up, ffn_w_down, rwkv_mix, rwkv_w0, rwkv_w1, rwkv_w2, rwkv_a0, rwkv_a1, rwkv_a2, rwkv_v0, rwkv_v1, rwkv_v2, rwkv_g1, rwkv_g2, rwkv_k_k, rwkv_k_a, rwkv_r_k, rwkv_w_rkv, rwkv_w_o, rwkv_lnx_g, rwkv_lnx_b, mla_w_in, mla_q_norm, mla_w_uq, mla_kv_norm, mla_w_ukv, mla_w_o, pool_w, pool_scale):
    b, seq, d = x.shape
    depth = norm_mix_pre.shape[0]
    ltot = N_META + seq
    lp = -(-ltot // SEQ_ALIGN) * SEQ_ALIGN
    meta = jnp.broadcast_to(meta_tokens.astype(x.dtype)[None], (b, N_META, d))
    h = jnp.concatenate([meta, x, jnp.zeros((b, lp - ltot, d), x.dtype)], axis=1)
    shared = dict(rwkv_mix=rwkv_mix, rwkv_w0=rwkv_w0, rwkv_w1=rwkv_w1, rwkv_w2=rwkv_w2, rwkv_a0=rwkv_a0,
                  rwkv_a1=rwkv_a1, rwkv_a2=rwkv_a2, rwkv_v0=rwkv_v0, rwkv_v1=rwkv_v1, rwkv_v2=rwkv_v2,
                  rwkv_g1=rwkv_g1, rwkv_g2=rwkv_g2, rwkv_k_k=rwkv_k_k, rwkv_k_a=rwkv_k_a, rwkv_r_k=rwkv_r_k,
                  rwkv_w_rkv=rwkv_w_rkv.astype(BF16), rwkv_w_o=rwkv_w_o.astype(BF16),
                  rwkv_lnx_g=rwkv_lnx_g, rwkv_lnx_b=rwkv_lnx_b,
                  mla_w_in=mla_w_in, mla_q_norm=mla_q_norm, mla_w_uq=mla_w_uq, mla_kv_norm=mla_kv_norm,
                  mla_w_ukv=mla_w_ukv, mla_w_o=mla_w_o.astype(BF16), pool_w=pool_w, pool_scale=pool_scale)
    w_gate, w_up, w_down = ffn_w_gate.astype(BF16), ffn_w_up.astype(BF16), ffn_w_down.astype(BF16)
    v_first = None
    for i in range(depth):
        p = dict(shared, norm_mix_pre=norm_mix_pre[i], norm_mix_post=norm_mix_post[i])
        kind, j = i % 3, i // 3
        if kind == 0:
            h, v_first = _rwkv_layer(h, j, v_first, p)
        elif kind == 1:
            h = _mla_layer(h, j, p)
        else:
            h = _pool_layer(h, j, p)
        if i == depth - 1:
            return _ffn_window(h, N_META, seq, norm_ffn_pre[i], norm_ffn_post[i], w_gate, w_up, w_down, i)
        h = _ffn(h.reshape(b * lp, d), norm_ffn_pre[i], norm_ffn_post[i], w_gate, w_up, w_down, i).reshape(b, lp, d)
```

```python
import functools

import jax
import jax.numpy as jnp
from jax import lax
from jax.experimental import pallas as pl
from jax.experimental.pallas import tpu as pltpu

F32 = jnp.float32
BF16 = jnp.bfloat16

N_META = 16
RMS_EPS = 1e-6
RWKV_HEAD = 64
LNX_EPS = 1e-5 * RWKV_HEAD
MLA_HEADS = 16
Q_LORA = 512
KV_LORA = 512
NOPE_D = 128
ROPE_D = 64
V_D = 128
ROPE_THETA = 10000.0
POOL_WINDOWS = (2, 4, 8, 16)

LANES = 128
WKV_CHUNK = 64
SEQ_ALIGN = WKV_CHUNK
WKV_HEADS_PER_GROUP = 4
WKV_GROUP = WKV_HEADS_PER_GROUP * RWKV_HEAD
WKV_GROUPS_PER_STEP = 8
WKV_CHUNKS_PER_STEP = 3
ATTN_Q_TILE = 256
ATTN_HEADS_PER_STEP = 4
VMEM_LIMIT = 56 * 1024 * 1024


def _cparams(sem):
    return pltpu.CompilerParams(dimension_semantics=sem, vmem_limit_bytes=VMEM_LIMIT)


def _dot(a, b):
    return jnp.dot(a, b, preferred_element_type=F32)


def _dot_nt(a, b):
    return lax.dot_general(a, b, (((1,), (1,)), ((), ())), preferred_element_type=F32)


def _dot_tn(a, b):
    return lax.dot_general(a, b, (((0,), (0,)), ((), ())), preferred_element_type=F32)


def _rms(x, g, eps=RMS_EPS):
    ms = jnp.mean(x * x, axis=-1, keepdims=True)
    return x * lax.rsqrt(ms + eps) * g


def _sigmoid(x):
    return 1.0 / (1.0 + jnp.exp(-x))


def _row_tile(m, cap=768):
    for t in (1056, 1024, 768, 512, 384, 256, 128, 64, 32, 16):
        if t <= cap and m % t == 0:
            return t
    raise ValueError(f"unsupported row count {m}")


def _proj_residual_kernel(x_ref, w_ref, h_ref, g_ref, o_ref):
    half = o_ref.shape[0] // 2
    for r0 in (0, half):
        rs = slice(r0, r0 + half)
        o_ref[rs, :] = h_ref[rs, :] + _rms(_dot(x_ref[rs, :], w_ref[...]), g_ref[...])


def _proj_residual(x2, w_all, layer, h2, g):
    m, k = x2.shape
    d = w_all.shape[2]
    bm = _row_tile(m)
    return pl.pallas_call(
        _proj_residual_kernel,
        out_shape=jax.ShapeDtypeStruct((m, d), F32),
        grid=(m // bm,),
        in_specs=[pl.BlockSpec((bm, k), lambda i: (i, 0)),
                  pl.BlockSpec((None, k, d), lambda i: (layer, 0, 0)),
                  pl.BlockSpec((bm, d), lambda i: (i, 0)),
                  pl.BlockSpec((1, d), lambda i: (0, 0))],
        out_specs=pl.BlockSpec((bm, d), lambda i: (i, 0)),
        compiler_params=_cparams(("parallel",)),
        name="proj_residual",
    )(x2, w_all, h2, g.reshape(1, d))


FFN_HIDDEN_TILE = 512
FFN_OUT_TILE = 512
FFN_ROW_SPLIT = 2
FFN_ROW_TILE_CAP = 1056


def _ffn_kernel(h_ref, gpre_ref, gpost_ref, wg_ref, wu_ref, wd_ref, o_ref, xn_ref):
    f = pl.program_id(1)
    last_f = pl.num_programs(1) - 1
    bm, d = o_ref.shape
    rows = bm // FFN_ROW_SPLIT

    def step(first, last):
        for r0 in range(0, bm, rows):
            rs = slice(r0, r0 + rows)
            if first:
                xn_ref[rs, :] = _rms(h_ref[rs, :], gpre_ref[...]).astype(BF16)
            x = xn_ref[rs, :]
            gate = _dot(x, wg_ref[...])
            up = _dot(x, wu_ref[...])
            mid = (gate * _sigmoid(gate) * up).astype(BF16)
            for n0 in range(0, d, FFN_OUT_TILE):
                ns = slice(n0, n0 + FFN_OUT_TILE)
                part = _dot(mid, wd_ref[:, ns])
                if first:
                    o_ref[rs, ns] = part
                else:
                    o_ref[rs, ns] += part
            if last:
                o_ref[rs, :] = h_ref[rs, :] + _rms(o_ref[rs, :], gpost_ref[...])

    pl.when(f == 0)(lambda: step(True, False))
    pl.when(jnp.logical_and(f > 0, f < last_f))(lambda: step(False, False))
    pl.when(f == last_f)(lambda: step(False, True))


def _ffn(h2, gpre, gpost, wg, wu, wd, layer):
    m, d = h2.shape
    f = wg.shape[2]
    bm = _row_tile(m, FFN_ROW_TILE_CAP)
    bf = FFN_HIDDEN_TILE if f % FFN_HIDDEN_TILE == 0 else f
    assert d % FFN_OUT_TILE == 0 and bm % (16 * FFN_ROW_SPLIT) == 0
    return pl.pallas_call(
        _ffn_kernel,
        out_shape=jax.ShapeDtypeStruct((m, d), F32),
        grid=(m // bm, f // bf),
        in_specs=[pl.BlockSpec((bm, d), lambda i, j: (i, 0)),
                  pl.BlockSpec((1, d), lambda i, j: (0, 0)),
                  pl.BlockSpec((1, d), lambda i, j: (0, 0)),
                  pl.BlockSpec((None, d, bf), lambda i, j: (layer, 0, j)),
                  pl.BlockSpec((None, d, bf), lambda i, j: (layer, 0, j)),
                  pl.BlockSpec((None, bf, d), lambda i, j: (layer, j, 0))],
        out_specs=pl.BlockSpec((bm, d), lambda i, j: (i, 0)),
        scratch_shapes=[pltpu.VMEM((bm, d), BF16)],
        compiler_params=_cparams(("parallel", "arbitrary")),
        name="ffn",
    )(h2, gpre.reshape(1, d), gpost.reshape(1, d), wg, wu, wd)


def _ffn_window(h3, start, count, gpre, gpost, wg, wu, wd, layer):
    b, _, d = h3.shape
    f = wg.shape[2]
    bm = next(t for t in (1024, 512, 256, 128, 64, 32) if count % t == 0)
    bf = FFN_HIDDEN_TILE if f % FFN_HIDDEN_TILE == 0 else f
    assert d % FFN_OUT_TILE == 0 and bm % (16 * FFN_ROW_SPLIT) == 0 and start % 8 == 0
    def window_kernel(h_ref, *rest):
        _ffn_kernel(h_ref.at[0], *rest)

    return pl.pallas_call(
        window_kernel,
        out_shape=jax.ShapeDtypeStruct((b, count, d), F32),
        grid=(b * (count // bm), f // bf),
        in_specs=[pl.BlockSpec((pl.Element(1), pl.Element(bm), pl.Element(d)),
                               lambda i, j: (i // (count // bm),
                                             pl.multiple_of(start + (i % (count // bm)) * bm, 8), 0)),
                  pl.BlockSpec((1, d), lambda i, j: (0, 0)),
                  pl.BlockSpec((1, d), lambda i, j: (0, 0)),
                  pl.BlockSpec((None, d, bf), lambda i, j: (layer, 0, j)),
                  pl.BlockSpec((None, d, bf), lambda i, j: (layer, 0, j)),
                  pl.BlockSpec((None, bf, d), lambda i, j: (layer, j, 0))],
        out_specs=pl.BlockSpec((None, bm, d), lambda i, j: (i // (count // bm), i % (count // bm), 0)),
        scratch_shapes=[pltpu.VMEM((bm, d), BF16)],
        compiler_params=_cparams(("parallel", "arbitrary")),
        name="ffn_window",
    )(h3, gpre.reshape(1, d), gpost.reshape(1, d), wg, wu, wd)


def _rkv_kernel(hn_ref, xx_ref, mix_ref, w_ref, o_ref):
    xs = (hn_ref[...].astype(F32) + xx_ref[...].astype(F32) * mix_ref[0]).astype(BF16)
    o_ref[...] = _dot(xs, w_ref[...]).astype(BF16)


def _rkv_proj(hn2, xx2, mix3, w_rkv_all, layer):
    m, d = hn2.shape
    bm = _row_tile(m)
    return pl.pallas_call(
        _rkv_kernel,
        out_shape=jax.ShapeDtypeStruct((m, 3 * d), BF16),
        grid=(m // bm, 3),
        in_specs=[pl.BlockSpec((bm, d), lambda i, s: (i, 0)),
                  pl.BlockSpec((bm, d), lambda i, s: (i, 0)),
                  pl.BlockSpec((1, 1, d), lambda i, s: (s, 0, 0)),
                  pl.BlockSpec((None, None, d, d), lambda i, s: (layer, s, 0, 0))],
        out_specs=pl.BlockSpec((bm, d), lambda i, s: (i, s)),
        compiler_params=_cparams(("parallel", "parallel")),
        name="rkv_proj",
    )(hn2, xx2, mix3, w_rkv_all)


LORA_SMALL_TILE = 256


def _shift_lora_kernel(*refs, has_vres, w_rank):
    if has_vres:
        (h_ref, halo_ref, gpre_ref, wf_ref, w0_ref, w2_ref, a0_ref, a2_ref, g2_ref, v0_ref, v2_ref,
         hn_ref, xx_ref, ld_ref, a_ref, g_ref, vg_ref) = refs
    else:
        (h_ref, halo_ref, gpre_ref, wf_ref, w0_ref, w2_ref, a0_ref, a2_ref, g2_ref,
         hn_ref, xx_ref, ld_ref, a_ref, g_ref) = refs
    gpre = gpre_ref[...]
    hn = _rms(h_ref[0], gpre)
    prev = jnp.where(pl.program_id(1) == 0, 0.0, _rms(halo_ref[0], gpre)[7:8])
    row = lax.broadcasted_iota(jnp.int32, hn.shape, 0)
    shifted = jnp.where(row == 0, prev, pltpu.roll(hn, 1, axis=0))
    hn_b = hn.astype(BF16)
    xx_b = (shifted - hn).astype(BF16)
    hn_ref[0] = hn_b
    xx_ref[0] = xx_b
    t = _dot(jnp.concatenate([hn_b, xx_b], axis=1), wf_ref[...])
    small = t[:, :LORA_SMALL_TILE]
    lane = lax.broadcasted_iota(jnp.int32, (1, LORA_SMALL_TILE), 1)
    act = jnp.where(lane < w_rank, jnp.tanh(small), small).astype(BF16)
    z = w0_ref[...] + _dot(act, w2_ref[...])
    ld_ref[0] = -jnp.exp(F32(-0.5)) * _sigmoid(z)
    a_ref[0] = _sigmoid(a0_ref[...] + _dot(act, a2_ref[...])).astype(BF16)
    g_ref[0] = _dot(_sigmoid(t[:, LORA_SMALL_TILE:]).astype(BF16), g2_ref[...]).astype(BF16)
    if has_vres:
        vg_ref[0] = _sigmoid(v0_ref[...] + _dot(act, v2_ref[...])).astype(BF16)


def _shift_lora(h3, gpre, mix, w0, w1, w2, a0, a1, a2, g1, g2, vres):
    b, lp, d = h3.shape
    tl = next(t for t in range(min(lp, 384), 0, -16) if lp % t == 0)
    halo_blocks = tl // 8
    has_vres = vres is not None
    downs = [(w1, mix[0]), (a1, mix[1])] + ([(vres[1], mix[3])] if has_vres else [])
    ups = [w2, a2] + ([vres[2]] if has_vres else [])
    ranks = [w.shape[1] for w, _ in downs]
    assert sum(ranks) <= LORA_SMALL_TILE
    pad = LORA_SMALL_TILE - sum(ranks)
    plain = jnp.concatenate([w for w, _ in downs] + [jnp.zeros((d, pad), F32), g1], axis=1)
    mixed = jnp.concatenate([w * mx[:, None] for w, mx in downs] + [jnp.zeros((d, pad), F32), g1 * mix[2][:, None]],
                            axis=1)
    w_first = jnp.concatenate([plain, mixed], axis=0).astype(BF16)

    def placed(up, offset):
        return jnp.pad(up, ((offset, LORA_SMALL_TILE - offset - up.shape[0]), (0, 0))).astype(BF16)

    offsets = [sum(ranks[:i]) for i in range(len(ranks))]
    ups = [placed(u, o) for u, o in zip(ups, offsets)]
    consts = [gpre.reshape(1, d), w_first, w0.reshape(1, d), ups[0], a0.reshape(1, d), ups[1], g2.astype(BF16)]
    if has_vres:
        consts += [vres[0].reshape(1, d), ups[2]]
    row = pl.BlockSpec((1, tl, d), lambda bi, i: (bi, i, 0))
    halo = pl.BlockSpec((1, 8, d), lambda bi, i: (bi, jnp.maximum(i * halo_blocks - 1, 0), 0))

    def full(a):
        return pl.BlockSpec(a.shape, lambda bi, i: (0, 0))

    dtypes = [BF16, BF16, F32, BF16, BF16] + ([BF16] if has_vres else [])
    return pl.pallas_call(
        functools.partial(_shift_lora_kernel, has_vres=has_vres, w_rank=ranks[0]),
        out_shape=tuple(jax.ShapeDtypeStruct((b, lp, d), t) for t in dtypes),
        grid=(b, lp // tl),
        in_specs=[row, halo] + [full(a) for a in consts],
        out_specs=tuple(row for _ in dtypes),
        compiler_params=_cparams(("parallel", "parallel")),
        name="shift_lora",
    )(h3, h3, *consts)


def _split_dot(a_bf16, x):
    hi = x.astype(BF16)
    lo = (x - hi.astype(F32)).astype(BF16)
    return _dot(a_bf16, hi) + _dot(a_bf16, lo)


def _split_dot_rhs(x, w_bf16):
    hi = x.astype(BF16)
    lo = (x - hi.astype(F32)).astype(BF16)
    return _dot(hi, w_bf16) + _dot(lo, w_bf16)


def _wkv_kernel(*refs, n_chunks, n_groups, has_vres):
    if has_vres:
        (r_ref, k_ref, v_ref, ld_ref, a_ref, g_ref, vf_ref, vg_ref,
         kk_ref, ka_ref, rk_ref, lg_ref, lb_ref, mbd_ref, msk_ref, tri_ref, o_ref, s_ref) = refs
    else:
        (r_ref, k_ref, v_ref, ld_ref, a_ref, g_ref,
         kk_ref, ka_ref, rk_ref, lg_ref, lb_ref, mbd_ref, msk_ref, tri_ref, o_ref, s_ref) = refs
    C, G, HG, N = WKV_CHUNK, WKV_GROUP, WKV_HEADS_PER_GROUP, RWKV_HEAD

    @pl.when(pl.program_id(2) == 0)
    def _():
        s_ref[...] = jnp.zeros_like(s_ref)

    def bd(x):
        xb = x.astype(BF16)
        return jnp.concatenate([xb] * HG, axis=0) * mbd_ref[...]

    gs = range(n_groups)
    lanes = [slice(gi * G, (gi + 1) * G) for gi in gs]

    def group_rows(x, n):
        return [x[i * n:(i + 1) * n] for i in gs]

    def chunk(j, carry):
        rows = pl.ds(pl.multiple_of(j * C, C), C)
        strict, incl, eye = msk_ref[0], msk_ref[1], msk_ref[2]
        r = [r_ref[0, rows, lanes[i]].astype(F32) for i in gs]
        k = [k_ref[0, rows, lanes[i]].astype(F32) for i in gs]
        v = [v_ref[0, rows, lanes[i]].astype(F32) for i in gs]
        a_s = [a_ref[0, rows, lanes[i]].astype(F32) for i in gs]
        if has_vres:
            v = [v[i] + (vf_ref[0, rows, lanes[i]].astype(F32) - v[i]) * vg_ref[0, rows, lanes[i]].astype(F32)
                 for i in gs]
        kk = [k[i] * kk_ref[:, lanes[i]] for i in gs]
        kp = [k[i] * (1.0 + (a_s[i] - 1.0) * ka_ref[:, lanes[i]]) for i in gs]
        lw = [_split_dot(tri_ref[...], ld_ref[0, rows, lanes[i]]) for i in gs]
        sums = group_rows(_dot(jnp.concatenate(
            [jnp.concatenate([kk[i] * kk[i], r[i] * kp[i] * rk_ref[:, lanes[i]]], axis=0).astype(BF16) for i in gs],
            axis=0), mbd_ref[...]), 2 * C)
        kk = [kk[i] / jnp.maximum(jnp.sqrt(sums[i][:C]), 1e-12) for i in gs]
        bb = [kk[i] * a_s[i] for i in gs]
        lw_last = [lw[i][C - 1:C, :] for i in gs]
        w_inv = [jnp.exp(-lw[i]) for i in gs]
        w_tail = [jnp.exp(lw_last[i] - lw[i]) for i in gs]
        a_t = [(-kk[i] * jnp.exp(lw[i] - ld_ref[0, rows, lanes[i]])).astype(BF16) for i in gs]
        r_t = [(r[i] * jnp.exp(lw[i])).astype(BF16) for i in gs]

        sc = [_dot_nt(jnp.concatenate([a_t[i], r_t[i]], axis=0),
                      jnp.concatenate([bd(bb[i] * w_inv[i]), bd(kp[i] * w_inv[i])], axis=0)) for i in gs]
        l_ab = [sc[i][:C, :G] * strict for i in gs]
        l_rb = [(sc[i][C:, :G] * incl).astype(BF16) for i in gs]
        l_k = [jnp.concatenate([sc[i][:C, G:] * strict, sc[i][C:, G:] * incl], axis=0).astype(BF16) for i in gs]
        v_bd = [bd(v[i]) for i in gs]
        lkv = [_dot(l_k[i], v_bd[i]) for i in gs]

        p = [l_ab[i].astype(BF16) for i in gs]
        t = [eye + l_ab[i] for i in gs]
        power = 1
        while power < C:
            if power == 1:
                p = [_dot(p[i], bd(p[i])).astype(BF16) for i in gs]
            elif power * 2 < C:
                res = [_dot(jnp.concatenate([p[i], t[i].astype(BF16)], axis=0), bd(p[i])) for i in gs]
                p = [res[i][:C].astype(BF16) for i in gs]
                t = [t[i] + res[i][C:] for i in gs]
            else:
                t = [t[i] + _dot(t[i].astype(BF16), bd(p[i])) for i in gs]
            power *= 2

        ta = [_dot(t[i].astype(BF16), jnp.concatenate([bd(a_t[i]), bd(lkv[i][:C])], axis=1)) for i in gs]
        s0 = [s_ref[i] for i in gs]
        res = [_dot_nt(jnp.concatenate([ta[i][:, :G].astype(BF16), r_t[i]], axis=0), s0[i].astype(BF16)) for i in gs]
        u_bd = [bd(res[i][:C] + ta[i][:, G:]) for i in gs]
        y = [res[i][C:] + lkv[i][C:] + _dot(l_rb[i], u_bd[i]) for i in gs]
        uv_stack = [jnp.concatenate([u_bd[i], v_bd[i]], axis=0) for i in gs]
        for i in gs:
            s_ref[i] = s0[i] * jnp.exp(lw_last[i]) + _dot_tn(
                uv_stack[i], jnp.concatenate([bd(bb[i] * w_tail[i]), bd(kp[i] * w_tail[i])], axis=0))

        y_hi = [y[i].astype(BF16) for i in gs]
        y_lo = [(y[i] - y_hi[i].astype(F32)).astype(BF16) for i in gs]
        mu2 = group_rows(_dot(jnp.concatenate([jnp.concatenate([y_hi[i], y_lo[i]], axis=0) for i in gs], axis=0),
                              mbd_ref[...]), 2 * C)
        dy = [y[i] - (mu2[i][:C] + mu2[i][C:]) * (1.0 / N) for i in gs]
        var = group_rows(_dot(jnp.concatenate([(dy[i] * dy[i]).astype(BF16) for i in gs], axis=0), mbd_ref[...]), C)
        var = [var[i] * (1.0 / N) for i in gs]
        for i in gs:
            yn = dy[i] * lax.rsqrt(var[i] + LNX_EPS) * lg_ref[:, lanes[i]] + lb_ref[:, lanes[i]]
            o_ref[0, rows, lanes[i]] = ((yn + sums[i][C:] * v[i]) * g_ref[0, rows, lanes[i]].astype(F32)).astype(BF16)
        return carry

    lax.fori_loop(0, n_chunks, chunk, 0, unroll=True)


def _wkv_tables():
    C, G, N = WKV_CHUNK, WKV_GROUP, RWKV_HEAD
    idx = jnp.arange(G)
    mbd = (idx[:, None] // N == idx[None, :] // N).astype(BF16)
    t = jnp.arange(C)[:, None]
    s = (idx % N)[None, :]
    masks = jnp.stack([s < t, s <= t, s == t]).astype(F32)
    tri = (jnp.arange(C)[None, :] <= t).astype(BF16)
    return mbd, masks, tri


def _wkv(rkv, ld, a, g, rkv_first, vg, k_k, k_a, r_k, lnx_g, lnx_b):
    b, lp, d = ld.shape
    n_groups = next(n for n in (WKV_GROUPS_PER_STEP, 4, 2, 1) if d % (n * WKV_GROUP) == 0)
    width = n_groups * WKV_GROUP
    n_lane_blocks = d // width
    n_chunks = next(n for n in (WKV_CHUNKS_PER_STEP, 2, 1) if (lp // WKV_CHUNK) % n == 0)
    tl = n_chunks * WKV_CHUNK
    has_vres = rkv_first is not None

    def seq(col):
        return pl.BlockSpec((1, tl, width), lambda bi, hg, c: (bi, c, hg + col * n_lane_blocks))

    par = pl.BlockSpec((1, width), lambda bi, hg, c: (0, hg))
    args = [rkv, rkv, rkv, ld, a, g] + ([rkv_first, vg] if has_vres else [])
    specs = [seq(0), seq(1), seq(2), seq(0), seq(0), seq(0)] + ([seq(2), seq(0)] if has_vres else [])
    params = [p.reshape(1, d) for p in (k_k, k_a, r_k, lnx_g, lnx_b)]
    tables = _wkv_tables()
    table_specs = [pl.BlockSpec(t.shape, lambda bi, hg, c, nd=t.ndim: (0,) * nd) for t in tables]
    return pl.pallas_call(
        functools.partial(_wkv_kernel, n_chunks=n_chunks, n_groups=n_groups, has_vres=has_vres),
        out_shape=jax.ShapeDtypeStruct((b, lp, d), BF16),
        grid=(b, n_lane_blocks, lp // tl),
        in_specs=specs + [par] * len(params) + table_specs,
        out_specs=seq(0),
        scratch_shapes=[pltpu.VMEM((n_groups, WKV_GROUP, WKV_GROUP), F32)],
        compiler_params=_cparams(("parallel", "parallel", "arbitrary")),
        name="wkv7",
    )(*args, *params, *tables)


def _rwkv_layer(h3, j, rkv_first, p):
    b, lp, d = h3.shape
    m = b * lp
    mix = p["rwkv_mix"][j]
    vres = None if j == 0 else (p["rwkv_v0"][j - 1], p["rwkv_v1"][j - 1], p["rwkv_v2"][j - 1])
    outs = _shift_lora(h3, p["norm_mix_pre"], jnp.stack([mix[1], mix[4], mix[5], mix[3]]), p["rwkv_w0"][j],
                       p["rwkv_w1"][j], p["rwkv_w2"][j], p["rwkv_a0"][j], p["rwkv_a1"][j], p["rwkv_a2"][j],
                       p["rwkv_g1"][j], p["rwkv_g2"][j], vres)
    hn, xx, ld, a, g = outs[:5]
    vg = outs[5] if vres is not None else None
    rkv = _rkv_proj(hn.reshape(m, d), xx.reshape(m, d), jnp.stack([mix[0], mix[2], mix[3]]).reshape(3, 1, d),
                    p["rwkv_w_rkv"], j).reshape(b, lp, 3 * d)
    if vres is None:
        rkv_first = rkv
    yg = _wkv(rkv, ld, a, g, rkv_first if vres is not None else None, vg,
              p["rwkv_k_k"][j], p["rwkv_k_a"][j], p["rwkv_r_k"][j].reshape(d),
              p["rwkv_lnx_g"][j], p["rwkv_lnx_b"][j])
    h2 = _proj_residual(yg.reshape(m, d), p["rwkv_w_o"], j, h3.reshape(m, d), p["norm_mix_post"])
    return h2.reshape(b, lp, d), rkv_first


def _mla_proj_kernel(h_ref, g_ref, win_ref, qn_ref, wuq_ref, kvn_ref, wukv_ref, cos_ref, sin_ref,
                     q_ref, kv_ref, kpe_ref, *, scale):
    hn = _rms(h_ref[...], g_ref[...]).astype(BF16)
    lat = _dot(hn, win_ref[...])
    cq = _rms(lat[:, :Q_LORA], qn_ref[...]).astype(BF16)
    ckv = _rms(lat[:, Q_LORA:Q_LORA + KV_LORA], kvn_ref[...]).astype(BF16)
    cos, sin = cos_ref[...], sin_ref[...]

    def rope(x):
        return x * cos + pltpu.roll(x, ROPE_D // 2, axis=1) * sin

    keep = (lax.broadcasted_iota(jnp.int32, (1, LANES), 1) < ROPE_D).astype(F32)
    kpe_ref[...] = (rope(lat[:, Q_LORA + KV_LORA:]) * keep).astype(BF16)
    kv_ref[...] = _dot(ckv, wukv_ref[...]).astype(BF16)
    q = _dot(cq, wuq_ref[...])
    hw = NOPE_D + LANES
    for h in range(MLA_HEADS):
        q_ref[:, h * hw:h * hw + NOPE_D] = (q[:, h * hw:h * hw + NOPE_D] * scale).astype(BF16)
        q_ref[:, h * hw + NOPE_D:(h + 1) * hw] = (rope(q[:, h * hw + NOPE_D:(h + 1) * hw]) * scale).astype(BF16)


def _mla_attn_kernel(q_ref, kv_ref, kpe_ref, o_ref, kcat_ref, vone_ref):
    lp = q_ref.shape[1]
    hw, kvw = NOPE_D + LANES, NOPE_D + V_D
    for e in range(ATTN_HEADS_PER_STEP):
        kcat_ref[e, :, :NOPE_D] = kv_ref[0, :, e * kvw:e * kvw + NOPE_D]
        kcat_ref[e, :, NOPE_D:] = kpe_ref[0]
        vone_ref[e, :, :V_D] = kv_ref[0, :, e * kvw + NOPE_D:(e + 1) * kvw]
        vone_ref[e, :, V_D:] = jnp.ones((lp, LANES), BF16)
    items = [(e, q0, min(q0 + ATTN_Q_TILE, lp)) for q0 in range(0, lp, ATTN_Q_TILE)
             for e in range(ATTN_HEADS_PER_STEP)]

    def scores(e, q0, q1):
        q = q_ref[0, q0:q1, e * hw:(e + 1) * hw]
        sd = _dot_nt(q, kcat_ref[e, q0:q1, :])
        sp = _dot_nt(q, kcat_ref[e, :q0, :]) if q0 > 0 else None
        return sd, sp

    def finish(e, q0, q1, sd, sp):
        row = lax.broadcasted_iota(jnp.int32, sd.shape, 0)
        col = lax.broadcasted_iota(jnp.int32, sd.shape, 1)
        sd = jnp.where(col <= row, sd, -1e30)
        mx = jnp.max(sd, axis=-1, keepdims=True)
        if sp is not None:
            mx = jnp.maximum(mx, jnp.max(sp, axis=-1, keepdims=True))
        acc = _dot(jnp.exp(sd - mx).astype(BF16), vone_ref[e, q0:q1, :])
        if sp is not None:
            acc = acc + _dot(jnp.exp(sp - mx).astype(BF16), vone_ref[e, :q0, :])
        o_ref[0, q0:q1, e * V_D:(e + 1) * V_D] = (acc[:, :V_D] / acc[:, V_D:V_D + 1]).astype(BF16)

    pending = scores(*items[0])
    for idx, item in enumerate(items):
        upcoming = scores(*items[idx + 1]) if idx + 1 < len(items) else None
        finish(*item, *pending)
        pending = upcoming


def _rope_tables(b, lp):
    half = ROPE_D // 2
    inv_freq = 1.0 / (ROPE_THETA ** (jnp.arange(0, ROPE_D, 2, dtype=F32) / ROPE_D))
    ang = jnp.arange(lp, dtype=jnp.int32).astype(F32)[:, None] * inv_freq[None, :]
    cos, sin = jnp.cos(ang), jnp.sin(ang)
    cos4 = jnp.tile(cos, (b, LANES // half))
    sin4 = jnp.tile(jnp.concatenate([-sin, sin], axis=1), (b, LANES // ROPE_D))
    return cos4, sin4


def _mla_layer(h3, j, p):
    b, lp, d = h3.shape
    m = b * lp
    hq = MLA_HEADS
    hw = NOPE_D + LANES
    bm = _row_tile(m, 384)
    w_in = p["mla_w_in"][j]
    w_in = jnp.concatenate([w_in, w_in[:, Q_LORA + KV_LORA:]], axis=1).astype(BF16)
    w_uq = p["mla_w_uq"][j].reshape(Q_LORA, hq, NOPE_D + ROPE_D)
    w_uq = jnp.concatenate([w_uq, w_uq[:, :, NOPE_D:]], axis=2).reshape(Q_LORA, hq * hw).astype(BF16)
    w_ukv = p["mla_w_ukv"][j].astype(BF16)
    cos4, sin4 = _rope_tables(b, lp)
    scale = float((NOPE_D + ROPE_D) ** -0.5)

    def full(a):
        return pl.BlockSpec(a.shape, lambda i: (0, 0))

    consts = [p["norm_mix_pre"].reshape(1, d), w_in, p["mla_q_norm"][j].reshape(1, Q_LORA), w_uq,
              p["mla_kv_norm"][j].reshape(1, KV_LORA), w_ukv]
    q, kv, kpe = pl.pallas_call(
        functools.partial(_mla_proj_kernel, scale=scale),
        out_shape=(jax.ShapeDtypeStruct((m, hq * hw), BF16),
                   jax.ShapeDtypeStruct((m, hq * (NOPE_D + V_D)), BF16),
                   jax.ShapeDtypeStruct((m, LANES), BF16)),
        grid=(m // bm,),
        in_specs=[pl.BlockSpec((bm, d), lambda i: (i, 0))] + [full(a) for a in consts]
                 + [pl.BlockSpec((bm, LANES), lambda i: (i, 0))] * 2,
        out_specs=(pl.BlockSpec((bm, hq * hw), lambda i: (i, 0)),
                   pl.BlockSpec((bm, hq * (NOPE_D + V_D)), lambda i: (i, 0)),
                   pl.BlockSpec((bm, LANES), lambda i: (i, 0))),
        compiler_params=_cparams(("parallel",)),
        name="mla_proj",
    )(h3.reshape(m, d), *consts, cos4, sin4)

    o = pl.pallas_call(
        _mla_attn_kernel,
        out_shape=jax.ShapeDtypeStruct((b, lp, hq * V_D), BF16),
        grid=(b, hq // ATTN_HEADS_PER_STEP),
        in_specs=[pl.BlockSpec((1, lp, ATTN_HEADS_PER_STEP * hw), lambda bi, h: (bi, 0, h)),
                  pl.BlockSpec((1, lp, ATTN_HEADS_PER_STEP * (NOPE_D + V_D)), lambda bi, h: (bi, 0, h)),
                  pl.BlockSpec((1, lp, LANES), lambda bi, h: (bi, 0, 0))],
        out_specs=pl.BlockSpec((1, lp, ATTN_HEADS_PER_STEP * V_D), lambda bi, h: (bi, 0, h)),
        scratch_shapes=[pltpu.VMEM((ATTN_HEADS_PER_STEP, lp, hw), BF16),
                        pltpu.VMEM((ATTN_HEADS_PER_STEP, lp, V_D + LANES), BF16)],
        compiler_params=_cparams(("parallel", "parallel")),
        name="mla_attention",
    )(q.reshape(b, lp, hq * hw), kv.reshape(b, lp, hq * (NOPE_D + V_D)), kpe.reshape(b, lp, LANES))

    h2 = _proj_residual(o.reshape(m, hq * V_D), p["mla_w_o"], j, h3.reshape(m, d), p["norm_mix_post"])
    return h2.reshape(b, lp, d)


POOL_HALO = 16


def _pool_kernel(h_ref, halo_ref, gpre_ref, w_ref, sc_ref, gpost_ref, o_ref):
    i = pl.program_id(1)
    tl, d = h_ref.shape[1], h_ref.shape[2]
    gw = d // len(POOL_WINDOWS)
    h = h_ref[0]
    hn = _rms(h, gpre_ref[...])
    hist = jnp.where(i == 0, 0.0, _rms(halo_ref[0], gpre_ref[...]))
    ext = jnp.concatenate([hist, hn], axis=0)
    t = i * tl + lax.broadcasted_iota(jnp.int32, (tl, gw), 0)
    ys = []
    for gi, win in enumerate(POOL_WINDOWS):
        cols = slice(gi * gw, (gi + 1) * gw)
        s = ext[:, cols]
        step = 1
        while step < win:
            s = s + pltpu.roll(s, step, axis=0)
            step *= 2
        cnt = jnp.minimum(t + 1, win).astype(F32)
        pooled = (s[POOL_HALO:] / cnt - hn[:, cols]).astype(BF16)
        ys.append(_dot(pooled, w_ref[gi]) * sc_ref[:, cols])
    o_ref[0] = h + _rms(jnp.concatenate(ys, axis=1), gpost_ref[...])


def _pool_layer(h3, j, p):
    b, lp, d = h3.shape
    assert max(POOL_WINDOWS) <= POOL_HALO
    tl = lp // 4 if lp % (4 * POOL_HALO) == 0 else lp
    halo_blocks = tl // POOL_HALO

    def full(a):
        return pl.BlockSpec(a.shape, lambda bi, i, nd=a.ndim: (0,) * nd)

    consts = [p["norm_mix_pre"].reshape(1, d), p["pool_w"][j].astype(BF16), p["pool_scale"][j].reshape(1, d),
              p["norm_mix_post"].reshape(1, d)]
    return pl.pallas_call(
        _pool_kernel,
        out_shape=jax.ShapeDtypeStruct((b, lp, d), F32),
        grid=(b, lp // tl),
        in_specs=[pl.BlockSpec((1, tl, d), lambda bi, i: (bi, i, 0)),
                  pl.BlockSpec((1, POOL_HALO, d), lambda bi, i: (bi, jnp.maximum(i * halo_blocks - 1, 0), 0))]
                 + [full(a) for a in consts],
        out_specs=pl.BlockSpec((1, tl, d), lambda bi, i: (bi, i, 0)),
        compiler_params=_cparams(("parallel", "parallel")),
        name="pool_layer",
    )(h3, h3, *consts)


def kernel(x, meta_tokens, norm_mix_pre, norm_mix_post, norm_ffn_pre, norm_ffn_post, ffn_w_gate, ffn_w_up, ffn_w_down, rwkv_mix, rwkv_w0, rwkv_w1, rwkv_w2, rwkv_a0, rwkv_a1, rwkv_a2, rwkv_v0, rwkv_v1, rwkv_v2, rwkv_g1, rwkv_g2, rwkv_k_k, rwkv_k_a, rwkv_r_k, rwkv_w_rkv, rwkv_w_o, rwkv_lnx_g, rwkv_lnx_b, mla_w_in, mla_q_norm, mla_w_uq, mla_kv_norm, mla_w_ukv, mla_w_o, pool_w, pool_scale):
    b, seq, d = x.shape
    depth = norm_mix_pre.shape[0]
    ltot = N_META + seq
    lp = -(-ltot // SEQ_ALIGN) * SEQ_ALIGN
    meta = jnp.broadcast_to(meta_tokens.astype(x.dtype)[None], (b, N_META, d))
    h = jnp.concatenate([meta, x, jnp.zeros((b, lp - ltot, d), x.dtype)], axis=1)
    shared = dict(rwkv_mix=rwkv_mix, rwkv_w0=rwkv_w0, rwkv_w1=rwkv_w1, rwkv_w2=rwkv_w2, rwkv_a0=rwkv_a0,
                  rwkv_a1=rwkv_a1, rwkv_a2=rwkv_a2, rwkv_v0=rwkv_v0, rwkv_v1=rwkv_v1, rwkv_v2=rwkv_v2,
                  rwkv_g1=rwkv_g1, rwkv_g2=rwkv_g2, rwkv_k_k=rwkv_k_k, rwkv_k_a=rwkv_k_a, rwkv_r_k=rwkv_r_k,
                  rwkv_w_rkv=rwkv_w_rkv.astype(BF16), rwkv_w_o=rwkv_w_o.astype(BF16),
                  rwkv_lnx_g=rwkv_lnx_g, rwkv_lnx_b=rwkv_lnx_b,
                  mla_w_in=mla_w_in, mla_q_norm=mla_q_norm, mla_w_uq=mla_w_uq, mla_kv_norm=mla_kv_norm,
                  mla_w_ukv=mla_w_ukv, mla_w_o=mla_w_o.astype(BF16), pool_w=pool_w, pool_scale=pool_scale)
    w_gate, w_up, w_down = ffn_w_gate.astype(BF16), ffn_w_up.astype(BF16), ffn_w_down.astype(BF16)
    v_first = None
    for i in range(depth):
        p = dict(shared, norm_mix_pre=norm_mix_pre[i], norm_mix_post=norm_mix_post[i])
        kind, j = i % 3, i // 3
        if kind == 0:
            h, v_first = _rwkv_layer(h, j, v_first, p)
        elif kind == 1:
            h = _mla_layer(h, j, p)
        else:
            h = _pool_layer(h, j, p)
        if i == depth - 1:
            return _ffn_window(h, N_META, seq, norm_ffn_pre[i], norm_ffn_post[i], w_gate, w_up, w_down, i)
        h = _ffn(h.reshape(b * lp, d), norm_ffn_pre[i], norm_ffn_post[i], w_gate, w_up, w_down, i).reshape(b, lp, d)
```

```python
import functools

import jax
import jax.numpy as jnp
from jax import lax
from jax.experimental import pallas as pl
from jax.experimental.pallas import tpu as pltpu

F32 = jnp.float32
BF16 = jnp.bfloat16

N_META = 16
RMS_EPS = 1e-6
RWKV_HEAD = 64
LNX_EPS = 1e-5 * RWKV_HEAD
MLA_HEADS = 16
Q_LORA = 512
KV_LORA = 512
NOPE_D = 128
ROPE_D = 64
V_D = 128
ROPE_THETA = 10000.0
POOL_WINDOWS = (2, 4, 8, 16)

LANES = 128
WKV_CHUNK = 64
SEQ_ALIGN = WKV_CHUNK
WKV_HEADS_PER_GROUP = 4
WKV_GROUP = WKV_HEADS_PER_GROUP * RWKV_HEAD
WKV_GROUPS_PER_STEP = 8
WKV_CHUNKS_PER_STEP = 3
ATTN_Q_TILE = 256
ATTN_HEADS_PER_STEP = 4
V7X_VMEM_BYTES = 64 * 1024 * 1024
VMEM_LIMIT = V7X_VMEM_BYTES * 7 // 8


def _cparams(sem):
    return pltpu.CompilerParams(dimension_semantics=sem, vmem_limit_bytes=VMEM_LIMIT)


def _dot(a, b):
    return jnp.dot(a, b, preferred_element_type=F32)


def _dot_nt(a, b):
    return lax.dot_general(a, b, (((1,), (1,)), ((), ())), preferred_element_type=F32)


def _dot_tn(a, b):
    return lax.dot_general(a, b, (((0,), (0,)), ((), ())), preferred_element_type=F32)


def _rms(x, g, eps=RMS_EPS):
    ms = jnp.mean(x * x, axis=-1, keepdims=True)
    return x * lax.rsqrt(ms + eps) * g


def _sigmoid(x):
    return 1.0 / (1.0 + jnp.exp(-x))


def _row_tile(m, cap=768):
    for t in (1056, 1024, 768, 512, 384, 256, 128, 64, 32, 16):
        if t <= cap and m % t == 0:
            return t
    raise ValueError(f"unsupported row count {m}")


def _proj_residual_kernel(x_ref, w_ref, h_ref, g_ref, o_ref):
    half = o_ref.shape[0] // 2
    for r0 in (0, half):
        rs = slice(r0, r0 + half)
        o_ref[rs, :] = h_ref[rs, :] + _rms(_dot(x_ref[rs, :], w_ref[...]), g_ref[...])


def _proj_residual(x2, w_all, layer, h2, g):
    m, k = x2.shape
    d = w_all.shape[2]
    bm = _row_tile(m)
    return pl.pallas_call(
        _proj_residual_kernel,
        out_shape=jax.ShapeDtypeStruct((m, d), F32),
        grid=(m // bm,),
        in_specs=[pl.BlockSpec((bm, k), lambda i: (i, 0)),
                  pl.BlockSpec((None, k, d), lambda i: (layer, 0, 0)),
                  pl.BlockSpec((bm, d), lambda i: (i, 0)),
                  pl.BlockSpec((1, d), lambda i: (0, 0))],
        out_specs=pl.BlockSpec((bm, d), lambda i: (i, 0)),
        compiler_params=_cparams(("parallel",)),
        name="proj_residual",
    )(x2, w_all, h2, g.reshape(1, d))


FFN_HIDDEN_TILE = 512
FFN_OUT_TILE = 512
FFN_ROW_SPLIT = 2
FFN_ROW_TILE_CAP = 1056


def _ffn_kernel(h_ref, gpre_ref, gpost_ref, wg_ref, wu_ref, wd_ref, o_ref, xn_ref):
    f = pl.program_id(1)
    last_f = pl.num_programs(1) - 1
    bm, d = o_ref.shape
    rows = bm // FFN_ROW_SPLIT

    def step(first, last):
        for r0 in range(0, bm, rows):
            rs = slice(r0, r0 + rows)
            if first:
                xn_ref[rs, :] = _rms(h_ref[rs, :], gpre_ref[...]).astype(BF16)
            x = xn_ref[rs, :]
            gate = _dot(x, wg_ref[...])
            up = _dot(x, wu_ref[...])
            mid = (gate * _sigmoid(gate) * up).astype(BF16)
            for n0 in range(0, d, FFN_OUT_TILE):
                ns = slice(n0, n0 + FFN_OUT_TILE)
                part = _dot(mid, wd_ref[:, ns])
                if first:
                    o_ref[rs, ns] = part
                else:
                    o_ref[rs, ns] += part
            if last:
                o_ref[rs, :] = h_ref[rs, :] + _rms(o_ref[rs, :], gpost_ref[...])

    pl.when(f == 0)(lambda: step(True, False))
    pl.when(jnp.logical_and(f > 0, f < last_f))(lambda: step(False, False))
    pl.when(f == last_f)(lambda: step(False, True))


def _ffn(h2, gpre, gpost, wg, wu, wd, layer):
    m, d = h2.shape
    f = wg.shape[2]
    bm = _row_tile(m, FFN_ROW_TILE_CAP)
    bf = FFN_HIDDEN_TILE
    assert f % bf == 0 and f // bf >= 2
    assert d % FFN_OUT_TILE == 0 and bm % (16 * FFN_ROW_SPLIT) == 0
    return pl.pallas_call(
        _ffn_kernel,
        out_shape=jax.ShapeDtypeStruct((m, d), F32),
        grid=(m // bm, f // bf),
        in_specs=[pl.BlockSpec((bm, d), lambda i, j: (i, 0)),
                  pl.BlockSpec((1, d), lambda i, j: (0, 0)),
                  pl.BlockSpec((1, d), lambda i, j: (0, 0)),
                  pl.BlockSpec((None, d, bf), lambda i, j: (layer, 0, j)),
                  pl.BlockSpec((None, d, bf), lambda i, j: (layer, 0, j)),
                  pl.BlockSpec((None, bf, d), lambda i, j: (layer, j, 0))],
        out_specs=pl.BlockSpec((bm, d), lambda i, j: (i, 0)),
        scratch_shapes=[pltpu.VMEM((bm, d), BF16)],
        compiler_params=_cparams(("parallel", "arbitrary")),
        name="ffn",
    )(h2, gpre.reshape(1, d), gpost.reshape(1, d), wg, wu, wd)


def _ffn_window(h3, start, count, gpre, gpost, wg, wu, wd, layer):
    b, _, d = h3.shape
    f = wg.shape[2]
    bm = next(t for t in (1024, 512, 256, 128, 64, 32) if count % t == 0)
    bf = FFN_HIDDEN_TILE
    assert f % bf == 0 and f // bf >= 2
    assert d % FFN_OUT_TILE == 0 and bm % (16 * FFN_ROW_SPLIT) == 0 and start % 8 == 0
    def window_kernel(h_ref, *rest):
        _ffn_kernel(h_ref.at[0], *rest)

    return pl.pallas_call(
        window_kernel,
        out_shape=jax.ShapeDtypeStruct((b, count, d), F32),
        grid=(b * (count // bm), f // bf),
        in_specs=[pl.BlockSpec((pl.Element(1), pl.Element(bm), pl.Element(d)),
                               lambda i, j: (i // (count // bm),
                                             pl.multiple_of(start + (i % (count // bm)) * bm, 8), 0)),
                  pl.BlockSpec((1, d), lambda i, j: (0, 0)),
                  pl.BlockSpec((1, d), lambda i, j: (0, 0)),
                  pl.BlockSpec((None, d, bf), lambda i, j: (layer, 0, j)),
                  pl.BlockSpec((None, d, bf), lambda i, j: (layer, 0, j)),
                  pl.BlockSpec((None, bf, d), lambda i, j: (layer, j, 0))],
        out_specs=pl.BlockSpec((None, bm, d), lambda i, j: (i // (count // bm), i % (count // bm), 0)),
        scratch_shapes=[pltpu.VMEM((bm, d), BF16)],
        compiler_params=_cparams(("parallel", "arbitrary")),
        name="ffn_window",
    )(h3, gpre.reshape(1, d), gpost.reshape(1, d), wg, wu, wd)


def _rkv_kernel(hn_ref, xx_ref, mix_ref, w_ref, o_ref):
    half = o_ref.shape[0] // 2
    for r0 in (0, half):
        rs = slice(r0, r0 + half)
        xs = (hn_ref[rs, :].astype(F32) + xx_ref[rs, :].astype(F32) * mix_ref[0]).astype(BF16)
        o_ref[rs, :] = _dot(xs, w_ref[...]).astype(BF16)


def _rkv_proj(hn2, xx2, mix3, w_rkv_all, layer):
    m, d = hn2.shape
    bm = _row_tile(m, 1056)
    return pl.pallas_call(
        _rkv_kernel,
        out_shape=jax.ShapeDtypeStruct((m, 3 * d), BF16),
        grid=(m // bm, 3),
        in_specs=[pl.BlockSpec((bm, d), lambda i, s: (i, 0)),
                  pl.BlockSpec((bm, d), lambda i, s: (i, 0)),
                  pl.BlockSpec((1, 1, d), lambda i, s: (s, 0, 0)),
                  pl.BlockSpec((None, None, d, d), lambda i, s: (layer, s, 0, 0))],
        out_specs=pl.BlockSpec((bm, d), lambda i, s: (i, s)),
        compiler_params=_cparams(("parallel", "parallel")),
        name="rkv_proj",
    )(hn2, xx2, mix3, w_rkv_all)


LORA_SMALL_TILE = 256


def _shift_lora_kernel(*refs, has_vres, w_rank):
    if has_vres:
        (h_ref, halo_ref, gpre_ref, wf_ref, w0_ref, w2_ref, a0_ref, a2_ref, g2_ref, v0_ref, v2_ref,
         hn_ref, xx_ref, ld_ref, a_ref, g_ref, vg_ref) = refs
    else:
        (h_ref, halo_ref, gpre_ref, wf_ref, w0_ref, w2_ref, a0_ref, a2_ref, g2_ref,
         hn_ref, xx_ref, ld_ref, a_ref, g_ref) = refs
    gpre = gpre_ref[...]
    hn = _rms(h_ref[0], gpre)
    prev = jnp.where(pl.program_id(1) == 0, 0.0, _rms(halo_ref[0], gpre)[7:8])
    row = lax.broadcasted_iota(jnp.int32, hn.shape, 0)
    shifted = jnp.where(row == 0, prev, pltpu.roll(hn, 1, axis=0))
    hn_b = hn.astype(BF16)
    xx_b = (shifted - hn).astype(BF16)
    hn_ref[0] = hn_b
    xx_ref[0] = xx_b
    t = _dot(jnp.concatenate([hn_b, xx_b], axis=1), wf_ref[...])
    small = t[:, :LORA_SMALL_TILE]
    lane = lax.broadcasted_iota(jnp.int32, (1, LORA_SMALL_TILE), 1)
    act = jnp.where(lane < w_rank, jnp.tanh(small), small).astype(BF16)
    z = w0_ref[...] + _dot(act, w2_ref[...])
    ld_ref[0] = -jnp.exp(F32(-0.5)) * _sigmoid(z)
    a_ref[0] = _sigmoid(a0_ref[...] + _dot(act, a2_ref[...])).astype(BF16)
    g_ref[0] = _dot(_sigmoid(t[:, LORA_SMALL_TILE:]).astype(BF16), g2_ref[...]).astype(BF16)
    if has_vres:
        vg_ref[0] = _sigmoid(v0_ref[...] + _dot(act, v2_ref[...])).astype(BF16)


def _shift_lora(h3, gpre, mix, w0, w1, w2, a0, a1, a2, g1, g2, vres):
    b, lp, d = h3.shape
    tl = next(t for t in range(min(lp, 384), 0, -16) if lp % t == 0)
    halo_blocks = tl // 8
    has_vres = vres is not None
    downs = [(w1, mix[0]), (a1, mix[1])] + ([(vres[1], mix[3])] if has_vres else [])
    ups = [w2, a2] + ([vres[2]] if has_vres else [])
    ranks = [w.shape[1] for w, _ in downs]
    assert sum(ranks) <= LORA_SMALL_TILE
    pad = LORA_SMALL_TILE - sum(ranks)
    plain = jnp.concatenate([w for w, _ in downs] + [jnp.zeros((d, pad), F32), g1], axis=1)
    mixed = jnp.concatenate([w * mx[:, None] for w, mx in downs] + [jnp.zeros((d, pad), F32), g1 * mix[2][:, None]],
                            axis=1)
    w_first = jnp.concatenate([plain, mixed], axis=0).astype(BF16)

    def placed(up, offset):
        return jnp.pad(up, ((offset, LORA_SMALL_TILE - offset - up.shape[0]), (0, 0))).astype(BF16)

    offsets = [sum(ranks[:i]) for i in range(len(ranks))]
    ups = [placed(u, o) for u, o in zip(ups, offsets)]
    consts = [gpre.reshape(1, d), w_first, w0.reshape(1, d), ups[0], a0.reshape(1, d), ups[1], g2.astype(BF16)]
    if has_vres:
        consts += [vres[0].reshape(1, d), ups[2]]
    row = pl.BlockSpec((1, tl, d), lambda bi, i: (bi, i, 0))
    halo = pl.BlockSpec((1, 8, d), lambda bi, i: (bi, jnp.maximum(i * halo_blocks - 1, 0), 0))

    def full(a):
        return pl.BlockSpec(a.shape, lambda bi, i: (0, 0))

    dtypes = [BF16, BF16, F32, BF16, BF16] + ([BF16] if has_vres else [])
    return pl.pallas_call(
        functools.partial(_shift_lora_kernel, has_vres=has_vres, w_rank=ranks[0]),
        out_shape=tuple(jax.ShapeDtypeStruct((b, lp, d), t) for t in dtypes),
        grid=(b, lp // tl),
        in_specs=[row, halo] + [full(a) for a in consts],
        out_specs=tuple(row for _ in dtypes),
        compiler_params=_cparams(("parallel", "parallel")),
        name="shift_lora",
    )(h3, h3, *consts)


def _split_dot(a_bf16, x):
    hi = x.astype(BF16)
    lo = (x - hi.astype(F32)).astype(BF16)
    return _dot(a_bf16, hi) + _dot(a_bf16, lo)


def _wkv_kernel(*refs, n_chunks, n_groups, has_vres):
    if has_vres:
        (r_ref, k_ref, v_ref, ld_ref, a_ref, g_ref, vf_ref, vg_ref,
         kk_ref, ka_ref, rk_ref, lg_ref, lb_ref, mbd_ref, msk_ref, tri_ref, o_ref, s_ref) = refs
    else:
        (r_ref, k_ref, v_ref, ld_ref, a_ref, g_ref,
         kk_ref, ka_ref, rk_ref, lg_ref, lb_ref, mbd_ref, msk_ref, tri_ref, o_ref, s_ref) = refs
    C, G, HG, N = WKV_CHUNK, WKV_GROUP, WKV_HEADS_PER_GROUP, RWKV_HEAD

    @pl.when(pl.program_id(2) == 0)
    def _():
        s_ref[...] = jnp.zeros_like(s_ref)

    def bd(x):
        xb = x.astype(BF16)
        return jnp.concatenate([xb] * HG, axis=0) * mbd_ref[...]

    gs = range(n_groups)
    lanes = [slice(gi * G, (gi + 1) * G) for gi in gs]

    def group_rows(x, n):
        return [x[i * n:(i + 1) * n] for i in gs]

    def chunk(j, carry):
        rows = pl.ds(pl.multiple_of(j * C, C), C)
        strict, incl, eye = msk_ref[0], msk_ref[1], msk_ref[2]
        r = [r_ref[0, rows, lanes[i]].astype(F32) for i in gs]
        k = [k_ref[0, rows, lanes[i]].astype(F32) for i in gs]
        v = [v_ref[0, rows, lanes[i]].astype(F32) for i in gs]
        a_s = [a_ref[0, rows, lanes[i]].astype(F32) for i in gs]
        if has_vres:
            v = [v[i] + (vf_ref[0, rows, lanes[i]].astype(F32) - v[i]) * vg_ref[0, rows, lanes[i]].astype(F32)
                 for i in gs]
        kk = [k[i] * kk_ref[:, lanes[i]] for i in gs]
        kp = [k[i] * (1.0 + (a_s[i] - 1.0) * ka_ref[:, lanes[i]]) for i in gs]
        lw = [_split_dot(tri_ref[...], ld_ref[0, rows, lanes[i]]) for i in gs]
        sums = group_rows(_dot(jnp.concatenate(
            [jnp.concatenate([kk[i] * kk[i], r[i] * kp[i] * rk_ref[:, lanes[i]]], axis=0).astype(BF16) for i in gs],
            axis=0), mbd_ref[...]), 2 * C)
        kk = [kk[i] / jnp.maximum(jnp.sqrt(sums[i][:C]), 1e-12) for i in gs]
        bb = [kk[i] * a_s[i] for i in gs]
        lw_last = [lw[i][C - 1:C, :] for i in gs]
        w_inv = [jnp.exp(-lw[i]) for i in gs]
        w_tail = [jnp.exp(lw_last[i] - lw[i]) for i in gs]
        a_t = [(-kk[i] * jnp.exp(lw[i] - ld_ref[0, rows, lanes[i]])).astype(BF16) for i in gs]
        r_t = [(r[i] * jnp.exp(lw[i])).astype(BF16) for i in gs]

        sc = [_dot_nt(jnp.concatenate([a_t[i], r_t[i]], axis=0),
                      jnp.concatenate([bd(bb[i] * w_inv[i]), bd(kp[i] * w_inv[i])], axis=0)) for i in gs]
        l_ab = [sc[i][:C, :G] * strict for i in gs]
        l_rb = [(sc[i][C:, :G] * incl).astype(BF16) for i in gs]
        l_k = [jnp.concatenate([sc[i][:C, G:] * strict, sc[i][C:, G:] * incl], axis=0).astype(BF16) for i in gs]
        v_bd = [bd(v[i]) for i in gs]
        lkv = [_dot(l_k[i], v_bd[i]) for i in gs]

        p = [l_ab[i].astype(BF16) for i in gs]
        t = [eye + l_ab[i] for i in gs]
        power = 1
        while power < C:
            if power == 1:
                p = [_dot(p[i], bd(p[i])).astype(BF16) for i in gs]
            elif power * 2 < C:
                res = [_dot(jnp.concatenate([p[i], t[i].astype(BF16)], axis=0), bd(p[i])) for i in gs]
                p = [res[i][:C].astype(BF16) for i in gs]
                t = [t[i] + res[i][C:] for i in gs]
            else:
                t = [t[i] + _dot(t[i].astype(BF16), bd(p[i])) for i in gs]
            power *= 2

        ta = [_dot(t[i].astype(BF16), jnp.concatenate([bd(a_t[i]), bd(lkv[i][:C])], axis=1)) for i in gs]
        s0 = [s_ref[i] for i in gs]
        res = [_dot_nt(jnp.concatenate([ta[i][:, :G].astype(BF16), r_t[i]], axis=0), s0[i].astype(BF16)) for i in gs]
        u_bd = [bd(res[i][:C] + ta[i][:, G:]) for i in gs]
        y = [res[i][C:] + lkv[i][C:] + _dot(l_rb[i], u_bd[i]) for i in gs]
        uv_stack = [jnp.concatenate([u_bd[i], v_bd[i]], axis=0) for i in gs]
        for i in gs:
            s_ref[i] = s0[i] * jnp.exp(lw_last[i]) + _dot_tn(
                uv_stack[i], jnp.concatenate([bd(bb[i] * w_tail[i]), bd(kp[i] * w_tail[i])], axis=0))

        y_hi = [y[i].astype(BF16) for i in gs]
        y_lo = [(y[i] - y_hi[i].astype(F32)).astype(BF16) for i in gs]
        mu2 = group_rows(_dot(jnp.concatenate([jnp.concatenate([y_hi[i], y_lo[i]], axis=0) for i in gs], axis=0),
                              mbd_ref[...]), 2 * C)
        dy = [y[i] - (mu2[i][:C] + mu2[i][C:]) * (1.0 / N) for i in gs]
        var = group_rows(_dot(jnp.concatenate([(dy[i] * dy[i]).astype(BF16) for i in gs], axis=0), mbd_ref[...]), C)
        var = [var[i] * (1.0 / N) for i in gs]
        for i in gs:
            yn = dy[i] * lax.rsqrt(var[i] + LNX_EPS) * lg_ref[:, lanes[i]] + lb_ref[:, lanes[i]]
            o_ref[0, rows, lanes[i]] = ((yn + sums[i][C:] * v[i]) * g_ref[0, rows, lanes[i]].astype(F32)).astype(BF16)
        return carry

    lax.fori_loop(0, n_chunks, chunk, 0, unroll=True)


def _wkv_tables():
    C, G, N = WKV_CHUNK, WKV_GROUP, RWKV_HEAD
    idx = jnp.arange(G)
    mbd = (idx[:, None] // N == idx[None, :] // N).astype(BF16)
    t = jnp.arange(C)[:, None]
    s = (idx % N)[None, :]
    masks = jnp.stack([s < t, s <= t, s == t]).astype(F32)
    tri = (jnp.arange(C)[None, :] <= t).astype(BF16)
    return mbd, masks, tri


def _wkv(rkv, ld, a, g, rkv_first, vg, k_k, k_a, r_k, lnx_g, lnx_b):
    b, lp, d = ld.shape
    n_groups = next(n for n in (WKV_GROUPS_PER_STEP, 4, 2, 1) if d % (n * WKV_GROUP) == 0)
    width = n_groups * WKV_GROUP
    n_lane_blocks = d // width
    n_chunks = next(n for n in (WKV_CHUNKS_PER_STEP, 2, 1) if (lp // WKV_CHUNK) % n == 0)
    tl = n_chunks * WKV_CHUNK
    has_vres = rkv_first is not None

    def seq(col):
        return pl.BlockSpec((1, tl, width), lambda bi, hg, c: (bi, c, hg + col * n_lane_blocks))

    par = pl.BlockSpec((1, width), lambda bi, hg, c: (0, hg))
    args = [rkv, rkv, rkv, ld, a, g] + ([rkv_first, vg] if has_vres else [])
    specs = [seq(0), seq(1), seq(2), seq(0), seq(0), seq(0)] + ([seq(2), seq(0)] if has_vres else [])
    params = [p.reshape(1, d) for p in (k_k, k_a, r_k, lnx_g, lnx_b)]
    tables = _wkv_tables()
    table_specs = [pl.BlockSpec(t.shape, lambda bi, hg, c, nd=t.ndim: (0,) * nd) for t in tables]
    return pl.pallas_call(
        functools.partial(_wkv_kernel, n_chunks=n_chunks, n_groups=n_groups, has_vres=has_vres),
        out_shape=jax.ShapeDtypeStruct((b, lp, d), BF16),
        grid=(b, n_lane_blocks, lp // tl),
        in_specs=specs + [par] * len(params) + table_specs,
        out_specs=seq(0),
        scratch_shapes=[pltpu.VMEM((n_groups, WKV_GROUP, WKV_GROUP), F32)],
        compiler_params=_cparams(("parallel", "parallel", "arbitrary")),
        name="wkv7",
    )(*args, *params, *tables)


def _rwkv_layer(h3, j, rkv_first, p):
    b, lp, d = h3.shape
    m = b * lp
    mix = p["rwkv_mix"][j]
    vres = None if j == 0 else (p["rwkv_v0"][j - 1], p["rwkv_v1"][j - 1], p["rwkv_v2"][j - 1])
    outs = _shift_lora(h3, p["norm_mix_pre"], jnp.stack([mix[1], mix[4], mix[5], mix[3]]), p["rwkv_w0"][j],
                       p["rwkv_w1"][j], p["rwkv_w2"][j], p["rwkv_a0"][j], p["rwkv_a1"][j], p["rwkv_a2"][j],
                       p["rwkv_g1"][j], p["rwkv_g2"][j], vres)
    hn, xx, ld, a, g = outs[:5]
    vg = outs[5] if vres is not None else None
    rkv = _rkv_proj(hn.reshape(m, d), xx.reshape(m, d), jnp.stack([mix[0], mix[2], mix[3]]).reshape(3, 1, d),
                    p["rwkv_w_rkv"], j).reshape(b, lp, 3 * d)
    if vres is None:
        rkv_first = rkv
    yg = _wkv(rkv, ld, a, g, rkv_first if vres is not None else None, vg,
              p["rwkv_k_k"][j], p["rwkv_k_a"][j], p["rwkv_r_k"][j].reshape(d),
              p["rwkv_lnx_g"][j], p["rwkv_lnx_b"][j])
    h2 = _proj_residual(yg.reshape(m, d), p["rwkv_w_o"], j, h3.reshape(m, d), p["norm_mix_post"])
    return h2.reshape(b, lp, d), rkv_first


def _mla_proj_kernel(h_ref, g_ref, win_ref, qn_ref, wuq_ref, kvn_ref, wukv_ref, cos_ref, sin_ref,
                     q_ref, kv_ref, kpe_ref, *, scale):
    hn = _rms(h_ref[...], g_ref[...]).astype(BF16)
    lat = _dot(hn, win_ref[...])
    cq = _rms(lat[:, :Q_LORA], qn_ref[...]).astype(BF16)
    ckv = _rms(lat[:, Q_LORA:Q_LORA + KV_LORA], kvn_ref[...]).astype(BF16)
    cos, sin = cos_ref[...], sin_ref[...]

    def rope(x):
        return x * cos + pltpu.roll(x, ROPE_D // 2, axis=1) * sin

    keep = (lax.broadcasted_iota(jnp.int32, (1, LANES), 1) < ROPE_D).astype(F32)
    kpe_ref[...] = (rope(lat[:, Q_LORA + KV_LORA:]) * keep).astype(BF16)
    kv_ref[...] = _dot(ckv, wukv_ref[...]).astype(BF16)
    q = _dot(cq, wuq_ref[...])
    hw = NOPE_D + LANES
    for h in range(MLA_HEADS):
        q_ref[:, h * hw:h * hw + NOPE_D] = (q[:, h * hw:h * hw + NOPE_D] * scale).astype(BF16)
        q_ref[:, h * hw + NOPE_D:(h + 1) * hw] = (rope(q[:, h * hw + NOPE_D:(h + 1) * hw]) * scale).astype(BF16)


def _mla_attn_kernel(q_ref, kv_ref, kpe_ref, o_ref, kcat_ref, vone_ref):
    lp = q_ref.shape[1]
    hw, kvw = NOPE_D + LANES, NOPE_D + V_D
    for e in range(ATTN_HEADS_PER_STEP):
        kcat_ref[e, :, :NOPE_D] = kv_ref[0, :, e * kvw:e * kvw + NOPE_D]
        kcat_ref[e, :, NOPE_D:] = kpe_ref[0]
        vone_ref[e, :, :V_D] = kv_ref[0, :, e * kvw + NOPE_D:(e + 1) * kvw]
        vone_ref[e, :, V_D:] = jnp.ones((lp, LANES), BF16)
    items = [(e, q0, min(q0 + ATTN_Q_TILE, lp)) for q0 in range(0, lp, ATTN_Q_TILE)
             for e in range(ATTN_HEADS_PER_STEP)]

    def scores(e, q0, q1):
        q = q_ref[0, q0:q1, e * hw:(e + 1) * hw]
        sd = _dot_nt(q, kcat_ref[e, q0:q1, :])
        sp = _dot_nt(q, kcat_ref[e, :q0, :]) if q0 > 0 else None
        return sd, sp

    def finish(e, q0, q1, sd, sp):
        row = lax.broadcasted_iota(jnp.int32, sd.shape, 0)
        col = lax.broadcasted_iota(jnp.int32, sd.shape, 1)
        sd = jnp.where(col <= row, sd, -1e30)
        mx = jnp.max(sd, axis=-1, keepdims=True)
        if sp is not None:
            mx = jnp.maximum(mx, jnp.max(sp, axis=-1, keepdims=True))
        acc = _dot(jnp.exp(sd - mx).astype(BF16), vone_ref[e, q0:q1, :])
        if sp is not None:
            acc = acc + _dot(jnp.exp(sp - mx).astype(BF16), vone_ref[e, :q0, :])
        o_ref[0, q0:q1, e * V_D:(e + 1) * V_D] = (acc[:, :V_D] / acc[:, V_D:V_D + 1]).astype(BF16)

    pending = scores(*items[0])
    for idx, item in enumerate(items):
        upcoming = scores(*items[idx + 1]) if idx + 1 < len(items) else None
        finish(*item, *pending)
        pending = upcoming


def _rope_tables(b, lp):
    half = ROPE_D // 2
    inv_freq = 1.0 / (ROPE_THETA ** (jnp.arange(0, ROPE_D, 2, dtype=F32) / ROPE_D))
    ang = jnp.arange(lp, dtype=jnp.int32).astype(F32)[:, None] * inv_freq[None, :]
    cos, sin = jnp.cos(ang), jnp.sin(ang)
    cos4 = jnp.tile(cos, (b, LANES // half))
    sin4 = jnp.tile(jnp.concatenate([-sin, sin], axis=1), (b, LANES // ROPE_D))
    return cos4, sin4


def _mla_layer(h3, j, p):
    b, lp, d = h3.shape
    m = b * lp
    hq = MLA_HEADS
    hw = NOPE_D + LANES
    bm = _row_tile(m, 384)
    w_in = p["mla_w_in"][j]
    w_in = jnp.concatenate([w_in, w_in[:, Q_LORA + KV_LORA:]], axis=1).astype(BF16)
    w_uq = p["mla_w_uq"][j].reshape(Q_LORA, hq, NOPE_D + ROPE_D)
    w_uq = jnp.concatenate([w_uq, w_uq[:, :, NOPE_D:]], axis=2).reshape(Q_LORA, hq * hw).astype(BF16)
    w_ukv = p["mla_w_ukv"][j].astype(BF16)
    cos4, sin4 = _rope_tables(b, lp)
    scale = float((NOPE_D + ROPE_D) ** -0.5)

    def full(a):
        return pl.BlockSpec(a.shape, lambda i: (0, 0))

    consts = [p["norm_mix_pre"].reshape(1, d), w_in, p["mla_q_norm"][j].reshape(1, Q_LORA), w_uq,
              p["mla_kv_norm"][j].reshape(1, KV_LORA), w_ukv]
    q, kv, kpe = pl.pallas_call(
        functools.partial(_mla_proj_kernel, scale=scale),
        out_shape=(jax.ShapeDtypeStruct((m, hq * hw), BF16),
                   jax.ShapeDtypeStruct((m, hq * (NOPE_D + V_D)), BF16),
                   jax.ShapeDtypeStruct((m, LANES), BF16)),
        grid=(m // bm,),
        in_specs=[pl.BlockSpec((bm, d), lambda i: (i, 0))] + [full(a) for a in consts]
                 + [pl.BlockSpec((bm, LANES), lambda i: (i, 0))] * 2,
        out_specs=(pl.BlockSpec((bm, hq * hw), lambda i: (i, 0)),
                   pl.BlockSpec((bm, hq * (NOPE_D + V_D)), lambda i: (i, 0)),
                   pl.BlockSpec((bm, LANES), lambda i: (i, 0))),
        compiler_params=_cparams(("parallel",)),
        name="mla_proj",
    )(h3.reshape(m, d), *consts, cos4, sin4)

    o = pl.pallas_call(
        _mla_attn_kernel,
        out_shape=jax.ShapeDtypeStruct((b, lp, hq * V_D), BF16),
        grid=(b, hq // ATTN_HEADS_PER_STEP),
        in_specs=[pl.BlockSpec((1, lp, ATTN_HEADS_PER_STEP * hw), lambda bi, h: (bi, 0, h)),
                  pl.BlockSpec((1, lp, ATTN_HEADS_PER_STEP * (NOPE_D + V_D)), lambda bi, h: (bi, 0, h)),
                  pl.BlockSpec((1, lp, LANES), lambda bi, h: (bi, 0, 0))],
        out_specs=pl.BlockSpec((1, lp, ATTN_HEADS_PER_STEP * V_D), lambda bi, h: (bi, 0, h)),
        scratch_shapes=[pltpu.VMEM((ATTN_HEADS_PER_STEP, lp, hw), BF16),
                        pltpu.VMEM((ATTN_HEADS_PER_STEP, lp, V_D + LANES), BF16)],
        compiler_params=_cparams(("parallel", "parallel")),
        name="mla_attention",
    )(q.reshape(b, lp, hq * hw), kv.reshape(b, lp, hq * (NOPE_D + V_D)), kpe.reshape(b, lp, LANES))

    h2 = _proj_residual(o.reshape(m, hq * V_D), p["mla_w_o"], j, h3.reshape(m, d), p["norm_mix_post"])
    return h2.reshape(b, lp, d)


POOL_HALO = 16


def _pool_kernel(h_ref, halo_ref, gpre_ref, w_ref, sc_ref, gpost_ref, o_ref):
    i = pl.program_id(1)
    tl, d = h_ref.shape[1], h_ref.shape[2]
    gw = d // len(POOL_WINDOWS)
    h = h_ref[0]
    hn = _rms(h, gpre_ref[...])
    hist = jnp.where(i == 0, 0.0, _rms(halo_ref[0], gpre_ref[...]))
    ext = jnp.concatenate([hist, hn], axis=0)
    t = i * tl + lax.broadcasted_iota(jnp.int32, (tl, gw), 0)
    ys = []
    for gi, win in enumerate(POOL_WINDOWS):
        cols = slice(gi * gw, (gi + 1) * gw)
        s = ext[:, cols]
        step = 1
        while step < win:
            s = s + pltpu.roll(s, step, axis=0)
            step *= 2
        cnt = jnp.minimum(t + 1, win).astype(F32)
        pooled = (s[POOL_HALO:] / cnt - hn[:, cols]).astype(BF16)
        ys.append(_dot(pooled, w_ref[gi]) * sc_ref[:, cols])
    o_ref[0] = h + _rms(jnp.concatenate(ys, axis=1), gpost_ref[...])


def _pool_layer(h3, j, p):
    b, lp, d = h3.shape
    assert max(POOL_WINDOWS) <= POOL_HALO
    tl = lp // 4 if lp % (4 * POOL_HALO) == 0 else lp
    halo_blocks = tl // POOL_HALO

    def full(a):
        return pl.BlockSpec(a.shape, lambda bi, i, nd=a.ndim: (0,) * nd)

    consts = [p["norm_mix_pre"].reshape(1, d), p["pool_w"][j].astype(BF16), p["pool_scale"][j].reshape(1, d),
              p["norm_mix_post"].reshape(1, d)]
    return pl.pallas_call(
        _pool_kernel,
        out_shape=jax.ShapeDtypeStruct((b, lp, d), F32),
        grid=(b, lp // tl),
        in_specs=[pl.BlockSpec((1, tl, d), lambda bi, i: (bi, i, 0)),
                  pl.BlockSpec((1, POOL_HALO, d), lambda bi, i: (bi, jnp.maximum(i * halo_blocks - 1, 0), 0))]
                 + [full(a) for a in consts],
        out_specs=pl.BlockSpec((1, tl, d), lambda bi, i: (bi, i, 0)),
        compiler_params=_cparams(("parallel", "parallel")),
        name="pool_layer",
    )(h3, h3, *consts)


def kernel(x, meta_tokens, norm_mix_pre, norm_mix_post, norm_ffn_pre, norm_ffn_post, ffn_w_gate, ffn_w_up, ffn_w_down, rwkv_mix, rwkv_w0, rwkv_w1, rwkv_w2, rwkv_a0, rwkv_a1, rwkv_a2, rwkv_v0, rwkv_v1, rwkv_v2, rwkv_g1, rwkv_g2, rwkv_k_k, rwkv_k_a, rwkv_r_k, rwkv_w_rkv, rwkv_w_o, rwkv_lnx_g, rwkv_lnx_b, mla_w_in, mla_q_norm, mla_w_uq, mla_kv_norm, mla_w_ukv, mla_w_o, pool_w, pool_scale):
    b, seq, d = x.shape
    depth = norm_mix_pre.shape[0]
    ltot = N_META + seq
    lp = -(-ltot // SEQ_ALIGN) * SEQ_ALIGN
    meta = jnp.broadcast_to(meta_tokens.astype(x.dtype)[None], (b, N_META, d))
    h = jnp.concatenate([meta, x, jnp.zeros((b, lp - ltot, d), x.dtype)], axis=1)
    shared = dict(rwkv_mix=rwkv_mix, rwkv_w0=rwkv_w0, rwkv_w1=rwkv_w1, rwkv_w2=rwkv_w2, rwkv_a0=rwkv_a0,
                  rwkv_a1=rwkv_a1, rwkv_a2=rwkv_a2, rwkv_v0=rwkv_v0, rwkv_v1=rwkv_v1, rwkv_v2=rwkv_v2,
                  rwkv_g1=rwkv_g1, rwkv_g2=rwkv_g2, rwkv_k_k=rwkv_k_k, rwkv_k_a=rwkv_k_a, rwkv_r_k=rwkv_r_k,
                  rwkv_w_rkv=rwkv_w_rkv.astype(BF16), rwkv_w_o=rwkv_w_o.astype(BF16),
                  rwkv_lnx_g=rwkv_lnx_g, rwkv_lnx_b=rwkv_lnx_b,
                  mla_w_in=mla_w_in, mla_q_norm=mla_q_norm, mla_w_uq=mla_w_uq, mla_kv_norm=mla_kv_norm,
                  mla_w_ukv=mla_w_ukv, mla_w_o=mla_w_o.astype(BF16), pool_w=pool_w, pool_scale=pool_scale)
    w_gate, w_up, w_down = ffn_w_gate.astype(BF16), ffn_w_up.astype(BF16), ffn_w_down.astype(BF16)
    v_first = None
    for i in range(depth):
        p = dict(shared, norm_mix_pre=norm_mix_pre[i], norm_mix_post=norm_mix_post[i])
        kind, j = i % 3, i // 3
        if kind == 0:
            h, v_first = _rwkv_layer(h, j, v_first, p)
        elif kind == 1:
            h = _mla_layer(h, j, p)
        else:
            h = _pool_layer(h, j, p)
        if i == depth - 1:
            return _ffn_window(h, N_META, seq, norm_ffn_pre[i], norm_ffn_post[i], w_gate, w_up, w_down, i)
        h = _ffn(h.reshape(b * lp, d), norm_ffn_pre[i], norm_ffn_post[i], w_gate, w_up, w_down, i).reshape(b, lp, d)
```

```python
import functools

import jax
import jax.numpy as jnp
from jax import lax
from jax.experimental import pallas as pl
from jax.experimental.pallas import tpu as pltpu

F32 = jnp.float32
BF16 = jnp.bfloat16

N_META = 16
RMS_EPS = 1e-6
RWKV_HEAD = 64
LNX_EPS = 1e-5 * RWKV_HEAD
MLA_HEADS = 16
Q_LORA = 512
KV_LORA = 512
NOPE_D = 128
ROPE_D = 64
V_D = 128
ROPE_THETA = 10000.0
POOL_WINDOWS = (2, 4, 8, 16)

LANES = 128
WKV_CHUNK = 64
SEQ_ALIGN = WKV_CHUNK
WKV_HEADS_PER_GROUP = 4
WKV_GROUP = WKV_HEADS_PER_GROUP * RWKV_HEAD
WKV_GROUPS_PER_STEP = 8
WKV_CHUNKS_PER_STEP = 3
ATTN_Q_TILE = 256
ATTN_HEADS_PER_STEP = 4
V7X_VMEM_BYTES = 64 * 1024 * 1024
VMEM_LIMIT = V7X_VMEM_BYTES * 7 // 8


def _cparams(sem):
    return pltpu.CompilerParams(dimension_semantics=sem, vmem_limit_bytes=VMEM_LIMIT)


def _dot(a, b):
    return jnp.dot(a, b, preferred_element_type=F32)


def _dot_nt(a, b):
    return lax.dot_general(a, b, (((1,), (1,)), ((), ())), preferred_element_type=F32)


def _dot_tn(a, b):
    return lax.dot_general(a, b, (((0,), (0,)), ((), ())), preferred_element_type=F32)


def _rms(x, g, eps=RMS_EPS):
    ms = jnp.mean(x * x, axis=-1, keepdims=True)
    return x * lax.rsqrt(ms + eps) * g


def _sigmoid(x):
    return 1.0 / (1.0 + jnp.exp(-x))


def _row_tile(m, cap=768):
    for t in (1056, 1024, 768, 512, 384, 256, 128, 64, 32, 16):
        if t <= cap and m % t == 0:
            return t
    raise ValueError(f"unsupported row count {m}")


def _proj_residual_kernel(x_ref, w_ref, h_ref, g_ref, o_ref):
    half = o_ref.shape[0] // 2
    for r0 in (0, half):
        rs = slice(r0, r0 + half)
        o_ref[rs, :] = h_ref[rs, :] + _rms(_dot(x_ref[rs, :], w_ref[...]), g_ref[...])


def _proj_residual(x2, w_all, layer, h2, g):
    m, k = x2.shape
    d = w_all.shape[2]
    bm = _row_tile(m)
    return pl.pallas_call(
        _proj_residual_kernel,
        out_shape=jax.ShapeDtypeStruct((m, d), F32),
        grid=(m // bm,),
        in_specs=[pl.BlockSpec((bm, k), lambda i: (i, 0)),
                  pl.BlockSpec((None, k, d), lambda i: (layer, 0, 0)),
                  pl.BlockSpec((bm, d), lambda i: (i, 0)),
                  pl.BlockSpec((1, d), lambda i: (0, 0))],
        out_specs=pl.BlockSpec((bm, d), lambda i: (i, 0)),
        compiler_params=_cparams(("parallel",)),
        name="proj_residual",
    )(x2, w_all, h2, g.reshape(1, d))


FFN_HIDDEN_TILE = 512
FFN_OUT_TILE = 512
FFN_ROW_SPLIT = 2
FFN_ROW_TILE_CAP = 1056


def _ffn_kernel(h_ref, gpre_ref, gpost_ref, wg_ref, wu_ref, wd_ref, o_ref, xn_ref):
    f = pl.program_id(1)
    last_f = pl.num_programs(1) - 1
    bm, d = o_ref.shape
    rows = bm // FFN_ROW_SPLIT

    def step(first, last):
        for r0 in range(0, bm, rows):
            rs = slice(r0, r0 + rows)
            if first:
                xn_ref[rs, :] = _rms(h_ref[rs, :], gpre_ref[...]).astype(BF16)
            x = xn_ref[rs, :]
            gate = _dot(x, wg_ref[...])
            up = _dot(x, wu_ref[...])
            mid = (gate * _sigmoid(gate) * up).astype(BF16)
            for n0 in range(0, d, FFN_OUT_TILE):
                ns = slice(n0, n0 + FFN_OUT_TILE)
                part = _dot(mid, wd_ref[:, ns])
                if first:
                    o_ref[rs, ns] = part
                else:
                    o_ref[rs, ns] += part
            if last:
                o_ref[rs, :] = h_ref[rs, :] + _rms(o_ref[rs, :], gpost_ref[...])

    pl.when(f == 0)(lambda: step(True, False))
    pl.when(jnp.logical_and(f > 0, f < last_f))(lambda: step(False, False))
    pl.when(f == last_f)(lambda: step(False, True))


def _ffn(h2, gpre, gpost, wg, wu, wd, layer):
    m, d = h2.shape
    f = wg.shape[2]
    bm = _row_tile(m, FFN_ROW_TILE_CAP)
    bf = FFN_HIDDEN_TILE
    assert f % bf == 0 and f // bf >= 2
    assert d % FFN_OUT_TILE == 0 and bm % (16 * FFN_ROW_SPLIT) == 0
    return pl.pallas_call(
        _ffn_kernel,
        out_shape=jax.ShapeDtypeStruct((m, d), F32),
        grid=(m // bm, f // bf),
        in_specs=[pl.BlockSpec((bm, d), lambda i, j: (i, 0)),
                  pl.BlockSpec((1, d), lambda i, j: (0, 0)),
                  pl.BlockSpec((1, d), lambda i, j: (0, 0)),
                  pl.BlockSpec((None, d, bf), lambda i, j: (layer, 0, j)),
                  pl.BlockSpec((None, d, bf), lambda i, j: (layer, 0, j)),
                  pl.BlockSpec((None, bf, d), lambda i, j: (layer, j, 0))],
        out_specs=pl.BlockSpec((bm, d), lambda i, j: (i, 0)),
        scratch_shapes=[pltpu.VMEM((bm, d), BF16)],
        compiler_params=_cparams(("parallel", "arbitrary")),
        name="ffn",
    )(h2, gpre.reshape(1, d), gpost.reshape(1, d), wg, wu, wd)


def _ffn_window(h3, start, count, gpre, gpost, wg, wu, wd, layer):
    b, _, d = h3.shape
    f = wg.shape[2]
    bm = next(t for t in (1024, 512, 256, 128, 64, 32) if count % t == 0)
    bf = FFN_HIDDEN_TILE
    assert f % bf == 0 and f // bf >= 2
    assert d % FFN_OUT_TILE == 0 and bm % (16 * FFN_ROW_SPLIT) == 0 and start % 8 == 0
    def window_kernel(h_ref, *rest):
        _ffn_kernel(h_ref.at[0], *rest)

    return pl.pallas_call(
        window_kernel,
        out_shape=jax.ShapeDtypeStruct((b, count, d), F32),
        grid=(b * (count // bm), f // bf),
        in_specs=[pl.BlockSpec((pl.Element(1), pl.Element(bm), pl.Element(d)),
                               lambda i, j: (i // (count // bm),
                                             pl.multiple_of(start + (i % (count // bm)) * bm, 8), 0)),
                  pl.BlockSpec((1, d), lambda i, j: (0, 0)),
                  pl.BlockSpec((1, d), lambda i, j: (0, 0)),
                  pl.BlockSpec((None, d, bf), lambda i, j: (layer, 0, j)),
                  pl.BlockSpec((None, d, bf), lambda i, j: (layer, 0, j)),
                  pl.BlockSpec((None, bf, d), lambda i, j: (layer, j, 0))],
        out_specs=pl.BlockSpec((None, bm, d), lambda i, j: (i // (count // bm), i % (count // bm), 0)),
        scratch_shapes=[pltpu.VMEM((bm, d), BF16)],
        compiler_params=_cparams(("parallel", "arbitrary")),
        name="ffn_window",
    )(h3, gpre.reshape(1, d), gpost.reshape(1, d), wg, wu, wd)


def _rkv_kernel(hn_ref, xx_ref, mix_ref, w_ref, o_ref):
    half = o_ref.shape[0] // 2
    w = w_ref[...].astype(BF16)
    for r0 in (0, half):
        rs = slice(r0, r0 + half)
        xs = (hn_ref[rs, :].astype(F32) + xx_ref[rs, :].astype(F32) * mix_ref[0]).astype(BF16)
        o_ref[rs, :] = _dot(xs, w).astype(BF16)


RKV_COL_SPLIT = 2


def _rkv_proj(hn2, xx2, mix3, w_rkv_all, layer):
    m, d = hn2.shape
    bm = _row_tile(m, 1056)
    dn = d // RKV_COL_SPLIT
    return pl.pallas_call(
        _rkv_kernel,
        out_shape=jax.ShapeDtypeStruct((m, 3 * d), BF16),
        grid=(m // bm, 3, RKV_COL_SPLIT),
        in_specs=[pl.BlockSpec((bm, d), lambda i, s, n: (i, 0)),
                  pl.BlockSpec((bm, d), lambda i, s, n: (i, 0)),
                  pl.BlockSpec((1, 1, d), lambda i, s, n: (s, 0, 0)),
                  pl.BlockSpec((None, None, d, dn), lambda i, s, n: (layer, s, 0, n))],
        out_specs=pl.BlockSpec((bm, dn), lambda i, s, n: (i, s * RKV_COL_SPLIT + n)),
        compiler_params=_cparams(("parallel", "parallel", "parallel")),
        name="rkv_proj",
    )(hn2, xx2, mix3, w_rkv_all)


LORA_SMALL_TILE = 256


def _shift_lora_kernel(*refs, has_vres, w_rank):
    if has_vres:
        (h_ref, halo_ref, gpre_ref, wf_ref, w0_ref, w2_ref, a0_ref, a2_ref, g2_ref, v0_ref, v2_ref,
         hn_ref, xx_ref, ld_ref, a_ref, g_ref, vg_ref) = refs
    else:
        (h_ref, halo_ref, gpre_ref, wf_ref, w0_ref, w2_ref, a0_ref, a2_ref, g2_ref,
         hn_ref, xx_ref, ld_ref, a_ref, g_ref) = refs
    gpre = gpre_ref[...]
    hn = _rms(h_ref[0], gpre)
    prev = jnp.where(pl.program_id(1) == 0, 0.0, _rms(halo_ref[0], gpre)[7:8])
    row = lax.broadcasted_iota(jnp.int32, hn.shape, 0)
    shifted = jnp.where(row == 0, prev, pltpu.roll(hn, 1, axis=0))
    hn_b = hn.astype(BF16)
    xx_b = (shifted - hn).astype(BF16)
    hn_ref[0] = hn_b
    xx_ref[0] = xx_b
    t = _dot(jnp.concatenate([hn_b, xx_b], axis=1), wf_ref[...])
    small = t[:, :LORA_SMALL_TILE]
    lane = lax.broadcasted_iota(jnp.int32, (1, LORA_SMALL_TILE), 1)
    act = jnp.where(lane < w_rank, jnp.tanh(small), small).astype(BF16)
    z = w0_ref[...] + _dot(act, w2_ref[...])
    ld_ref[0] = -jnp.exp(F32(-0.5)) * _sigmoid(z)
    a_ref[0] = _sigmoid(a0_ref[...] + _dot(act, a2_ref[...])).astype(BF16)
    g_ref[0] = _dot(_sigmoid(t[:, LORA_SMALL_TILE:]).astype(BF16), g2_ref[...]).astype(BF16)
    if has_vres:
        vg_ref[0] = _sigmoid(v0_ref[...] + _dot(act, v2_ref[...])).astype(BF16)


def _shift_lora(h3, gpre, mix, w0, w1, w2, a0, a1, a2, g1, g2, vres):
    b, lp, d = h3.shape
    tl = next(t for t in range(min(lp, 384), 0, -16) if lp % t == 0)
    halo_blocks = tl // 8
    has_vres = vres is not None
    downs = [(w1, mix[0]), (a1, mix[1])] + ([(vres[1], mix[3])] if has_vres else [])
    ups = [w2, a2] + ([vres[2]] if has_vres else [])
    ranks = [w.shape[1] for w, _ in downs]
    assert sum(ranks) <= LORA_SMALL_TILE
    pad = LORA_SMALL_TILE - sum(ranks)
    plain = jnp.concatenate([w for w, _ in downs] + [jnp.zeros((d, pad), F32), g1], axis=1)
    mixed = jnp.concatenate([w * mx[:, None] for w, mx in downs] + [jnp.zeros((d, pad), F32), g1 * mix[2][:, None]],
                            axis=1)
    w_first = jnp.concatenate([plain, mixed], axis=0).astype(BF16)

    def placed(up, offset):
        return jnp.pad(up, ((offset, LORA_SMALL_TILE - offset - up.shape[0]), (0, 0))).astype(BF16)

    offsets = [sum(ranks[:i]) for i in range(len(ranks))]
    ups = [placed(u, o) for u, o in zip(ups, offsets)]
    consts = [gpre.reshape(1, d), w_first, w0.reshape(1, d), ups[0], a0.reshape(1, d), ups[1], g2.astype(BF16)]
    if has_vres:
        consts += [vres[0].reshape(1, d), ups[2]]
    row = pl.BlockSpec((1, tl, d), lambda bi, i: (bi, i, 0))
    halo = pl.BlockSpec((1, 8, d), lambda bi, i: (bi, jnp.maximum(i * halo_blocks - 1, 0), 0))

    def full(a):
        return pl.BlockSpec(a.shape, lambda bi, i: (0, 0))

    dtypes = [BF16, BF16, F32, BF16, BF16] + ([BF16] if has_vres else [])
    return pl.pallas_call(
        functools.partial(_shift_lora_kernel, has_vres=has_vres, w_rank=ranks[0]),
        out_shape=tuple(jax.ShapeDtypeStruct((b, lp, d), t) for t in dtypes),
        grid=(b, lp // tl),
        in_specs=[row, halo] + [full(a) for a in consts],
        out_specs=tuple(row for _ in dtypes),
        compiler_params=_cparams(("parallel", "parallel")),
        name="shift_lora",
    )(h3, h3, *consts)


def _split_dot(a_bf16, x):
    hi = x.astype(BF16)
    lo = (x - hi.astype(F32)).astype(BF16)
    return _dot(a_bf16, hi) + _dot(a_bf16, lo)


def _wkv_kernel(*refs, n_chunks, n_groups, has_vres):
    if has_vres:
        (r_ref, k_ref, v_ref, ld_ref, a_ref, g_ref, vf_ref, vg_ref,
         kk_ref, ka_ref, rk_ref, lg_ref, lb_ref, mbd_ref, msk_ref, tri_ref, o_ref, s_ref) = refs
    else:
        (r_ref, k_ref, v_ref, ld_ref, a_ref, g_ref,
         kk_ref, ka_ref, rk_ref, lg_ref, lb_ref, mbd_ref, msk_ref, tri_ref, o_ref, s_ref) = refs
    C, G, HG, N = WKV_CHUNK, WKV_GROUP, WKV_HEADS_PER_GROUP, RWKV_HEAD

    @pl.when(pl.program_id(2) == 0)
    def _():
        s_ref[...] = jnp.zeros_like(s_ref)

    def bd(x):
        xb = x.astype(BF16)
        return jnp.concatenate([xb] * HG, axis=0) * mbd_ref[...]

    gs = range(n_groups)
    lanes = [slice(gi * G, (gi + 1) * G) for gi in gs]

    def group_rows(x, n):
        return [x[i * n:(i + 1) * n] for i in gs]

    def chunk(j, carry):
        rows = pl.ds(pl.multiple_of(j * C, C), C)
        strict, incl, eye = msk_ref[0], msk_ref[1], msk_ref[2]
        r = [r_ref[0, rows, lanes[i]].astype(F32) for i in gs]
        k = [k_ref[0, rows, lanes[i]].astype(F32) for i in gs]
        v = [v_ref[0, rows, lanes[i]].astype(F32) for i in gs]
        a_s = [a_ref[0, rows, lanes[i]].astype(F32) for i in gs]
        if has_vres:
            v = [v[i] + (vf_ref[0, rows, lanes[i]].astype(F32) - v[i]) * vg_ref[0, rows, lanes[i]].astype(F32)
                 for i in gs]
        kk = [k[i] * kk_ref[:, lanes[i]] for i in gs]
        kp = [k[i] * (1.0 + (a_s[i] - 1.0) * ka_ref[:, lanes[i]]) for i in gs]
        lw = [_split_dot(tri_ref[...], ld_ref[0, rows, lanes[i]]) for i in gs]
        sums = group_rows(_dot(jnp.concatenate(
            [jnp.concatenate([kk[i] * kk[i], r[i] * kp[i] * rk_ref[:, lanes[i]]], axis=0).astype(BF16) for i in gs],
            axis=0), mbd_ref[...]), 2 * C)
        kk = [kk[i] / jnp.maximum(jnp.sqrt(sums[i][:C]), 1e-12) for i in gs]
        bb = [kk[i] * a_s[i] for i in gs]
        lw_last = [lw[i][C - 1:C, :] for i in gs]
        w_inv = [jnp.exp(-lw[i]) for i in gs]
        w_tail = [jnp.exp(lw_last[i] - lw[i]) for i in gs]
        a_t = [(-kk[i] * jnp.exp(lw[i] - ld_ref[0, rows, lanes[i]])).astype(BF16) for i in gs]
        r_t = [(r[i] * jnp.exp(lw[i])).astype(BF16) for i in gs]

        sc = [_dot_nt(jnp.concatenate([a_t[i], r_t[i]], axis=0),
                      jnp.concatenate([bd(bb[i] * w_inv[i]), bd(kp[i] * w_inv[i])], axis=0)) for i in gs]
        l_ab = [sc[i][:C, :G] * strict for i in gs]
        l_rb = [(sc[i][C:, :G] * incl).astype(BF16) for i in gs]
        l_k = [jnp.concatenate([sc[i][:C, G:] * strict, sc[i][C:, G:] * incl], axis=0).astype(BF16) for i in gs]
        v_bd = [bd(v[i]) for i in gs]
        lkv = [_dot(l_k[i], v_bd[i]) for i in gs]

        p = [l_ab[i].astype(BF16) for i in gs]
        t = [eye + l_ab[i] for i in gs]
        power = 1
        while power < C:
            if power == 1:
                p = [_dot(p[i], bd(p[i])).astype(BF16) for i in gs]
            elif power * 2 < C:
                res = [_dot(jnp.concatenate([p[i], t[i].astype(BF16)], axis=0), bd(p[i])) for i in gs]
                p = [res[i][:C].astype(BF16) for i in gs]
                t = [t[i] + res[i][C:] for i in gs]
            else:
                t = [t[i] + _dot(t[i].astype(BF16), bd(p[i])) for i in gs]
            power *= 2

        ta = [_dot(t[i].astype(BF16), jnp.concatenate([bd(a_t[i]), bd(lkv[i][:C])], axis=1)) for i in gs]
        s0 = [s_ref[i] for i in gs]
        res = [_dot_nt(jnp.concatenate([ta[i][:, :G].astype(BF16), r_t[i]], axis=0), s0[i].astype(BF16)) for i in gs]
        u_bd = [bd(res[i][:C] + ta[i][:, G:]) for i in gs]
        y = [res[i][C:] + lkv[i][C:] + _dot(l_rb[i], u_bd[i]) for i in gs]
        uv_stack = [jnp.concatenate([u_bd[i], v_bd[i]], axis=0) for i in gs]
        for i in gs:
            s_ref[i] = s0[i] * jnp.exp(lw_last[i]) + _dot_tn(
                uv_stack[i], jnp.concatenate([bd(bb[i] * w_tail[i]), bd(kp[i] * w_tail[i])], axis=0))

        y_hi = [y[i].astype(BF16) for i in gs]
        y_lo = [(y[i] - y_hi[i].astype(F32)).astype(BF16) for i in gs]
        mu2 = group_rows(_dot(jnp.concatenate([jnp.concatenate([y_hi[i], y_lo[i]], axis=0) for i in gs], axis=0),
                              mbd_ref[...]), 2 * C)
        dy = [y[i] - (mu2[i][:C] + mu2[i][C:]) * (1.0 / N) for i in gs]
        var = group_rows(_dot(jnp.concatenate([(dy[i] * dy[i]).astype(BF16) for i in gs], axis=0), mbd_ref[...]), C)
        var = [var[i] * (1.0 / N) for i in gs]
        for i in gs:
            yn = dy[i] * lax.rsqrt(var[i] + LNX_EPS) * lg_ref[:, lanes[i]] + lb_ref[:, lanes[i]]
            o_ref[0, rows, lanes[i]] = ((yn + sums[i][C:] * v[i]) * g_ref[0, rows, lanes[i]].astype(F32)).astype(BF16)
        return carry

    lax.fori_loop(0, n_chunks, chunk, 0, unroll=True)


def _wkv_tables():
    C, G, N = WKV_CHUNK, WKV_GROUP, RWKV_HEAD
    idx = jnp.arange(G)
    mbd = (idx[:, None] // N == idx[None, :] // N).astype(BF16)
    t = jnp.arange(C)[:, None]
    s = (idx % N)[None, :]
    masks = jnp.stack([s < t, s <= t, s == t]).astype(F32)
    tri = (jnp.arange(C)[None, :] <= t).astype(BF16)
    return mbd, masks, tri


def _wkv(rkv, ld, a, g, rkv_first, vg, k_k, k_a, r_k, lnx_g, lnx_b):
    b, lp, d = ld.shape
    n_groups = next(n for n in (WKV_GROUPS_PER_STEP, 4, 2, 1) if d % (n * WKV_GROUP) == 0)
    width = n_groups * WKV_GROUP
    n_lane_blocks = d // width
    n_chunks = next(n for n in (WKV_CHUNKS_PER_STEP, 2, 1) if (lp // WKV_CHUNK) % n == 0)
    tl = n_chunks * WKV_CHUNK
    has_vres = rkv_first is not None

    def seq(col):
        return pl.BlockSpec((1, tl, width), lambda bi, hg, c: (bi, c, hg + col * n_lane_blocks))

    par = pl.BlockSpec((1, width), lambda bi, hg, c: (0, hg))
    args = [rkv, rkv, rkv, ld, a, g] + ([rkv_first, vg] if has_vres else [])
    specs = [seq(0), seq(1), seq(2), seq(0), seq(0), seq(0)] + ([seq(2), seq(0)] if has_vres else [])
    params = [p.reshape(1, d) for p in (k_k, k_a, r_k, lnx_g, lnx_b)]
    tables = _wkv_tables()
    table_specs = [pl.BlockSpec(t.shape, lambda bi, hg, c, nd=t.ndim: (0,) * nd) for t in tables]
    return pl.pallas_call(
        functools.partial(_wkv_kernel, n_chunks=n_chunks, n_groups=n_groups, has_vres=has_vres),
        out_shape=jax.ShapeDtypeStruct((b, lp, d), BF16),
        grid=(b, n_lane_blocks, lp // tl),
        in_specs=specs + [par] * len(params) + table_specs,
        out_specs=seq(0),
        scratch_shapes=[pltpu.VMEM((n_groups, WKV_GROUP, WKV_GROUP), F32)],
        compiler_params=_cparams(("parallel", "parallel", "arbitrary")),
        name="wkv7",
    )(*args, *params, *tables)


def _rwkv_layer(h3, j, rkv_first, p):
    b, lp, d = h3.shape
    m = b * lp
    mix = p["rwkv_mix"][j]
    vres = None if j == 0 else (p["rwkv_v0"][j - 1], p["rwkv_v1"][j - 1], p["rwkv_v2"][j - 1])
    outs = _shift_lora(h3, p["norm_mix_pre"], jnp.stack([mix[1], mix[4], mix[5], mix[3]]), p["rwkv_w0"][j],
                       p["rwkv_w1"][j], p["rwkv_w2"][j], p["rwkv_a0"][j], p["rwkv_a1"][j], p["rwkv_a2"][j],
                       p["rwkv_g1"][j], p["rwkv_g2"][j], vres)
    hn, xx, ld, a, g = outs[:5]
    vg = outs[5] if vres is not None else None
    rkv = _rkv_proj(hn.reshape(m, d), xx.reshape(m, d), jnp.stack([mix[0], mix[2], mix[3]]).reshape(3, 1, d),
                    p["rwkv_w_rkv"], j).reshape(b, lp, 3 * d)
    if vres is None:
        rkv_first = rkv
    yg = _wkv(rkv, ld, a, g, rkv_first if vres is not None else None, vg,
              p["rwkv_k_k"][j], p["rwkv_k_a"][j], p["rwkv_r_k"][j].reshape(d),
              p["rwkv_lnx_g"][j], p["rwkv_lnx_b"][j])
    h2 = _proj_residual(yg.reshape(m, d), p["rwkv_w_o"], j, h3.reshape(m, d), p["norm_mix_post"])
    return h2.reshape(b, lp, d), rkv_first


def _mla_proj_kernel(h_ref, g_ref, win_ref, qn_ref, wuq_ref, kvn_ref, wukv_ref, cos_ref, sin_ref,
                     q_ref, kv_ref, kpe_ref, *, scale):
    hn = _rms(h_ref[...], g_ref[...]).astype(BF16)
    lat = _dot(hn, win_ref[...])
    cq = _rms(lat[:, :Q_LORA], qn_ref[...]).astype(BF16)
    ckv = _rms(lat[:, Q_LORA:Q_LORA + KV_LORA], kvn_ref[...]).astype(BF16)
    cos, sin = cos_ref[...], sin_ref[...]

    def rope(x):
        return x * cos + pltpu.roll(x, ROPE_D // 2, axis=1) * sin

    keep = (lax.broadcasted_iota(jnp.int32, (1, LANES), 1) < ROPE_D).astype(F32)
    kpe_ref[...] = (rope(lat[:, Q_LORA + KV_LORA:]) * keep).astype(BF16)
    kv_ref[...] = _dot(ckv, wukv_ref[...]).astype(BF16)
    q = _dot(cq, wuq_ref[...])
    hw = NOPE_D + LANES
    for h in range(MLA_HEADS):
        q_ref[:, h * hw:h * hw + NOPE_D] = (q[:, h * hw:h * hw + NOPE_D] * scale).astype(BF16)
        q_ref[:, h * hw + NOPE_D:(h + 1) * hw] = (rope(q[:, h * hw + NOPE_D:(h + 1) * hw]) * scale).astype(BF16)


def _mla_attn_kernel(q_ref, kv_ref, kpe_ref, o_ref, kcat_ref, vone_ref):
    lp = q_ref.shape[1]
    hw, kvw = NOPE_D + LANES, NOPE_D + V_D
    for e in range(ATTN_HEADS_PER_STEP):
        kcat_ref[e, :, :NOPE_D] = kv_ref[0, :, e * kvw:e * kvw + NOPE_D]
        kcat_ref[e, :, NOPE_D:] = kpe_ref[0]
        vone_ref[e, :, :V_D] = kv_ref[0, :, e * kvw + NOPE_D:(e + 1) * kvw]
        vone_ref[e, :, V_D:] = jnp.ones((lp, LANES), BF16)
    items = [(e, q0, min(q0 + ATTN_Q_TILE, lp)) for q0 in range(0, lp, ATTN_Q_TILE)
             for e in range(ATTN_HEADS_PER_STEP)]

    def scores(e, q0, q1):
        q = q_ref[0, q0:q1, e * hw:(e + 1) * hw]
        sd = _dot_nt(q, kcat_ref[e, q0:q1, :])
        sp = _dot_nt(q, kcat_ref[e, :q0, :]) if q0 > 0 else None
        return sd, sp

    def finish(e, q0, q1, sd, sp):
        row = lax.broadcasted_iota(jnp.int32, sd.shape, 0)
        col = lax.broadcasted_iota(jnp.int32, sd.shape, 1)
        sd = jnp.where(col <= row, sd, -1e30)
        mx = jnp.max(sd, axis=-1, keepdims=True)
        if sp is not None:
            mx = jnp.maximum(mx, jnp.max(sp, axis=-1, keepdims=True))
        acc = _dot(jnp.exp(sd - mx).astype(BF16), vone_ref[e, q0:q1, :])
        if sp is not None:
            acc = acc + _dot(jnp.exp(sp - mx).astype(BF16), vone_ref[e, :q0, :])
        o_ref[0, q0:q1, e * V_D:(e + 1) * V_D] = (acc[:, :V_D] / acc[:, V_D:V_D + 1]).astype(BF16)

    pending = scores(*items[0])
    for idx, item in enumerate(items):
        upcoming = scores(*items[idx + 1]) if idx + 1 < len(items) else None
        finish(*item, *pending)
        pending = upcoming


def _rope_tables(b, lp):
    half = ROPE_D // 2
    inv_freq = 1.0 / (ROPE_THETA ** (jnp.arange(0, ROPE_D, 2, dtype=F32) / ROPE_D))
    ang = jnp.arange(lp, dtype=jnp.int32).astype(F32)[:, None] * inv_freq[None, :]
    cos, sin = jnp.cos(ang), jnp.sin(ang)
    cos4 = jnp.tile(cos, (b, LANES // half))
    sin4 = jnp.tile(jnp.concatenate([-sin, sin], axis=1), (b, LANES // ROPE_D))
    return cos4, sin4


def _mla_layer(h3, j, p):
    b, lp, d = h3.shape
    m = b * lp
    hq = MLA_HEADS
    hw = NOPE_D + LANES
    bm = _row_tile(m, 384)
    w_in = p["mla_w_in"][j]
    w_in = jnp.concatenate([w_in, w_in[:, Q_LORA + KV_LORA:]], axis=1).astype(BF16)
    w_uq = p["mla_w_uq"][j].reshape(Q_LORA, hq, NOPE_D + ROPE_D)
    w_uq = jnp.concatenate([w_uq, w_uq[:, :, NOPE_D:]], axis=2).reshape(Q_LORA, hq * hw).astype(BF16)
    w_ukv = p["mla_w_ukv"][j].astype(BF16)
    cos4, sin4 = _rope_tables(b, lp)
    scale = float((NOPE_D + ROPE_D) ** -0.5)

    def full(a):
        return pl.BlockSpec(a.shape, lambda i: (0, 0))

    consts = [p["norm_mix_pre"].reshape(1, d), w_in, p["mla_q_norm"][j].reshape(1, Q_LORA), w_uq,
              p["mla_kv_norm"][j].reshape(1, KV_LORA), w_ukv]
    q, kv, kpe = pl.pallas_call(
        functools.partial(_mla_proj_kernel, scale=scale),
        out_shape=(jax.ShapeDtypeStruct((m, hq * hw), BF16),
                   jax.ShapeDtypeStruct((m, hq * (NOPE_D + V_D)), BF16),
                   jax.ShapeDtypeStruct((m, LANES), BF16)),
        grid=(m // bm,),
        in_specs=[pl.BlockSpec((bm, d), lambda i: (i, 0))] + [full(a) for a in consts]
                 + [pl.BlockSpec((bm, LANES), lambda i: (i, 0))] * 2,
        out_specs=(pl.BlockSpec((bm, hq * hw), lambda i: (i, 0)),
                   pl.BlockSpec((bm, hq * (NOPE_D + V_D)), lambda i: (i, 0)),
                   pl.BlockSpec((bm, LANES), lambda i: (i, 0))),
        compiler_params=_cparams(("parallel",)),
        name="mla_proj",
    )(h3.reshape(m, d), *consts, cos4, sin4)

    o = pl.pallas_call(
        _mla_attn_kernel,
        out_shape=jax.ShapeDtypeStruct((b, lp, hq * V_D), BF16),
        grid=(b, hq // ATTN_HEADS_PER_STEP),
        in_specs=[pl.BlockSpec((1, lp, ATTN_HEADS_PER_STEP * hw), lambda bi, h: (bi, 0, h)),
                  pl.BlockSpec((1, lp, ATTN_HEADS_PER_STEP * (NOPE_D + V_D)), lambda bi, h: (bi, 0, h)),
                  pl.BlockSpec((1, lp, LANES), lambda bi, h: (bi, 0, 0))],
        out_specs=pl.BlockSpec((1, lp, ATTN_HEADS_PER_STEP * V_D), lambda bi, h: (bi, 0, h)),
        scratch_shapes=[pltpu.VMEM((ATTN_HEADS_PER_STEP, lp, hw), BF16),
                        pltpu.VMEM((ATTN_HEADS_PER_STEP, lp, V_D + LANES), BF16)],
        compiler_params=_cparams(("parallel", "parallel")),
        name="mla_attention",
    )(q.reshape(b, lp, hq * hw), kv.reshape(b, lp, hq * (NOPE_D + V_D)), kpe.reshape(b, lp, LANES))

    h2 = _proj_residual(o.reshape(m, hq * V_D), p["mla_w_o"], j, h3.reshape(m, d), p["norm_mix_post"])
    return h2.reshape(b, lp, d)


POOL_HALO = 16


def _pool_kernel(h_ref, halo_ref, gpre_ref, w_ref, sc_ref, gpost_ref, o_ref):
    i = pl.program_id(1)
    tl, d = h_ref.shape[1], h_ref.shape[2]
    gw = d // len(POOL_WINDOWS)
    h = h_ref[0]
    hn = _rms(h, gpre_ref[...])
    hist = jnp.where(i == 0, 0.0, _rms(halo_ref[0], gpre_ref[...]))
    ext = jnp.concatenate([hist, hn], axis=0)
    t = i * tl + lax.broadcasted_iota(jnp.int32, (tl, gw), 0)
    ys = []
    for gi, win in enumerate(POOL_WINDOWS):
        cols = slice(gi * gw, (gi + 1) * gw)
        s = ext[:, cols]
        step = 1
        while step < win:
            s = s + pltpu.roll(s, step, axis=0)
            step *= 2
        cnt = jnp.minimum(t + 1, win).astype(F32)
        pooled = (s[POOL_HALO:] / cnt - hn[:, cols]).astype(BF16)
        ys.append(_dot(pooled, w_ref[gi]) * sc_ref[:, cols])
    o_ref[0] = h + _rms(jnp.concatenate(ys, axis=1), gpost_ref[...])


def _pool_layer(h3, j, p):
    b, lp, d = h3.shape
    assert max(POOL_WINDOWS) <= POOL_HALO
    tl = lp // 4 if lp % (4 * POOL_HALO) == 0 else lp
    halo_blocks = tl // POOL_HALO

    def full(a):
        return pl.BlockSpec(a.shape, lambda bi, i, nd=a.ndim: (0,) * nd)

    consts = [p["norm_mix_pre"].reshape(1, d), p["pool_w"][j].astype(BF16), p["pool_scale"][j].reshape(1, d),
              p["norm_mix_post"].reshape(1, d)]
    return pl.pallas_call(
        _pool_kernel,
        out_shape=jax.ShapeDtypeStruct((b, lp, d), F32),
        grid=(b, lp // tl),
        in_specs=[pl.BlockSpec((1, tl, d), lambda bi, i: (bi, i, 0)),
                  pl.BlockSpec((1, POOL_HALO, d), lambda bi, i: (bi, jnp.maximum(i * halo_blocks - 1, 0), 0))]
                 + [full(a) for a in consts],
        out_specs=pl.BlockSpec((1, tl, d), lambda bi, i: (bi, i, 0)),
        compiler_params=_cparams(("parallel", "parallel")),
        name="pool_layer",
    )(h3, h3, *consts)


def kernel(x, meta_tokens, norm_mix_pre, norm_mix_post, norm_ffn_pre, norm_ffn_post, ffn_w_gate, ffn_w_up, ffn_w_down, rwkv_mix, rwkv_w0, rwkv_w1, rwkv_w2, rwkv_a0, rwkv_a1, rwkv_a2, rwkv_v0, rwkv_v1, rwkv_v2, rwkv_g1, rwkv_g2, rwkv_k_k, rwkv_k_a, rwkv_r_k, rwkv_w_rkv, rwkv_w_o, rwkv_lnx_g, rwkv_lnx_b, mla_w_in, mla_q_norm, mla_w_uq, mla_kv_norm, mla_w_ukv, mla_w_o, pool_w, pool_scale):
    b, seq, d = x.shape
    depth = norm_mix_pre.shape[0]
    ltot = N_META + seq
    lp = -(-ltot // SEQ_ALIGN) * SEQ_ALIGN
    meta = jnp.broadcast_to(meta_tokens.astype(x.dtype)[None], (b, N_META, d))
    h = jnp.concatenate([meta, x, jnp.zeros((b, lp - ltot, d), x.dtype)], axis=1)
    shared = dict(rwkv_mix=rwkv_mix, rwkv_w0=rwkv_w0, rwkv_w1=rwkv_w1, rwkv_w2=rwkv_w2, rwkv_a0=rwkv_a0,
                  rwkv_a1=rwkv_a1, rwkv_a2=rwkv_a2, rwkv_v0=rwkv_v0, rwkv_v1=rwkv_v1, rwkv_v2=rwkv_v2,
                  rwkv_g1=rwkv_g1, rwkv_g2=rwkv_g2, rwkv_k_k=rwkv_k_k, rwkv_k_a=rwkv_k_a, rwkv_r_k=rwkv_r_k,
                  rwkv_w_rkv=rwkv_w_rkv, rwkv_w_o=rwkv_w_o.astype(BF16),
                  rwkv_lnx_g=rwkv_lnx_g, rwkv_lnx_b=rwkv_lnx_b,
                  mla_w_in=mla_w_in, mla_q_norm=mla_q_norm, mla_w_uq=mla_w_uq, mla_kv_norm=mla_kv_norm,
                  mla_w_ukv=mla_w_ukv, mla_w_o=mla_w_o.astype(BF16), pool_w=pool_w, pool_scale=pool_scale)
    w_gate, w_up, w_down = ffn_w_gate.astype(BF16), ffn_w_up.astype(BF16), ffn_w_down.astype(BF16)
    v_first = None
    for i in range(depth):
        p = dict(shared, norm_mix_pre=norm_mix_pre[i], norm_mix_post=norm_mix_post[i])
        kind, j = i % 3, i // 3
        if kind == 0:
            h, v_first = _rwkv_layer(h, j, v_first, p)
        elif kind == 1:
            h = _mla_layer(h, j, p)
        else:
            h = _pool_layer(h, j, p)
        if i == depth - 1:
            return _ffn_window(h, N_META, seq, norm_ffn_pre[i], norm_ffn_post[i], w_gate, w_up, w_down, i)
        h = _ffn(h.reshape(b * lp, d), norm_ffn_pre[i], norm_ffn_post[i], w_gate, w_up, w_down, i).reshape(b, lp, d)
```
